```python
import jax, jax.numpy as jnp
from jax import lax
import numpy as np

D_MODEL = 1024
BATCH = 8
SEQ = 2048
DEPTH = 1
DEC_BATCH = 128
DEC_SEQ = 1
PAST_LEN = 16384
PAGE_SIZE = 128

HG_HEADS = 8
HG_KEY = 128
HG_VAL = D_MODEL // HG_HEADS
HG_KW = HG_HEADS * HG_KEY
HG_VW = HG_HEADS * HG_VAL
HG_CHUNK = 32
POOL_WINDOWS = (2, 4, 8, 16)
POOL_GROUPS = 4
POOL_W = D_MODEL
POOL_GW = POOL_W // POOL_GROUPS
POOL_BUF = 15
N_EXPERTS = 32
TOP_K = 4
D_FF = D_MODEL
SWIGLU_LIMIT = 7.0
SWIGLU_ALPHA = 1.702
MOE_BLOCK = 128
PLE_DIM = 256
EPS = 1e-6
IN_COLS = 2 * HG_KW + 2 * HG_VW + POOL_W + 2 * D_MODEL

kernel_name = "hgrn2_pool_gated_moe_ple_step"


def rmsnorm(x, g):
    xf = x.astype(jnp.float32)
    xf = xf * lax.rsqrt(jnp.mean(xf * xf, axis=-1, keepdims=True) + EPS)
    return (xf * g.astype(jnp.float32)).astype(x.dtype)


def hgrn2_chunked(q, k, v, logf):
    B, T, H, K = q.shape
    V = v.shape[-1]
    C = min(HG_CHUNK, T)
    N = T // C
    causal = jnp.tril(jnp.ones((C, C), dtype=bool))

    def to_chunks(a):
        return jnp.moveaxis(a.reshape(B, N, C, H, a.shape[-1]), 1, 0)

    def chunk_step(S, xs):
        qc, kc, vc, lc = xs
        b = jnp.cumsum(lc, axis=1)
        b_last = b[:, -1]
        qe = qc * jnp.exp(b)
        ke = kc * jnp.exp(-b)
        scores = jnp.where(causal[None, None], jnp.einsum('bchk,bshk->bhcs', qe, ke), 0.0)
        o = jnp.einsum('bchk,bhkv->bchv', qe, S) + jnp.einsum('bhcs,bshv->bchv', scores, vc)
        S_new = jnp.exp(b_last)[..., None] * S + jnp.einsum(
            'bchk,bchv->bhkv', kc * jnp.exp(b_last[:, None] - b), vc)
        return S_new, o

    S0 = jnp.zeros((B, H, K, V), jnp.float32)
    S_fin, o = lax.scan(chunk_step, S0, (to_chunks(q), to_chunks(k), to_chunks(v), to_chunks(logf)))
    o = jnp.moveaxis(o, 0, 1).reshape(B, T, H, V)
    return o, S_fin


def hgrn2_recurrent(q, k, v, logf, S0):
    def step(S, xs):
        qt, kt, vt, lt = xs
        S = jnp.exp(lt)[..., None] * S + kt[..., None] * vt[..., None, :]
        return S, jnp.einsum('bhk,bhkv->bhv', qt, S)

    S_fin, o = lax.scan(step, S0, tuple(jnp.moveaxis(a, 1, 0) for a in (q, k, v, logf)))
    return jnp.moveaxis(o, 0, 1), S_fin


def pool_mixer(u, buf, start_pos, w_pool, pool_scale):
    B, T, _ = u.shape
    ext = jnp.concatenate([buf, u], axis=1).astype(jnp.float32)
    cs = jnp.concatenate([jnp.zeros((B, 1, POOL_W), jnp.float32), jnp.cumsum(ext, axis=1)], axis=1)
    pos = start_pos + jnp.arange(T)
    end = cs[:, POOL_BUF + 1:POOL_BUF + 1 + T]
    uf = u.astype(jnp.float32)
    outs = []
    for g, w in enumerate(POOL_WINDOWS):
        lo, hi = g * POOL_GW, (g + 1) * POOL_GW
        s = end[..., lo:hi] - cs[:, POOL_BUF + 1 - w:POOL_BUF + 1 - w + T, lo:hi]
        cnt = jnp.minimum(pos + 1, w).astype(jnp.float32)[None, :, None]
        d = s / cnt - uf[..., lo:hi]
        outs.append(jnp.einsum('btc,cd->btd', d, w_pool[g].astype(jnp.float32)))
    y = jnp.concatenate(outs, axis=-1) * pool_scale.astype(jnp.float32)
    new_buf = ext[:, -POOL_BUF:].astype(u.dtype)
    return y, new_buf


def moe(h, w_router, b_router, w_gate, b_gate, w_up, b_up, w_down, b_down):
    lead = h.shape[:-1]
    xt = h.reshape(-1, D_MODEL)
    T = xt.shape[0]
    logits = xt.astype(jnp.float32) @ w_router.astype(jnp.float32) + b_router.astype(jnp.float32)
    top_v, top_i = lax.top_k(logits, TOP_K)
    top_w = jax.nn.softmax(top_v, axis=-1)
    flat_e = top_i.reshape(-1)
    flat_w = top_w.reshape(-1)
    n_assign = T * TOP_K
    order = jnp.argsort(flat_e)
    e_sorted = flat_e[order]
    counts = jnp.bincount(flat_e, length=N_EXPERTS)
    padded = (counts + MOE_BLOCK - 1) // MOE_BLOCK * MOE_BLOCK
    start = jnp.cumsum(counts) - counts
    pstart = jnp.cumsum(padded) - padded
    dest = pstart[e_sorted] + (jnp.arange(n_assign) - start[e_sorted])
    n_blocks = -(-n_assign // MOE_BLOCK) + N_EXPERTS
    n_rows = n_blocks * MOE_BLOCK
    row_tok = jnp.full((n_rows,), T, jnp.int32).at[dest].set((order // TOP_K).astype(jnp.int32))
    row_w = jnp.zeros((n_rows,), jnp.float32).at[dest].set(flat_w[order])
    block_e = jnp.minimum(jnp.searchsorted(jnp.cumsum(padded), jnp.arange(n_blocks) * MOE_BLOCK,
                                           side='right'), N_EXPERTS - 1).astype(jnp.int32)
    x_pad = jnp.concatenate([xt, jnp.zeros((1, D_MODEL), xt.dtype)], axis=0)
    x_rows = x_pad[row_tok].reshape(n_blocks, MOE_BLOCK, D_MODEL)

    def expert_block(args):
        xb, e = args
        gate = jnp.minimum(xb @ w_gate[e] + b_gate[e], SWIGLU_LIMIT)
        up = jnp.clip(xb @ w_up[e] + b_up[e], -SWIGLU_LIMIT, SWIGLU_LIMIT)
        act = (up + 1.0) * gate * jax.nn.sigmoid(SWIGLU_ALPHA * gate)
        return act @ w_down[e] + b_down[e]

    y_rows = lax.map(expert_block, (x_rows, block_e)).reshape(n_rows, D_MODEL)
    y = jax.ops.segment_sum(y_rows * row_w[:, None].astype(y_rows.dtype), row_tok, num_segments=T + 1)[:T]
    return y.reshape(*lead, D_MODEL)


def decoder_layer(x, p, S0, pool_buf, start_pos, is_prompt, lb, g_mix, w_in, hg_norm, w_pool, pool_scale,
                  w_out, g_ffn, w_router, b_router, w_gate, b_gate, w_up, b_up, w_down, b_down,
                  g_ple, w_ple_gate, w_ple_proj):
    B, T, _ = x.shape
    h = rmsnorm(x, g_mix)
    z = jnp.einsum('btd,dc->btc', h, w_in)
    cuts = [HG_KW, 2 * HG_KW, 2 * HG_KW + HG_VW, 2 * HG_KW + 2 * HG_VW,
            2 * HG_KW + 2 * HG_VW + POOL_W, 2 * HG_KW + 2 * HG_VW + POOL_W + D_MODEL]
    q_raw, f_raw, i_raw, g_raw, u, gate_a, gate_b = jnp.split(z, cuts, axis=-1)
    q = jax.nn.silu(q_raw.astype(jnp.float32)).reshape(B, T, HG_HEADS, HG_KEY)
    f = lb + (1.0 - lb) * jax.nn.sigmoid(f_raw.astype(jnp.float32))
    k = (1.0 - f).reshape(B, T, HG_HEADS, HG_KEY)
    logf = jnp.log(f).reshape(B, T, HG_HEADS, HG_KEY)
    v = i_raw.astype(jnp.float32).reshape(B, T, HG_HEADS, HG_VAL)
    if is_prompt:
        o, S_new = hgrn2_chunked(q, k, v, logf)
    else:
        o, S_new = hgrn2_recurrent(q, k, v, logf, S0.astype(jnp.float32))
    o = rmsnorm(o, hg_norm).reshape(B, T, HG_VW) * jax.nn.silu(g_raw.astype(jnp.float32))
    y_pool, new_buf = pool_mixer(u, pool_buf, start_pos, w_pool, pool_scale)
    merged = jax.nn.sigmoid(gate_a.astype(jnp.float32)) * o + jax.nn.sigmoid(gate_b.astype(jnp.float32)) * y_pool
    x = x + jnp.einsum('btc,cd->btd', merged.astype(x.dtype), w_out)
    x = x + moe(rmsnorm(x, g_ffn), w_router, b_router, w_gate, b_gate, w_up, b_up, w_down, b_down)
    gate_p = jax.nn.sigmoid(jnp.einsum('btd,de->bte', rmsnorm(x, g_ple), w_ple_gate).astype(jnp.float32))
    x = x + (gate_p * jnp.einsum('btp,pd->btd', p, w_ple_proj).astype(jnp.float32)).astype(x.dtype)
    return x, S_new.astype(x.dtype), new_buf


def setup_inputs(seed: int = 0) -> dict:
    key = jax.random.key(seed)
    ks = iter(jax.random.split(key, 32))

    def nrm(shape, scale):
        return scale * jax.random.normal(next(ks), shape, jnp.float32)

    def gain(shape):
        return 1.0 + nrm(shape, 0.05)

    return {
        "x_prompt": nrm((BATCH, SEQ, D_MODEL), 1.0),
        "x_sample": nrm((DEC_BATCH, DEC_SEQ, D_MODEL), 1.0),
        "p_prompt": nrm((DEPTH, BATCH, SEQ, PLE_DIM), 1.0),
        "p_sample": nrm((DEPTH, DEC_BATCH, DEC_SEQ, PLE_DIM), 1.0),
        "state_hgrn": nrm((DEPTH, DEC_BATCH, HG_HEADS, HG_KEY, HG_VAL), 0.5),
        "state_pool": nrm((DEPTH, DEC_BATCH, POOL_BUF, POOL_W), 1.0),
        "g_mix": gain((DEPTH, D_MODEL)),
        "w_in": nrm((DEPTH, D_MODEL, IN_COLS), D_MODEL ** -0.5),
        "hg_lb_logits": nrm((DEPTH + 1, HG_KW), 0.1),
        "hg_norm": gain((DEPTH, HG_VAL)),
        "w_pool": nrm((DEPTH, POOL_GROUPS, POOL_GW, POOL_GW), POOL_GW ** -0.5),
        "pool_scale": gain((DEPTH, POOL_W)),
        "w_out": nrm((DEPTH, D_MODEL, D_MODEL), D_MODEL ** -0.5),
        "g_ffn": gain((DEPTH, D_MODEL)),
        "w_router": nrm((DEPTH, D_MODEL, N_EXPERTS), D_MODEL ** -0.5),
        "b_router": nrm((DEPTH, N_EXPERTS), 0.01),
        "w_gate": nrm((DEPTH, N_EXPERTS, D_MODEL, D_FF), D_MODEL ** -0.5),
        "b_gate": nrm((DEPTH, N_EXPERTS, D_FF), 0.01),
        "w_up": nrm((DEPTH, N_EXPERTS, D_MODEL, D_FF), D_MODEL ** -0.5),
        "b_up": nrm((DEPTH, N_EXPERTS, D_FF), 0.01),
        "w_down": nrm((DEPTH, N_EXPERTS, D_FF, D_MODEL), D_FF ** -0.5),
        "b_down": nrm((DEPTH, N_EXPERTS, D_MODEL), 0.01),
        "g_ple": gain((DEPTH, D_MODEL)),
        "w_ple_gate": nrm((DEPTH, D_MODEL, D_MODEL), D_MODEL ** -0.5),
        "w_ple_proj": nrm((DEPTH, PLE_DIM, D_MODEL), PLE_DIM ** -0.5),
        "g_final": gain((D_MODEL,)),
    }


def reference(x_prompt, x_sample, p_prompt, p_sample, state_hgrn, state_pool, g_mix, w_in, hg_lb_logits,
              hg_norm, w_pool, pool_scale, w_out, g_ffn, w_router, b_router, w_gate, b_gate, w_up, b_up,
              w_down, b_down, g_ple, w_ple_gate, w_ple_proj, g_final):
    lb_all = jnp.cumsum(jax.nn.softmax(hg_lb_logits.astype(jnp.float32), axis=0), axis=0)
    xp, xs = x_prompt, x_sample
    hg_p, pool_p, hg_s, pool_s = [], [], [], []
    for i in range(DEPTH):
        lw = (lb_all[i], g_mix[i], w_in[i], hg_norm[i], w_pool[i], pool_scale[i], w_out[i], g_ffn[i],
              w_router[i], b_router[i], w_gate[i], b_gate[i], w_up[i], b_up[i], w_down[i], b_down[i],
              g_ple[i], w_ple_gate[i], w_ple_proj[i])
        zero_buf = jnp.zeros((xp.shape[0], POOL_BUF, POOL_W), xp.dtype)
        xp, sp, bp = decoder_layer(xp, p_prompt[i], None, zero_buf, 0, True, *lw)
        xs, ss, bs = decoder_layer(xs, p_sample[i], state_hgrn[i], state_pool[i], PAST_LEN, False, *lw)
        hg_p.append(sp)
        pool_p.append(bp)
        hg_s.append(ss)
        pool_s.append(bs)
    y_prompt = rmsnorm(xp, g_final)
    y_sample = rmsnorm(xs, g_final)
    return (y_prompt, y_sample, jnp.stack(hg_p), jnp.stack(pool_p), jnp.stack(hg_s), jnp.stack(pool_s))
```

```python
import functools

import jax
import jax.numpy as jnp
from jax import lax
from jax.experimental import pallas as pl
from jax.experimental.pallas import tpu as pltpu

F32 = jnp.float32
BF16 = jnp.bfloat16

D_MODEL = 1024
HEADS = 8
HEAD_DIM = 128
CHUNK = 32
POOL_WINDOWS = (2, 4, 8, 16)
POOL_GW = D_MODEL // len(POOL_WINDOWS)
POOL_BUF = 15
POOL_CARRY = 16
N_EXPERTS = 32
TOP_K = 4
ROUTER_LANES = 128
SWIGLU_LIMIT = 7.0
SWIGLU_ALPHA = 1.702
EPS = 1e-6
PAST_LEN = 16384

SEG_Q, SEG_F, SEG_I, SEG_G, SEG_U, SEG_A, SEG_B = range(7)

MIXER_ROWS = 256
MOE_ROWS = 256
FINAL_ROWS = 512
VMEM_LIMIT = 56 * 1024 * 1024


def _rms(x, g):
    ms = jnp.mean(x * x, axis=-1, keepdims=True)
    return x * lax.rsqrt(ms + EPS) * g


def _sigmoid(x):
    return 1.0 / (1.0 + jnp.exp(-x))


def _dot(a, b):
    return jnp.dot(a, b, preferred_element_type=F32)


def _dot_nt(a, b):
    return lax.dot_general(a, b, (((1,), (1,)), ((), ())), preferred_element_type=F32)


def _dot_tn(a, b):
    return lax.dot_general(a, b, (((0,), (0,)), ((), ())), preferred_element_type=F32)


def _split_bf16(x):
    hi = x.astype(BF16)
    lo = (x - hi.astype(F32)).astype(BF16)
    return hi, lo


def _forget_lower_bound(lbl):
    m = jnp.max(lbl, axis=0, keepdims=True)
    e = jnp.exp(lbl - m)
    return e[0:1] / jnp.sum(e, axis=0, keepdims=True)


def _head_norm_gate(o, g_raw, hgn):
    parts = []
    for h in range(HEADS):
        oh = o[:, h * HEAD_DIM:(h + 1) * HEAD_DIM]
        parts.append(_rms(oh, hgn))
    return jnp.concatenate(parts, axis=-1) * (g_raw * _sigmoid(g_raw))


def _route(h2, wr_ref, br_ref, ti_ref, tw_ref):
    rows = h2.shape[0]
    h_hi, h_lo = _split_bf16(h2)
    w_hi, w_lo = _split_bf16(wr_ref[...])
    logits = _dot(h_hi, w_hi) + _dot(h_lo, w_hi) + _dot(h_hi, w_lo) + br_ref[...]
    lane = lax.broadcasted_iota(jnp.int32, (rows, ROUTER_LANES), 1)
    neg = jnp.float32(-jnp.inf)
    l = jnp.where(lane < N_EXPERTS, logits, neg)
    ti = jnp.zeros((rows, ROUTER_LANES), jnp.int32)
    tw = jnp.zeros((rows, ROUTER_LANES), F32)
    m0 = None
    denom = None
    es = []
    for j in range(TOP_K):
        m = jnp.max(l, axis=-1, keepdims=True)
        idx = jnp.min(jnp.where(l == m, lane, ROUTER_LANES), axis=-1, keepdims=True)
        l = jnp.where(lane == idx, neg, l)
        if j == 0:
            m0 = m
        e = jnp.exp(m - m0)
        es.append(e)
        denom = e if denom is None else denom + e
        ti = jnp.where(lane == j, idx, ti)
    for j in range(TOP_K):
        tw = jnp.where(lane == j, es[j] / denom, tw)
    ti_ref[...] = ti
    tw_ref[...] = tw


def _mixer_prompt_kernel(x_ref, gmix_ref, win_ref, lbl_ref, hgn_ref, wpool_ref, pscale_ref, wout_ref, gffn_ref,
                         wr_ref, br_ref,
                         x1_ref, h2_ref, ti_ref, tw_ref, sfin_ref, ptail_ref,
                         st_ref, uext_ref, qe_ref, ke_ref, kd_ref, v_ref, o_ref, p_ref):
    rows = MIXER_ROWS
    n_chunks = rows // CHUNK
    t = pl.program_id(1)
    last_t = pl.num_programs(1) - 1

    @pl.when(t == 0)
    def _():
        st_ref[...] = jnp.zeros_like(st_ref)
        uext_ref[0:POOL_CARRY, :] = jnp.zeros((POOL_CARRY, D_MODEL), F32)

    @pl.when(t > 0)
    def _():
        uext_ref[0:POOL_CARRY, :] = uext_ref[rows:rows + POOL_CARRY, :]

    x = x_ref[0]
    h = _rms(x, gmix_ref[...]).astype(BF16)

    def proj(seg):
        return _dot(h, win_ref[:, seg * D_MODEL:(seg + 1) * D_MODEL])

    lb = _forget_lower_bound(lbl_ref[...])
    q_raw = proj(SEG_Q)
    q = q_raw * _sigmoid(q_raw)
    f = lb + (1.0 - lb) * _sigmoid(proj(SEG_F))
    k = 1.0 - f
    row_in_chunk = lax.broadcasted_iota(jnp.int32, (rows, D_MODEL), 0) % CHUNK
    p = f
    s = 1
    while s < CHUNK:
        p = p * jnp.where(row_in_chunk >= s, pltpu.roll(p, s, axis=0), 1.0)
        s *= 2
    p3 = p.reshape(n_chunks, CHUNK, D_MODEL)
    plast = p3[:, CHUNK - 1:CHUNK, :]
    qe_ref[...] = (q * p).astype(BF16)
    ke_ref[...] = (k / p).astype(BF16)
    kd_ref[...] = (k * (plast / p3).reshape(rows, D_MODEL)).astype(BF16)
    p_ref[...] = p
    v_ref[...] = proj(SEG_I).astype(BF16)

    causal = (lax.broadcasted_iota(jnp.int32, (CHUNK, CHUNK), 0)
              >= lax.broadcasted_iota(jnp.int32, (CHUNK, CHUNK), 1))

    def chunk_step(c, carry):
        r0 = pl.multiple_of(c * CHUNK, CHUNK)
        pl_row = p_ref[pl.ds(r0 + CHUNK - 1, 1), :]
        for hd in range(HEADS):
            cols = slice(hd * HEAD_DIM, (hd + 1) * HEAD_DIM)
            qe = qe_ref[pl.ds(r0, CHUNK), cols]
            ke = ke_ref[pl.ds(r0, CHUNK), cols]
            kd = kd_ref[pl.ds(r0, CHUNK), cols]
            vv = v_ref[pl.ds(r0, CHUNK), cols]
            st = st_ref[hd]
            scores = jnp.where(causal, _dot_nt(qe, ke), 0.0)
            o_ref[pl.ds(r0, CHUNK), cols] = _dot_nt(qe, st.astype(BF16)) + _dot(scores.astype(BF16), vv)
            st_ref[hd] = st * pl_row[:, cols] + _dot_tn(vv, kd)
        return carry

    lax.fori_loop(0, n_chunks, chunk_step, 0)

    @pl.when(t == last_t)
    def _():
        for hd in range(HEADS):
            sfin_ref[0, hd] = st_ref[hd].T

    g_raw = proj(SEG_G)
    o = _head_norm_gate(o_ref[...], g_raw, hgn_ref[...])

    u = proj(SEG_U)
    uext_ref[POOL_CARRY:POOL_CARRY + rows, :] = u
    pos1 = t * rows + lax.broadcasted_iota(jnp.int32, (rows, 1), 0) + 1
    pooled = []
    for g, w in enumerate(POOL_WINDOWS):
        cols = slice(g * POOL_GW, (g + 1) * POOL_GW)
        sw = u[:, cols]
        for j in range(1, w):
            sw = sw + uext_ref[POOL_CARRY - j:POOL_CARRY - j + rows, cols]
        inv_cnt = 1.0 / jnp.minimum(pos1, w).astype(F32)
        dg = sw * inv_cnt - u[:, cols]
        pooled.append(_dot(dg.astype(BF16), wpool_ref[g]))
    y_pool = jnp.concatenate(pooled, axis=-1) * pscale_ref[...]

    @pl.when(t == last_t)
    def _():
        ptail_ref[0] = uext_ref[rows + POOL_CARRY - POOL_BUF:rows + POOL_CARRY, :]

    merged = _sigmoid(proj(SEG_A)) * o + _sigmoid(proj(SEG_B)) * y_pool
    x1 = x + _dot(merged.astype(BF16), wout_ref[...])
    x1_ref[0] = x1
    h2 = _rms(x1, gffn_ref[...])
    h2_ref[...] = h2
    _route(h2, wr_ref, br_ref, ti_ref, tw_ref)


def _const_spec(shape):
    zeros = (0,) * len(shape)
    return pl.BlockSpec(shape, lambda *_: zeros, pipeline_mode=pl.Buffered(1))


def _mixer_prompt(x, gmix, win, lbl, hgn, wpool, pscale, wout, gffn, wr, br):
    b, t, _ = x.shape
    rows = MIXER_ROWS
    nt = t // rows
    tok_spec = pl.BlockSpec((1, rows, D_MODEL), lambda i, j: (i, j, 0))
    flat_spec = pl.BlockSpec((rows, D_MODEL), lambda i, j: (i * nt + j, 0))
    lane_spec = pl.BlockSpec((rows, ROUTER_LANES), lambda i, j: (i * nt + j, 0))
    return pl.pallas_call(
        _mixer_prompt_kernel,
        grid=(b, nt),
        in_specs=[
            tok_spec,
            _const_spec((1, D_MODEL)),
            _const_spec(win.shape),
            _const_spec(lbl.shape),
            _const_spec((1, HEAD_DIM)),
            _const_spec(wpool.shape),
            _const_spec((1, D_MODEL)),
            _const_spec(wout.shape),
            _const_spec((1, D_MODEL)),
            _const_spec(wr.shape),
            _const_spec(br.shape),
        ],
        out_specs=[
            tok_spec,
            flat_spec,
            lane_spec,
            lane_spec,
            pl.BlockSpec((1, HEADS, HEAD_DIM, HEAD_DIM), lambda i, j: (i, 0, 0, 0)),
            pl.BlockSpec((1, POOL_BUF, D_MODEL), lambda i, j: (i, 0, 0)),
        ],
        out_shape=[
            jax.ShapeDtypeStruct(x.shape, F32),
            jax.ShapeDtypeStruct((b * t, D_MODEL), F32),
            jax.ShapeDtypeStruct((b * t, ROUTER_LANES), jnp.int32),
            jax.ShapeDtypeStruct((b * t, ROUTER_LANES), F32),
            jax.ShapeDtypeStruct((b, HEADS, HEAD_DIM, HEAD_DIM), F32),
            jax.ShapeDtypeStruct((b, POOL_BUF, D_MODEL), F32),
        ],
        scratch_shapes=[
            pltpu.VMEM((HEADS, HEAD_DIM, HEAD_DIM), F32),
            pltpu.VMEM((rows + POOL_CARRY, D_MODEL), F32),
            pltpu.VMEM((rows, D_MODEL), BF16),
            pltpu.VMEM((rows, D_MODEL), BF16),
            pltpu.VMEM((rows, D_MODEL), BF16),
            pltpu.VMEM((rows, D_MODEL), BF16),
            pltpu.VMEM((rows, D_MODEL), F32),
            pltpu.VMEM((rows, D_MODEL), F32),
        ],
        compiler_params=pltpu.CompilerParams(dimension_semantics=("arbitrary", "arbitrary"),
                                             vmem_limit_bytes=VMEM_LIMIT),
        name="mixer_prompt",
    )(x, gmix, win, lbl, hgn, wpool, pscale, wout, gffn, wr, br)


SAMPLE_GROUP = 8


def _mixer_sample_kernel(pool_cnt, x_ref, gmix_ref, win_ref, wqft_ref, lblt_ref, hgn_ref, wpool_ref, pscale_ref,
                         wout_ref, gffn_ref, wr_ref, br_ref, s_ref, pbuf_ref,
                         x1_ref, h2_ref, ti_ref, tw_ref, snew_ref, pnew_ref,
                         h_ref, znat_ref, o_ref, bsum_ref):
    step = pl.program_id(0)
    last = pl.num_programs(0) - 1
    r0 = pl.multiple_of(step * SAMPLE_GROUP, SAMPLE_GROUP)

    @pl.when(step == 0)
    def _():
        hf = _rms(x_ref[...], gmix_ref[...])
        h_ref[...] = hf
        h = hf.astype(BF16)
        for i, seg in enumerate((SEG_I, SEG_G, SEG_U, SEG_A, SEG_B)):
            znat_ref[:, i * D_MODEL:(i + 1) * D_MODEL] = _dot(h, win_ref[:, seg * D_MODEL:(seg + 1) * D_MODEL])

    hg = h_ref[pl.ds(r0, SAMPLE_GROUP), :].astype(BF16)
    qft = _dot_nt(wqft_ref[...], hg)
    lbl = lblt_ref[...]
    m = jnp.max(lbl, axis=1, keepdims=True)
    e = jnp.exp(lbl - m)
    lb = e[:, 0:1] / jnp.sum(e, axis=1, keepdims=True)
    q_raw = qft[0:D_MODEL]
    qt = q_raw * _sigmoid(q_raw)
    ft = lb + (1.0 - lb) * _sigmoid(qft[D_MODEL:2 * D_MODEL])
    kt = 1.0 - ft

    for j in range(SAMPLE_GROUP):
        v_row = znat_ref[pl.ds(r0 + j, 1), 0:D_MODEL]
        o_parts = []
        for hd in range(HEADS):
            rs = slice(hd * HEAD_DIM, (hd + 1) * HEAD_DIM)
            s_new = ft[rs, j:j + 1] * s_ref[j, hd] + kt[rs, j:j + 1] * v_row[:, rs]
            snew_ref[j, hd] = s_new
            o_parts.append(jnp.sum(qt[rs, j:j + 1] * s_new, axis=0, keepdims=True))
        o_ref[pl.ds(r0 + j, 1), :] = jnp.concatenate(o_parts, axis=-1)

    u_g = znat_ref[pl.ds(r0, SAMPLE_GROUP), 2 * D_MODEL:3 * D_MODEL]
    sums = []
    for g, w in enumerate(POOL_WINDOWS):
        acc = jnp.zeros((SAMPLE_GROUP, POOL_GW), F32)
        for j in range(1, w):
            row = POOL_BUF - j
            acc = acc + pbuf_ref[:, row * D_MODEL + g * POOL_GW:row * D_MODEL + (g + 1) * POOL_GW]
        sums.append(acc)
    bsum_ref[pl.ds(r0, SAMPLE_GROUP), :] = jnp.concatenate(sums, axis=-1)
    pnew_ref[:, 0:(POOL_BUF - 1) * D_MODEL] = pbuf_ref[:, D_MODEL:POOL_BUF * D_MODEL]
    pnew_ref[:, (POOL_BUF - 1) * D_MODEL:POOL_BUF * D_MODEL] = u_g

    @pl.when(step == last)
    def _():
        x = x_ref[...]
        g_raw = znat_ref[:, D_MODEL:2 * D_MODEL]
        u = znat_ref[:, 2 * D_MODEL:3 * D_MODEL]
        o = _head_norm_gate(o_ref[...], g_raw, hgn_ref[...])
        sw = bsum_ref[...] + u
        pooled = []
        for g, w in enumerate(POOL_WINDOWS):
            cols = slice(g * POOL_GW, (g + 1) * POOL_GW)
            dg = sw[:, cols] * (1.0 / pool_cnt[g]) - u[:, cols]
            pooled.append(_dot(dg.astype(BF16), wpool_ref[g]))
        y_pool = jnp.concatenate(pooled, axis=-1) * pscale_ref[...]
        merged = (_sigmoid(znat_ref[:, 3 * D_MODEL:4 * D_MODEL]) * o
                  + _sigmoid(znat_ref[:, 4 * D_MODEL:5 * D_MODEL]) * y_pool)
        x1 = x + _dot(merged.astype(BF16), wout_ref[...])
        x1_ref[...] = x1
        h2 = _rms(x1, gffn_ref[...])
        h2_ref[...] = h2
        _route(h2, wr_ref, br_ref, ti_ref, tw_ref)


def _mixer_sample(x, gmix, win, wqft, lblt, hgn, wpool, pscale, wout, gffn, wr, br, state, pbuf, start_pos):
    n = x.shape[0]
    steps = n // SAMPLE_GROUP
    pool_cnt = tuple(float(min(start_pos + 1, w)) for w in POOL_WINDOWS)
    pbuf2 = pbuf.reshape(n, POOL_BUF * D_MODEL)
    full = _const_spec
    out = pl.pallas_call(
        functools.partial(_mixer_sample_kernel, pool_cnt),
        grid=(steps,),
        in_specs=[
            full((n, D_MODEL)),
            full((1, D_MODEL)),
            full(win.shape),
            full(wqft.shape),
            full(lblt.shape),
            full((1, HEAD_DIM)),
            full(wpool.shape),
            full((1, D_MODEL)),
            full(wout.shape),
            full((1, D_MODEL)),
            full(wr.shape),
            full(br.shape),
            pl.BlockSpec((SAMPLE_GROUP, HEADS, HEAD_DIM, HEAD_DIM), lambda i: (i, 0, 0, 0)),
            pl.BlockSpec((SAMPLE_GROUP, POOL_BUF * D_MODEL), lambda i: (i, 0)),
        ],
        out_specs=[
            pl.BlockSpec((n, D_MODEL), lambda i: (0, 0)),
            pl.BlockSpec((n, D_MODEL), lambda i: (0, 0)),
            pl.BlockSpec((n, ROUTER_LANES), lambda i: (0, 0)),
            pl.BlockSpec((n, ROUTER_LANES), lambda i: (0, 0)),
            pl.BlockSpec((SAMPLE_GROUP, HEADS, HEAD_DIM, HEAD_DIM), lambda i: (i, 0, 0, 0)),
            pl.BlockSpec((SAMPLE_GROUP, POOL_BUF * D_MODEL), lambda i: (i, 0)),
        ],
        out_shape=[
            jax.ShapeDtypeStruct((n, D_MODEL), F32),
            jax.ShapeDtypeStruct((n, D_MODEL), F32),
            jax.ShapeDtypeStruct((n, ROUTER_LANES), jnp.int32),
            jax.ShapeDtypeStruct((n, ROUTER_LANES), F32),
            jax.ShapeDtypeStruct(state.shape, F32),
            jax.ShapeDtypeStruct(pbuf2.shape, F32),
        ],
        scratch_shapes=[
            pltpu.VMEM((n, D_MODEL), F32),
            pltpu.VMEM((n, 5 * D_MODEL), F32),
            pltpu.VMEM((n, D_MODEL), F32),
            pltpu.VMEM((n, D_MODEL), F32),
        ],
        compiler_params=pltpu.CompilerParams(dimension_semantics=("arbitrary",), vmem_limit_bytes=VMEM_LIMIT),
        name="mixer_sample",
    )(x, gmix, win, wqft, lblt, hgn, wpool, pscale, wout, gffn, wr, br, state, pbuf2)
    x1, h2, ti, tw, snew, pnew = out
    return x1, h2, ti, tw, snew, pnew.reshape(n, POOL_BUF, D_MODEL)


def _moe_kernel(be_ref, np_ref, nv_ref, nu_ref, src_ref, src_next_ref, dst_ref, h2p_hbm, h2s_hbm,
                wg_ref, bg_ref, wu_ref, bu_ref, wd_ref, bd_ref, y_hbm, xbuf, ybuf, wgb, wub, wdb, gsem, ssem):
    rows = MOE_ROWS
    i = pl.program_id(0)
    n_used = nu_ref[0]
    slot = i % 2

    def gather_copy(src_hbm, tok, r, s):
        return pltpu.make_async_copy(src_hbm.at[pl.ds(tok, 1), :], xbuf.at[s, pl.ds(r, 1), :], gsem.at[s])

    def scatter_copy(dst, r, n, s):
        return pltpu.make_async_copy(ybuf.at[s, pl.ds(r, n), :], y_hbm.at[pl.ds(dst, n), :], ssem.at[s])

    def start_gather(idx_ref, blk, s):
        def from_prompt(r, c):
            gather_copy(h2p_hbm, idx_ref[0, 0, r], r, s).start()
            return c

        def from_sample(r, c):
            gather_copy(h2s_hbm, idx_ref[0, 0, r], r, s).start()
            return c

        lax.fori_loop(0, np_ref[blk], from_prompt, 0)
        lax.fori_loop(np_ref[blk], rows, from_sample, 0)

    def wait_gather(s):
        pltpu.make_async_copy(h2p_hbm.at[pl.ds(0, rows), :], xbuf.at[s], gsem.at[s]).wait()

    def wait_scatter(blk, s):
        n = nv_ref[blk]
        size = rows
        while size >= 1:
            @pl.when((n & size) != 0)
            def _():
                scatter_copy(0, 0, size, s).wait()
            size //= 2

    @pl.when(i == 0)
    def _():
        start_gather(src_ref, 0, 0)

    @pl.when(i + 1 < n_used)
    def _():
        start_gather(src_next_ref, i + 1, 1 - slot)

    @pl.when(i < n_used)
    def _():
        first_of_expert = jnp.logical_or(i == 0, be_ref[i] != be_ref[jnp.maximum(i - 1, 0)])

        @pl.when(first_of_expert)
        def _():
            wgb[...] = wg_ref[0].astype(BF16)
            wub[...] = wu_ref[0].astype(BF16)
            wdb[...] = wd_ref[0].astype(BF16)

        wait_gather(slot)

        @pl.when(i >= 2)
        def _():
            wait_scatter(i - 2, slot)

        xb = xbuf[slot].astype(BF16)
        gate = jnp.minimum(_dot(xb, wgb[...]) + bg_ref[0], SWIGLU_LIMIT)
        up = jnp.clip(_dot(xb, wub[...]) + bu_ref[0], -SWIGLU_LIMIT, SWIGLU_LIMIT)
        act = (up + 1.0) * gate * _sigmoid(SWIGLU_ALPHA * gate)
        ybuf[slot] = _dot(act.astype(BF16), wdb[...]) + bd_ref[0]

        def body(r, c):
            scatter_copy(dst_ref[0, 0, r], r, 1, slot).start()
            return c
        lax.fori_loop(0, nv_ref[i], body, 0)

        @pl.when(i == n_used - 1)
        def _():
            wait_scatter(i, slot)

            @pl.when(i >= 1)
            def _():
                wait_scatter(i - 1, 1 - slot)


def _moe(h2_p, h2_s, routing, wg, bg, wu, bu, wd, bd, y_rows_total):
    rows = MOE_ROWS
    block_e, n_prompt_rows, n_valid_rows, n_used, row_src, row_dst = routing
    n_blocks = row_src.shape[0]
    w_spec = pl.BlockSpec((1, D_MODEL, D_MODEL), lambda i, be, *_: (be[i], 0, 0))
    b_spec = pl.BlockSpec((1, 1, D_MODEL), lambda i, be, *_: (be[i], 0, 0))
    idx_spec = pl.BlockSpec((1, 1, rows), lambda i, *_: (i, 0, 0), memory_space=pltpu.SMEM)
    idx_next_spec = pl.BlockSpec((1, 1, rows), lambda i, *_: (jnp.minimum(i + 1, n_blocks - 1), 0, 0),
                                 memory_space=pltpu.SMEM)
    grid_spec = pltpu.PrefetchScalarGridSpec(
        num_scalar_prefetch=4,
        grid=(n_blocks,),
        in_specs=[idx_spec, idx_next_spec, idx_spec, pl.BlockSpec(memory_space=pl.ANY),
                  pl.BlockSpec(memory_space=pl.ANY), w_spec, b_spec, w_spec, b_spec, w_spec, b_spec],
        out_specs=pl.BlockSpec(memory_space=pl.ANY),
        scratch_shapes=[
            pltpu.VMEM((2, rows, D_MODEL), F32),
            pltpu.VMEM((2, rows, D_MODEL), F32),
            pltpu.VMEM((D_MODEL, D_MODEL), BF16),
            pltpu.VMEM((D_MODEL, D_MODEL), BF16),
            pltpu.VMEM((D_MODEL, D_MODEL), BF16),
            pltpu.SemaphoreType.DMA((2,)),
            pltpu.SemaphoreType.DMA((2,)),
        ],
    )
    return pl.pallas_call(
        _moe_kernel,
        grid_spec=grid_spec,
        out_shape=jax.ShapeDtypeStruct((y_rows_total, D_MODEL), F32),
        compiler_params=pltpu.CompilerParams(dimension_semantics=("arbitrary",), vmem_limit_bytes=VMEM_LIMIT),
        name="moe_experts",
    )(block_e, n_prompt_rows, n_valid_rows, n_used, row_src, row_src, row_dst, h2_p, h2_s, wg, bg, wu, bu, wd, bd)


def _moe_routing(top_i, n_prompt, n_tokens):
    rows = MOE_ROWS
    n_assign = n_tokens * TOP_K
    n_blocks = -(-n_assign // rows) + N_EXPERTS
    flat_e = top_i.reshape(-1)
    order = jnp.argsort(flat_e, stable=True).astype(jnp.int32)
    counts = jnp.sum((flat_e[:, None] == jnp.arange(N_EXPERTS, dtype=jnp.int32)[None, :]).astype(jnp.int32), axis=0)
    padded = (counts + rows - 1) // rows * rows
    start = jnp.cumsum(counts) - counts
    pend = jnp.cumsum(padded)
    pstart = pend - padded
    blk = jnp.arange(n_blocks, dtype=jnp.int32)
    block_e = jnp.minimum(jnp.searchsorted(pend, blk * rows, side='right'), N_EXPERTS - 1).astype(jnp.int32)
    n_used = (pend[-1] // rows).astype(jnp.int32).reshape(1)
    r = jnp.arange(n_blocks * rows, dtype=jnp.int32)
    e_r = block_e[r // rows]
    off = r - pstart[e_r]
    valid = jnp.logical_and(off < counts[e_r], r < pend[-1])
    src = order[jnp.clip(start[e_r] + off, 0, n_assign - 1)]
    tok = src // TOP_K
    from_prompt = jnp.logical_and(valid, tok < n_prompt)
    row_src = jnp.where(valid, jnp.where(from_prompt, tok, tok - n_prompt), 0)
    row_dst = jnp.where(valid, (src % TOP_K) * n_tokens + tok, 0)
    shape2 = (n_blocks, rows)
    n_prompt_rows = jnp.sum(from_prompt.reshape(shape2).astype(jnp.int32), axis=1)
    n_valid_rows = jnp.sum(valid.reshape(shape2).astype(jnp.int32), axis=1)
    shape3 = (n_blocks, 1, rows)
    return (block_e, n_prompt_rows, n_valid_rows, n_used, row_src.astype(jnp.int32).reshape(shape3),
            row_dst.astype(jnp.int32).reshape(shape3))


def _final_kernel(x1_ref, y_ref, tw_ref, p_ref, gple_ref, wpg_ref, wpp_ref, gfin_ref, out_ref):
    x = x1_ref[...]
    tw = tw_ref[...]
    for j in range(TOP_K):
        x = x + tw[:, j:j + 1] * y_ref[j]
    gate = _sigmoid(_dot(_rms(x, gple_ref[...]).astype(BF16), wpg_ref[...]))
    x = x + gate * _dot(p_ref[...].astype(BF16), wpp_ref[...])
    out_ref[...] = _rms(x, gfin_ref[...])


def _final(x1, y4, tw, p, gple, wpg, wpp, gfin, rows, row_offset):
    n = x1.shape[0]
    off = row_offset // rows
    assert off * rows == row_offset and n % rows == 0
    ple = p.shape[-1]
    return pl.pallas_call(
        _final_kernel,
        grid=(n // rows,),
        in_specs=[
            pl.BlockSpec((rows, D_MODEL), lambda i: (i, 0)),
            pl.BlockSpec((TOP_K, rows, D_MODEL), lambda i: (0, i + off, 0)),
            pl.BlockSpec((rows, ROUTER_LANES), lambda i: (i, 0)),
            pl.BlockSpec((rows, ple), lambda i: (i, 0)),
            _const_spec((1, D_MODEL)),
            _const_spec(wpg.shape),
            _const_spec(wpp.shape),
            _const_spec((1, D_MODEL)),
        ],
        out_specs=pl.BlockSpec((rows, D_MODEL), lambda i: (i, 0)),
        out_shape=jax.ShapeDtypeStruct((n, D_MODEL), F32),
        compiler_params=pltpu.CompilerParams(dimension_semantics=("arbitrary",), vmem_limit_bytes=VMEM_LIMIT),
        name="final",
    )(x1, y4, tw, p, gple, wpg, wpp, gfin)


def kernel(x_prompt, x_sample, p_prompt, p_sample, state_hgrn, state_pool, g_mix, w_in, hg_lb_logits, hg_norm, w_pool,
           pool_scale, w_out, g_ffn, w_router, b_router, w_gate, b_gate, w_up, b_up, w_down, b_down, g_ple,
           w_ple_gate, w_ple_proj, g_final):
    depth = w_in.shape[0]
    assert depth == 1, "single-layer step"
    b, t, _ = x_prompt.shape
    ns = x_sample.shape[0]
    assert x_sample.shape[1] == 1
    n_prompt = b * t
    n_tok = n_prompt + ns

    row = lambda a: a.reshape(1, -1)
    win = w_in[0].astype(BF16)
    wqft = win[:, :2 * D_MODEL].T
    lbl = hg_lb_logits.astype(F32)
    wpool = w_pool[0].astype(BF16)
    wout = w_out[0].astype(BF16)
    wr = jnp.pad(w_router[0], ((0, 0), (0, ROUTER_LANES - N_EXPERTS)))
    br = jnp.pad(b_router[0], (0, ROUTER_LANES - N_EXPERTS)).reshape(1, -1)
    shared = (row(hg_norm[0]), wpool, row(pool_scale[0]), wout, row(g_ffn[0]), wr, br)

    x1_p, h2_p, ti_p, tw_p, s_p, pool_p = _mixer_prompt(x_prompt, row(g_mix[0]), win, lbl, *shared)
    x1_s, h2_s, ti_s, tw_s, s_s, pool_s = _mixer_sample(
        x_sample.reshape(ns, D_MODEL), row(g_mix[0]), win, wqft, lbl.T, *shared,
        state_hgrn[0], state_pool[0], start_pos=PAST_LEN)

    top_i = jnp.concatenate([ti_p[:, :TOP_K], ti_s[:, :TOP_K]], axis=0)
    routing = _moe_routing(top_i, n_prompt, n_tok)
    e3 = lambda a: a[0].reshape(N_EXPERTS, 1, D_MODEL)
    y = _moe(h2_p, h2_s, routing, w_gate[0], e3(b_gate), w_up[0], e3(b_up), w_down[0], e3(b_down), TOP_K * n_tok)
    y4 = y.reshape(TOP_K, n_tok, D_MODEL)

    wpg = w_ple_gate[0].astype(BF16)
    wpp = w_ple_proj[0].astype(BF16)
    fin = (row(g_ple[0]), wpg, wpp, row(g_final))
    y_p = _final(x1_p.reshape(n_prompt, D_MODEL), y4, tw_p, p_prompt[0].reshape(n_prompt, -1), *fin,
                 rows=FINAL_ROWS, row_offset=0)
    y_s = _final(x1_s, y4, tw_s, p_sample[0].reshape(ns, -1), *fin, rows=ns, row_offset=n_prompt)

    return (y_p.reshape(b, t, D_MODEL), y_s.reshape(ns, 1, D_MODEL), s_p[None], pool_p[None], s_s[None],
            pool_s[None])
```

```python
import functools

import jax
import jax.numpy as jnp
from jax import lax
from jax.experimental import pallas as pl
from jax.experimental.pallas import tpu as pltpu

F32 = jnp.float32
BF16 = jnp.bfloat16

D_MODEL = 1024
HEADS = 8
HEAD_DIM = 128
CHUNK = 32
POOL_WINDOWS = (2, 4, 8, 16)
POOL_GW = D_MODEL // len(POOL_WINDOWS)
POOL_BUF = 15
POOL_CARRY = 16
N_EXPERTS = 32
TOP_K = 4
ROUTER_LANES = 128
SWIGLU_LIMIT = 7.0
SWIGLU_ALPHA = 1.702
EPS = 1e-6
PAST_LEN = 16384

SEG_Q, SEG_F, SEG_I, SEG_G, SEG_U, SEG_A, SEG_B = range(7)

MIXER_ROWS = 256
MOE_ROWS = 256
FINAL_ROWS = 512
VMEM_LIMIT = 56 * 1024 * 1024


def _rms(x, g):
    ms = jnp.mean(x * x, axis=-1, keepdims=True)
    return x * lax.rsqrt(ms + EPS) * g


def _sigmoid(x):
    return 1.0 / (1.0 + jnp.exp(-x))


def _dot(a, b):
    return jnp.dot(a, b, preferred_element_type=F32)


def _dot_nt(a, b):
    return lax.dot_general(a, b, (((1,), (1,)), ((), ())), preferred_element_type=F32)


def _dot_tn(a, b):
    return lax.dot_general(a, b, (((0,), (0,)), ((), ())), preferred_element_type=F32)


def _split_bf16(x):
    hi = x.astype(BF16)
    lo = (x - hi.astype(F32)).astype(BF16)
    return hi, lo


def _forget_lower_bound(lbl):
    m = jnp.max(lbl, axis=0, keepdims=True)
    e = jnp.exp(lbl - m)
    return e[0:1] / jnp.sum(e, axis=0, keepdims=True)


def _head_norm_gate(o, g_raw, hgn):
    parts = []
    for h in range(HEADS):
        oh = o[:, h * HEAD_DIM:(h + 1) * HEAD_DIM]
        parts.append(_rms(oh, hgn))
    return jnp.concatenate(parts, axis=-1) * (g_raw * _sigmoid(g_raw))


def _route(h2, wr_ref, br_ref, ti_ref, tw_ref):
    rows = h2.shape[0]
    h_hi, h_lo = _split_bf16(h2)
    w_hi, w_lo = _split_bf16(wr_ref[...])
    logits = _dot(h_hi, w_hi) + _dot(h_lo, w_hi) + _dot(h_hi, w_lo) + br_ref[...]
    lane = lax.broadcasted_iota(jnp.int32, (rows, ROUTER_LANES), 1)
    neg = jnp.float32(-jnp.inf)
    l = jnp.where(lane < N_EXPERTS, logits, neg)
    ti = jnp.zeros((rows, ROUTER_LANES), jnp.int32)
    tw = jnp.zeros((rows, ROUTER_LANES), F32)
    m0 = None
    denom = None
    es = []
    for j in range(TOP_K):
        m = jnp.max(l, axis=-1, keepdims=True)
        idx = jnp.min(jnp.where(l == m, lane, ROUTER_LANES), axis=-1, keepdims=True)
        l = jnp.where(lane == idx, neg, l)
        if j == 0:
            m0 = m
        e = jnp.exp(m - m0)
        es.append(e)
        denom = e if denom is None else denom + e
        ti = jnp.where(lane == j, idx, ti)
    for j in range(TOP_K):
        tw = jnp.where(lane == j, es[j] / denom, tw)
    ti_ref[...] = ti
    tw_ref[...] = tw


def _mixer_prompt_kernel(x_ref, gmix_ref, win_ref, lbl_ref, hgn_ref, wpool_ref, pscale_ref, wout_ref, gffn_ref,
                         wr_ref, br_ref,
                         x1_ref, h2_ref, ti_ref, tw_ref, sfin_ref, ptail_ref,
                         st_ref, uext_ref, qe_ref, ke_ref, kd_ref, v_ref, o_ref, p_ref):
    rows = MIXER_ROWS
    n_chunks = rows // CHUNK
    t = pl.program_id(1)
    last_t = pl.num_programs(1) - 1

    @pl.when(t == 0)
    def _():
        st_ref[...] = jnp.zeros_like(st_ref)
        uext_ref[0:POOL_CARRY, :] = jnp.zeros((POOL_CARRY, D_MODEL), F32)

    @pl.when(t > 0)
    def _():
        uext_ref[0:POOL_CARRY, :] = uext_ref[rows:rows + POOL_CARRY, :]

    x = x_ref[0]
    h = _rms(x, gmix_ref[...]).astype(BF16)

    def proj(seg):
        return _dot(h, win_ref[:, seg * D_MODEL:(seg + 1) * D_MODEL])

    lb = _forget_lower_bound(lbl_ref[...])
    q_raw = proj(SEG_Q)
    q = q_raw * _sigmoid(q_raw)
    f = lb + (1.0 - lb) * _sigmoid(proj(SEG_F))
    k = 1.0 - f
    row_in_chunk = lax.broadcasted_iota(jnp.int32, (rows, D_MODEL), 0) % CHUNK
    p = f
    s = 1
    while s < CHUNK:
        p = p * jnp.where(row_in_chunk >= s, pltpu.roll(p, s, axis=0), 1.0)
        s *= 2
    p3 = p.reshape(n_chunks, CHUNK, D_MODEL)
    plast = p3[:, CHUNK - 1:CHUNK, :]
    qe_ref[...] = (q * p).astype(BF16)
    ke_ref[...] = (k / p).astype(BF16)
    kd_ref[...] = (k * (plast / p3).reshape(rows, D_MODEL)).astype(BF16)
    p_ref[...] = p
    v_ref[...] = proj(SEG_I).astype(BF16)

    causal = (lax.broadcasted_iota(jnp.int32, (CHUNK, CHUNK), 0)
              >= lax.broadcasted_iota(jnp.int32, (CHUNK, CHUNK), 1))

    for hd in range(HEADS):
        cols = slice(hd * HEAD_DIM, (hd + 1) * HEAD_DIM)
        st = st_ref[hd]
        for c in range(n_chunks):
            rs = slice(c * CHUNK, (c + 1) * CHUNK)
            qe = qe_ref[rs, cols]
            ke = ke_ref[rs, cols]
            kd = kd_ref[rs, cols]
            vv = v_ref[rs, cols]
            decay = p_ref[(c + 1) * CHUNK - 1:(c + 1) * CHUNK, cols]
            scores = jnp.where(causal, _dot_nt(qe, ke), 0.0)
            o_ref[rs, cols] = _dot_nt(qe, st.astype(BF16)) + _dot(scores.astype(BF16), vv)
            st = st * decay + _dot_tn(vv, kd)
        st_ref[hd] = st

    @pl.when(t == last_t)
    def _():
        for hd in range(HEADS):
            sfin_ref[0, hd] = st_ref[hd].T

    g_raw = proj(SEG_G)
    o = _head_norm_gate(o_ref[...], g_raw, hgn_ref[...])

    u = proj(SEG_U)
    uext_ref[POOL_CARRY:POOL_CARRY + rows, :] = u
    pos1 = t * rows + lax.broadcasted_iota(jnp.int32, (rows, 1), 0) + 1
    pooled = []
    for g, w in enumerate(POOL_WINDOWS):
        cols = slice(g * POOL_GW, (g + 1) * POOL_GW)
        sw = u[:, cols]
        for j in range(1, w):
            sw = sw + uext_ref[POOL_CARRY - j:POOL_CARRY - j + rows, cols]
        inv_cnt = 1.0 / jnp.minimum(pos1, w).astype(F32)
        dg = sw * inv_cnt - u[:, cols]
        pooled.append(_dot(dg.astype(BF16), wpool_ref[g]))
    y_pool = jnp.concatenate(pooled, axis=-1) * pscale_ref[...]

    @pl.when(t == last_t)
    def _():
        ptail_ref[0] = uext_ref[rows + POOL_CARRY - POOL_BUF:rows + POOL_CARRY, :]

    merged = _sigmoid(proj(SEG_A)) * o + _sigmoid(proj(SEG_B)) * y_pool
    x1 = x + _dot(merged.astype(BF16), wout_ref[...])
    x1_ref[0] = x1
    h2 = _rms(x1, gffn_ref[...])
    h2_ref[...] = h2
    _route(h2, wr_ref, br_ref, ti_ref, tw_ref)


def _const_spec(shape):
    zeros = (0,) * len(shape)
    return pl.BlockSpec(shape, lambda *_: zeros, pipeline_mode=pl.Buffered(1))


def _mixer_prompt(x, gmix, win, lbl, hgn, wpool, pscale, wout, gffn, wr, br):
    b, t, _ = x.shape
    rows = MIXER_ROWS
    nt = t // rows
    tok_spec = pl.BlockSpec((1, rows, D_MODEL), lambda i, j: (i, j, 0))
    flat_spec = pl.BlockSpec((rows, D_MODEL), lambda i, j: (i * nt + j, 0))
    lane_spec = pl.BlockSpec((rows, ROUTER_LANES), lambda i, j: (i * nt + j, 0))
    return pl.pallas_call(
        _mixer_prompt_kernel,
        grid=(b, nt),
        in_specs=[
            tok_spec,
            _const_spec((1, D_MODEL)),
            _const_spec(win.shape),
            _const_spec(lbl.shape),
            _const_spec((1, HEAD_DIM)),
            _const_spec(wpool.shape),
            _const_spec((1, D_MODEL)),
            _const_spec(wout.shape),
            _const_spec((1, D_MODEL)),
            _const_spec(wr.shape),
            _const_spec(br.shape),
        ],
        out_specs=[
            tok_spec,
            flat_spec,
            lane_spec,
            lane_spec,
            pl.BlockSpec((1, HEADS, HEAD_DIM, HEAD_DIM), lambda i, j: (i, 0, 0, 0)),
            pl.BlockSpec((1, POOL_BUF, D_MODEL), lambda i, j: (i, 0, 0)),
        ],
        out_shape=[
            jax.ShapeDtypeStruct(x.shape, F32),
            jax.ShapeDtypeStruct((b * t, D_MODEL), F32),
            jax.ShapeDtypeStruct((b * t, ROUTER_LANES), jnp.int32),
            jax.ShapeDtypeStruct((b * t, ROUTER_LANES), F32),
            jax.ShapeDtypeStruct((b, HEADS, HEAD_DIM, HEAD_DIM), F32),
            jax.ShapeDtypeStruct((b, POOL_BUF, D_MODEL), F32),
        ],
        scratch_shapes=[
            pltpu.VMEM((HEADS, HEAD_DIM, HEAD_DIM), F32),
            pltpu.VMEM((rows + POOL_CARRY, D_MODEL), F32),
            pltpu.VMEM((rows, D_MODEL), BF16),
            pltpu.VMEM((rows, D_MODEL), BF16),
            pltpu.VMEM((rows, D_MODEL), BF16),
            pltpu.VMEM((rows, D_MODEL), BF16),
            pltpu.VMEM((rows, D_MODEL), F32),
            pltpu.VMEM((rows, D_MODEL), F32),
        ],
        compiler_params=pltpu.CompilerParams(dimension_semantics=("arbitrary", "arbitrary"),
                                             vmem_limit_bytes=VMEM_LIMIT),
        name="mixer_prompt",
    )(x, gmix, win, lbl, hgn, wpool, pscale, wout, gffn, wr, br)


SAMPLE_GROUP = 8


def _mixer_sample_kernel(pool_cnt, x_ref, gmix_ref, win_ref, wqft_ref, lblt_ref, hgn_ref, wpool_ref, pscale_ref,
                         wout_ref, gffn_ref, wr_ref, br_ref, s_ref, pbuf_ref,
                         x1_ref, h2_ref, ti_ref, tw_ref, snew_ref, pnew_ref,
                         h_ref, znat_ref, o_ref, bsum_ref):
    step = pl.program_id(0)
    last = pl.num_programs(0) - 1
    r0 = pl.multiple_of(step * SAMPLE_GROUP, SAMPLE_GROUP)

    @pl.when(step == 0)
    def _():
        hf = _rms(x_ref[...], gmix_ref[...])
        h_ref[...] = hf
        h = hf.astype(BF16)
        for i, seg in enumerate((SEG_I, SEG_G, SEG_U, SEG_A, SEG_B)):
            znat_ref[:, i * D_MODEL:(i + 1) * D_MODEL] = _dot(h, win_ref[:, seg * D_MODEL:(seg + 1) * D_MODEL])

    hg = h_ref[pl.ds(r0, SAMPLE_GROUP), :].astype(BF16)
    qft = _dot_nt(wqft_ref[...], hg)
    lbl = lblt_ref[...]
    m = jnp.max(lbl, axis=1, keepdims=True)
    e = jnp.exp(lbl - m)
    lb = e[:, 0:1] / jnp.sum(e, axis=1, keepdims=True)
    q_raw = qft[0:D_MODEL]
    qt = q_raw * _sigmoid(q_raw)
    ft = lb + (1.0 - lb) * _sigmoid(qft[D_MODEL:2 * D_MODEL])
    kt = 1.0 - ft

    for j in range(SAMPLE_GROUP):
        v_row = znat_ref[pl.ds(r0 + j, 1), 0:D_MODEL]
        o_parts = []
        for hd in range(HEADS):
            rs = slice(hd * HEAD_DIM, (hd + 1) * HEAD_DIM)
            s_new = ft[rs, j:j + 1] * s_ref[j, hd] + kt[rs, j:j + 1] * v_row[:, rs]
            snew_ref[j, hd] = s_new
            o_parts.append(jnp.sum(qt[rs, j:j + 1] * s_new, axis=0, keepdims=True))
        o_ref[pl.ds(r0 + j, 1), :] = jnp.concatenate(o_parts, axis=-1)

    u_g = znat_ref[pl.ds(r0, SAMPLE_GROUP), 2 * D_MODEL:3 * D_MODEL]
    sums = []
    for g, w in enumerate(POOL_WINDOWS):
        acc = jnp.zeros((SAMPLE_GROUP, POOL_GW), F32)
        for j in range(1, w):
            row = POOL_BUF - j
            acc = acc + pbuf_ref[:, row * D_MODEL + g * POOL_GW:row * D_MODEL + (g + 1) * POOL_GW]
        sums.append(acc)
    bsum_ref[pl.ds(r0, SAMPLE_GROUP), :] = jnp.concatenate(sums, axis=-1)
    pnew_ref[:, 0:(POOL_BUF - 1) * D_MODEL] = pbuf_ref[:, D_MODEL:POOL_BUF * D_MODEL]
    pnew_ref[:, (POOL_BUF - 1) * D_MODEL:POOL_BUF * D_MODEL] = u_g

    @pl.when(step == last)
    def _():
        x = x_ref[...]
        g_raw = znat_ref[:, D_MODEL:2 * D_MODEL]
        u = znat_ref[:, 2 * D_MODEL:3 * D_MODEL]
        o = _head_norm_gate(o_ref[...], g_raw, hgn_ref[...])
        sw = bsum_ref[...] + u
        pooled = []
        for g, w in enumerate(POOL_WINDOWS):
            cols = slice(g * POOL_GW, (g + 1) * POOL_GW)
            dg = sw[:, cols] * (1.0 / pool_cnt[g]) - u[:, cols]
            pooled.append(_dot(dg.astype(BF16), wpool_ref[g]))
        y_pool = jnp.concatenate(pooled, axis=-1) * pscale_ref[...]
        merged = (_sigmoid(znat_ref[:, 3 * D_MODEL:4 * D_MODEL]) * o
                  + _sigmoid(znat_ref[:, 4 * D_MODEL:5 * D_MODEL]) * y_pool)
        x1 = x + _dot(merged.astype(BF16), wout_ref[...])
        x1_ref[...] = x1
        h2 = _rms(x1, gffn_ref[...])
        h2_ref[...] = h2
        _route(h2, wr_ref, br_ref, ti_ref, tw_ref)


def _mixer_sample(x, gmix, win, wqft, lblt, hgn, wpool, pscale, wout, gffn, wr, br, state, pbuf, start_pos):
    n = x.shape[0]
    steps = n // SAMPLE_GROUP
    pool_cnt = tuple(float(min(start_pos + 1, w)) for w in POOL_WINDOWS)
    pbuf2 = pbuf.reshape(n, POOL_BUF * D_MODEL)
    full = _const_spec
    out = pl.pallas_call(
        functools.partial(_mixer_sample_kernel, pool_cnt),
        grid=(steps,),
        in_specs=[
            full((n, D_MODEL)),
            full((1, D_MODEL)),
            full(win.shape),
            full(wqft.shape),
            full(lblt.shape),
            full((1, HEAD_DIM)),
            full(wpool.shape),
            full((1, D_MODEL)),
            full(wout.shape),
            full((1, D_MODEL)),
            full(wr.shape),
            full(br.shape),
            pl.BlockSpec((SAMPLE_GROUP, HEADS, HEAD_DIM, HEAD_DIM), lambda i: (i, 0, 0, 0)),
            pl.BlockSpec((SAMPLE_GROUP, POOL_BUF * D_MODEL), lambda i: (i, 0)),
        ],
        out_specs=[
            pl.BlockSpec((n, D_MODEL), lambda i: (0, 0)),
            pl.BlockSpec((n, D_MODEL), lambda i: (0, 0)),
            pl.BlockSpec((n, ROUTER_LANES), lambda i: (0, 0)),
            pl.BlockSpec((n, ROUTER_LANES), lambda i: (0, 0)),
            pl.BlockSpec((SAMPLE_GROUP, HEADS, HEAD_DIM, HEAD_DIM), lambda i: (i, 0, 0, 0)),
            pl.BlockSpec((SAMPLE_GROUP, POOL_BUF * D_MODEL), lambda i: (i, 0)),
        ],
        out_shape=[
            jax.ShapeDtypeStruct((n, D_MODEL), F32),
            jax.ShapeDtypeStruct((n, D_MODEL), F32),
            jax.ShapeDtypeStruct((n, ROUTER_LANES), jnp.int32),
            jax.ShapeDtypeStruct((n, ROUTER_LANES), F32),
            jax.ShapeDtypeStruct(state.shape, F32),
            jax.ShapeDtypeStruct(pbuf2.shape, F32),
        ],
        scratch_shapes=[
            pltpu.VMEM((n, D_MODEL), F32),
            pltpu.VMEM((n, 5 * D_MODEL), F32),
            pltpu.VMEM((n, D_MODEL), F32),
            pltpu.VMEM((n, D_MODEL), F32),
        ],
        compiler_params=pltpu.CompilerParams(dimension_semantics=("arbitrary",), vmem_limit_bytes=VMEM_LIMIT),
        name="mixer_sample",
    )(x, gmix, win, wqft, lblt, hgn, wpool, pscale, wout, gffn, wr, br, state, pbuf2)
    x1, h2, ti, tw, snew, pnew = out
    return x1, h2, ti, tw, snew, pnew.reshape(n, POOL_BUF, D_MODEL)


def _moe_kernel(n_tokens, be_ref, off_ref, nv_ref, nu_ref, sa_ref, h2_hbm,
                wg_ref, bg_ref, wu_ref, bu_ref, wd_ref, bd_ref, y_hbm, xbuf, ybuf, wgb, wub, wdb, gsem, ssem):
    rows = MOE_ROWS
    i = pl.program_id(0)
    n_used = nu_ref[0]
    n_blocks = pl.num_programs(0)
    slot = i % 2
    prev = jnp.maximum(i - 1, 0)
    nv_prev = jnp.where(i >= 1, nv_ref[prev], 0)

    def gather_copy(a, r, s):
        return pltpu.make_async_copy(h2_hbm.at[pl.ds(a >> 2, 1), :], xbuf.at[s, pl.ds(r, 1), :], gsem.at[s])

    def scatter_copy(a, r, n, s):
        dst = (a & (TOP_K - 1)) * n_tokens + (a >> 2)
        return pltpu.make_async_copy(ybuf.at[s, pl.ds(r, n), :], y_hbm.at[pl.ds(dst, n), :], ssem.at[s])

    def wait_gather(s):
        pltpu.make_async_copy(h2_hbm.at[pl.ds(0, rows), :], xbuf.at[s], gsem.at[s]).wait()

    def wait_scatter(n, s):
        size = rows
        while size >= 1:
            @pl.when((n & size) != 0)
            def _():
                scatter_copy(0, 0, size, s).wait()
            size //= 2

    def scatter_loop(blk, n, s):
        def body(r, c):
            scatter_copy(sa_ref[off_ref[blk] + r], r, 1, s).start()
            return c
        lax.fori_loop(0, n, body, 0)

    def step(full_prev):
        nxt = jnp.minimum(i + 1, n_blocks - 1)
        off_next = off_ref[nxt]
        off_prev = off_ref[prev]
        for r in range(rows):
            gather_copy(sa_ref[off_next + r], r, 1 - slot).start()
        xb = xbuf[slot].astype(BF16)
        gate = jnp.minimum(_dot(xb, wgb[...]) + bg_ref[0], SWIGLU_LIMIT)
        up = jnp.clip(_dot(xb, wub[...]) + bu_ref[0], -SWIGLU_LIMIT, SWIGLU_LIMIT)
        act = (up + 1.0) * gate * _sigmoid(SWIGLU_ALPHA * gate)
        ybuf[slot] = _dot(act.astype(BF16), wdb[...]) + bd_ref[0]
        if full_prev:
            for r in range(rows):
                scatter_copy(sa_ref[off_prev + r], r, 1, 1 - slot).start()
        else:
            scatter_loop(prev, nv_prev, 1 - slot)

    @pl.when(i == 0)
    def _():
        def body(r, c):
            gather_copy(sa_ref[off_ref[0] + r], r, 0).start()
            return c
        lax.fori_loop(0, rows, body, 0)

    @pl.when(i < n_used)
    def _():
        first_of_expert = jnp.logical_or(i == 0, be_ref[i] != be_ref[prev])

        @pl.when(first_of_expert)
        def _():
            wgb[...] = wg_ref[0].astype(BF16)
            wub[...] = wu_ref[0].astype(BF16)
            wdb[...] = wd_ref[0].astype(BF16)

        wait_gather(slot)

        @pl.when(i >= 2)
        def _():
            wait_scatter(nv_ref[jnp.maximum(i - 2, 0)], slot)

        @pl.when(nv_prev == rows)
        def _():
            step(True)

        @pl.when(nv_prev != rows)
        def _():
            step(False)

        @pl.when(i == n_used - 1)
        def _():
            scatter_loop(i, nv_ref[i], slot)
            wait_scatter(nv_prev, 1 - slot)
            wait_scatter(nv_ref[i], slot)
            wait_gather(1 - slot)


def _moe(h2, routing, wg, bg, wu, bu, wd, bd):
    rows = MOE_ROWS
    n_tokens = h2.shape[0]
    block_e, block_off, block_nv, n_used, sorted_a = routing
    n_blocks = block_e.shape[0]
    w_spec = pl.BlockSpec((1, D_MODEL, D_MODEL), lambda i, be, *_: (be[i], 0, 0))
    b_spec = pl.BlockSpec((1, 1, D_MODEL), lambda i, be, *_: (be[i], 0, 0))
    grid_spec = pltpu.PrefetchScalarGridSpec(
        num_scalar_prefetch=5,
        grid=(n_blocks,),
        in_specs=[pl.BlockSpec(memory_space=pl.ANY), w_spec, b_spec, w_spec, b_spec, w_spec, b_spec],
        out_specs=pl.BlockSpec(memory_space=pl.ANY),
        scratch_shapes=[
            pltpu.VMEM((2, rows, D_MODEL), F32),
            pltpu.VMEM((2, rows, D_MODEL), F32),
            pltpu.VMEM((D_MODEL, D_MODEL), BF16),
            pltpu.VMEM((D_MODEL, D_MODEL), BF16),
            pltpu.VMEM((D_MODEL, D_MODEL), BF16),
            pltpu.SemaphoreType.DMA((2,)),
            pltpu.SemaphoreType.DMA((2,)),
        ],
    )
    return pl.pallas_call(
        functools.partial(_moe_kernel, n_tokens),
        grid_spec=grid_spec,
        out_shape=jax.ShapeDtypeStruct((TOP_K * n_tokens, D_MODEL), F32),
        compiler_params=pltpu.CompilerParams(dimension_semantics=("arbitrary",), vmem_limit_bytes=VMEM_LIMIT),
        name="moe_experts",
    )(block_e, block_off, block_nv, n_used, sorted_a, h2, wg, bg, wu, bu, wd, bd)


ASSIGN_BITS = 17


def _moe_routing(top_i):
    rows = MOE_ROWS
    n_assign = top_i.shape[0] * TOP_K
    assert n_assign <= 1 << ASSIGN_BITS
    n_blocks = -(-n_assign // rows) + N_EXPERTS
    flat_e = top_i.reshape(-1)
    keys = jnp.sort(flat_e * (1 << ASSIGN_BITS) + jnp.arange(n_assign, dtype=jnp.int32))
    smem_len = -(-(n_assign + rows) // 1024) * 1024
    sorted_a = jnp.pad(keys & ((1 << ASSIGN_BITS) - 1), (0, smem_len - n_assign))
    experts = jnp.arange(N_EXPERTS, dtype=jnp.int32)
    counts = jnp.sum((flat_e[:, None] == experts[None, :]).astype(jnp.int32), axis=0)
    nblk = (counts + rows - 1) // rows
    blk_end = jnp.cumsum(nblk)
    blk_start = blk_end - nblk
    row_start = jnp.cumsum(counts) - counts
    blk = jnp.arange(n_blocks, dtype=jnp.int32)
    owner = (blk[:, None] >= blk_start[None, :]) & (blk[:, None] < blk_end[None, :])
    pick = lambda v: jnp.sum(jnp.where(owner, v[None, :], 0), axis=1)
    within = (blk - pick(blk_start)) * rows
    used = blk < blk_end[-1]
    block_e = jnp.where(used, pick(experts), N_EXPERTS - 1).astype(jnp.int32)
    block_off = jnp.where(used, pick(row_start) + within, 0).astype(jnp.int32)
    block_nv = jnp.where(used, jnp.clip(pick(counts) - within, 0, rows), 0).astype(jnp.int32)
    n_used = blk_end[-1].astype(jnp.int32).reshape(1)
    return block_e, block_off, block_nv, n_used, sorted_a.astype(jnp.int32)


def _final_kernel(x1_ref, y_ref, tw_ref, p_ref, gple_ref, wpg_ref, wpp_ref, gfin_ref, out_ref):
    x = x1_ref[...]
    tw = tw_ref[...]
    for j in range(TOP_K):
        x = x + tw[:, j:j + 1] * y_ref[j]
    gate = _sigmoid(_dot(_rms(x, gple_ref[...]).astype(BF16), wpg_ref[...]))
    x = x + gate * _dot(p_ref[...].astype(BF16), wpp_ref[...])
    out_ref[...] = _rms(x, gfin_ref[...])


def _final(x1, y4, tw, p, gple, wpg, wpp, gfin, rows, row_offset):
    n = x1.shape[0]
    off = row_offset // rows
    assert off * rows == row_offset and n % rows == 0
    ple = p.shape[-1]
    return pl.pallas_call(
        _final_kernel,
        grid=(n // rows,),
        in_specs=[
            pl.BlockSpec((rows, D_MODEL), lambda i: (i, 0)),
            pl.BlockSpec((TOP_K, rows, D_MODEL), lambda i: (0, i + off, 0)),
            pl.BlockSpec((rows, ROUTER_LANES), lambda i: (i, 0)),
            pl.BlockSpec((rows, ple), lambda i: (i, 0)),
            _const_spec((1, D_MODEL)),
            _const_spec(wpg.shape),
            _const_spec(wpp.shape),
            _const_spec((1, D_MODEL)),
        ],
        out_specs=pl.BlockSpec((rows, D_MODEL), lambda i: (i, 0)),
        out_shape=jax.ShapeDtypeStruct((n, D_MODEL), F32),
        compiler_params=pltpu.CompilerParams(dimension_semantics=("arbitrary",), vmem_limit_bytes=VMEM_LIMIT),
        name="final",
    )(x1, y4, tw, p, gple, wpg, wpp, gfin)


def kernel(x_prompt, x_sample, p_prompt, p_sample, state_hgrn, state_pool, g_mix, w_in, hg_lb_logits, hg_norm, w_pool,
           pool_scale, w_out, g_ffn, w_router, b_router, w_gate, b_gate, w_up, b_up, w_down, b_down, g_ple,
           w_ple_gate, w_ple_proj, g_final):
    depth = w_in.shape[0]
    assert depth == 1, "single-layer step"
    b, t, _ = x_prompt.shape
    ns = x_sample.shape[0]
    assert x_sample.shape[1] == 1
    n_prompt = b * t
    n_tok = n_prompt + ns

    row = lambda a: a.reshape(1, -1)
    win = w_in[0].astype(BF16)
    wqft = win[:, :2 * D_MODEL].T
    lbl = hg_lb_logits.astype(F32)
    wpool = w_pool[0].astype(BF16)
    wout = w_out[0].astype(BF16)
    wr = jnp.pad(w_router[0], ((0, 0), (0, ROUTER_LANES - N_EXPERTS)))
    br = jnp.pad(b_router[0], (0, ROUTER_LANES - N_EXPERTS)).reshape(1, -1)
    shared = (row(hg_norm[0]), wpool, row(pool_scale[0]), wout, row(g_ffn[0]), wr, br)

    x1_p, h2_p, ti_p, tw_p, s_p, pool_p = _mixer_prompt(x_prompt, row(g_mix[0]), win, lbl, *shared)
    x1_s, h2_s, ti_s, tw_s, s_s, pool_s = _mixer_sample(
        x_sample.reshape(ns, D_MODEL), row(g_mix[0]), win, wqft, lbl.T, *shared,
        state_hgrn[0], state_pool[0], start_pos=PAST_LEN)

    top_i = jnp.concatenate([ti_p[:, :TOP_K], ti_s[:, :TOP_K]], axis=0)
    routing = _moe_routing(top_i)
    e3 = lambda a: a[0].reshape(N_EXPERTS, 1, D_MODEL)
    h2 = jnp.concatenate([h2_p, h2_s], axis=0)
    y = _moe(h2, routing, w_gate[0], e3(b_gate), w_up[0], e3(b_up), w_down[0], e3(b_down))
    y4 = y.reshape(TOP_K, n_tok, D_MODEL)

    wpg = w_ple_gate[0].astype(BF16)
    wpp = w_ple_proj[0].astype(BF16)
    fin = (row(g_ple[0]), wpg, wpp, row(g_final))
    y_p = _final(x1_p.reshape(n_prompt, D_MODEL), y4, tw_p, p_prompt[0].reshape(n_prompt, -1), *fin,
                 rows=FINAL_ROWS, row_offset=0)
    y_s = _final(x1_s, y4, tw_s, p_sample[0].reshape(ns, -1), *fin, rows=ns, row_offset=n_prompt)

    return (y_p.reshape(b, t, D_MODEL), y_s.reshape(ns, 1, D_MODEL), s_p[None], pool_p[None], s_s[None],
            pool_s[None])
```

```python
import functools

import jax
import jax.numpy as jnp
from jax import lax
from jax.experimental import pallas as pl
from jax.experimental.pallas import tpu as pltpu

F32 = jnp.float32
BF16 = jnp.bfloat16

D_MODEL = 1024
HEADS = 8
HEAD_DIM = 128
CHUNK = 32
POOL_WINDOWS = (2, 4, 8, 16)
POOL_GW = D_MODEL // len(POOL_WINDOWS)
POOL_BUF = 15
POOL_CARRY = 16
N_EXPERTS = 32
TOP_K = 4
ROUTER_LANES = 128
SWIGLU_LIMIT = 7.0
SWIGLU_ALPHA = 1.702
EPS = 1e-6
PAST_LEN = 16384

SEG_Q, SEG_F, SEG_I, SEG_G, SEG_U, SEG_A, SEG_B = range(7)

MIXER_ROWS = 512
MOE_ROWS = 256
FINAL_ROWS = 512
VMEM_LIMIT = 56 * 1024 * 1024


def _rms(x, g):
    ms = jnp.mean(x * x, axis=-1, keepdims=True)
    return x * lax.rsqrt(ms + EPS) * g


def _sigmoid(x):
    return 1.0 / (1.0 + jnp.exp(-x))


def _dot(a, b):
    return jnp.dot(a, b, preferred_element_type=F32)


def _dot_nt(a, b):
    return lax.dot_general(a, b, (((1,), (1,)), ((), ())), preferred_element_type=F32)


def _dot_tn(a, b):
    return lax.dot_general(a, b, (((0,), (0,)), ((), ())), preferred_element_type=F32)


LANES = 128
ROW_TILE = D_MODEL // LANES


def _store_row_tiles(ref, lead, x, period=ROW_TILE, offset=0):
    rows = x.shape[0]
    for c in range(ROW_TILE):
        ref[lead + (pl.ds(offset + c, rows, stride=period), slice(None))] = x[:, c * LANES:(c + 1) * LANES]


def _load_row_tiles(ref, lead, rows, period=ROW_TILE, offset=0):
    return jnp.concatenate(
        [ref[lead + (pl.ds(offset + c, rows, stride=period), slice(None))] for c in range(ROW_TILE)], axis=-1)


def _split_bf16(x):
    hi = x.astype(BF16)
    lo = (x - hi.astype(F32)).astype(BF16)
    return hi, lo


def _forget_lower_bound(lbl):
    m = jnp.max(lbl, axis=0, keepdims=True)
    e = jnp.exp(lbl - m)
    return e[0:1] / jnp.sum(e, axis=0, keepdims=True)


def _head_norm_gate(o, g_raw, hgn):
    parts = []
    for h in range(HEADS):
        oh = o[:, h * HEAD_DIM:(h + 1) * HEAD_DIM]
        parts.append(_rms(oh, hgn))
    return jnp.concatenate(parts, axis=-1) * (g_raw * _sigmoid(g_raw))


def _route(h2, wr_ref, br_ref, ti_ref, tw_ref):
    rows = h2.shape[0]
    h_hi, h_lo = _split_bf16(h2)
    w_hi, w_lo = _split_bf16(wr_ref[...])
    logits = _dot(h_hi, w_hi) + _dot(h_lo, w_hi) + _dot(h_hi, w_lo) + br_ref[...]
    lane = lax.broadcasted_iota(jnp.int32, (rows, ROUTER_LANES), 1)
    neg = jnp.float32(-jnp.inf)
    l = jnp.where(lane < N_EXPERTS, logits, neg)
    ti = jnp.zeros((rows, ROUTER_LANES), jnp.int32)
    tw = jnp.zeros((rows, ROUTER_LANES), F32)
    m0 = None
    denom = None
    es = []
    for j in range(TOP_K):
        m = jnp.max(l, axis=-1, keepdims=True)
        idx = jnp.min(jnp.where(l == m, lane, ROUTER_LANES), axis=-1, keepdims=True)
        l = jnp.where(lane == idx, neg, l)
        if j == 0:
            m0 = m
        e = jnp.exp(m - m0)
        es.append(e)
        denom = e if denom is None else denom + e
        ti = jnp.where(lane == j, idx, ti)
    for j in range(TOP_K):
        tw = jnp.where(lane == j, es[j] / denom, tw)
    ti_ref[...] = ti
    tw_ref[...] = tw


def _mixer_prompt_kernel(x_ref, gmix_ref, win_ref, lbl_ref, hgn_ref, wpool_ref, pscale_ref, wout_ref, gffn_ref,
                         wr_ref, br_ref,
                         x1_ref, h2_ref, ti_ref, tw_ref, sfin_ref, ptail_ref,
                         st_ref, uext_ref, qe_ref, ke_ref, kd_ref, v_ref, o_ref, p_ref):
    rows = MIXER_ROWS
    n_chunks = rows // CHUNK
    t = pl.program_id(1)
    last_t = pl.num_programs(1) - 1

    @pl.when(t == 0)
    def _():
        st_ref[...] = jnp.zeros_like(st_ref)
        uext_ref[0:POOL_CARRY, :] = jnp.zeros((POOL_CARRY, D_MODEL), F32)

    @pl.when(t > 0)
    def _():
        uext_ref[0:POOL_CARRY, :] = uext_ref[rows:rows + POOL_CARRY, :]

    x = x_ref[0]
    h = _rms(x, gmix_ref[...]).astype(BF16)

    def proj(seg):
        return _dot(h, win_ref[:, seg * D_MODEL:(seg + 1) * D_MODEL])

    lb = _forget_lower_bound(lbl_ref[...])
    q_raw = proj(SEG_Q)
    q = q_raw * _sigmoid(q_raw)
    f = lb + (1.0 - lb) * _sigmoid(proj(SEG_F))
    k = 1.0 - f
    row_in_chunk = lax.broadcasted_iota(jnp.int32, (rows, D_MODEL), 0) % CHUNK
    p = f
    s = 1
    while s < CHUNK:
        p = p * jnp.where(row_in_chunk >= s, pltpu.roll(p, s, axis=0), 1.0)
        s *= 2
    p3 = p.reshape(n_chunks, CHUNK, D_MODEL)
    plast = p3[:, CHUNK - 1:CHUNK, :]
    qe_ref[...] = (q * p).astype(BF16)
    ke_ref[...] = (k / p).astype(BF16)
    kd_ref[...] = (k * (plast / p3).reshape(rows, D_MODEL)).astype(BF16)
    p_ref[...] = p
    v_ref[...] = proj(SEG_I).astype(BF16)

    causal = (lax.broadcasted_iota(jnp.int32, (CHUNK, CHUNK), 0)
              >= lax.broadcasted_iota(jnp.int32, (CHUNK, CHUNK), 1))

    for hd in range(HEADS):
        cols = slice(hd * HEAD_DIM, (hd + 1) * HEAD_DIM)
        st = st_ref[hd]
        for c in range(n_chunks):
            rs = slice(c * CHUNK, (c + 1) * CHUNK)
            qe = qe_ref[rs, cols]
            ke = ke_ref[rs, cols]
            kd = kd_ref[rs, cols]
            vv = v_ref[rs, cols]
            decay = p_ref[(c + 1) * CHUNK - 1:(c + 1) * CHUNK, cols]
            scores = jnp.where(causal, _dot_nt(qe, ke), 0.0)
            o_ref[rs, cols] = _dot_nt(qe, st.astype(BF16)) + _dot(scores.astype(BF16), vv)
            st = st * decay + _dot_tn(vv, kd)
        st_ref[hd] = st

    @pl.when(t == last_t)
    def _():
        for hd in range(HEADS):
            sfin_ref[0, hd] = st_ref[hd].T

    g_raw = proj(SEG_G)
    o = _head_norm_gate(o_ref[...], g_raw, hgn_ref[...])

    u = proj(SEG_U)
    uext_ref[POOL_CARRY:POOL_CARRY + rows, :] = u
    pos1 = t * rows + lax.broadcasted_iota(jnp.int32, (rows, 1), 0) + 1
    pooled = []
    for g, w in enumerate(POOL_WINDOWS):
        cols = slice(g * POOL_GW, (g + 1) * POOL_GW)
        sw = u[:, cols]
        for j in range(1, w):
            sw = sw + uext_ref[POOL_CARRY - j:POOL_CARRY - j + rows, cols]
        inv_cnt = 1.0 / jnp.minimum(pos1, w).astype(F32)
        dg = sw * inv_cnt - u[:, cols]
        pooled.append(_dot(dg.astype(BF16), wpool_ref[g]))
    y_pool = jnp.concatenate(pooled, axis=-1) * pscale_ref[...]

    @pl.when(t == last_t)
    def _():
        ptail_ref[0] = uext_ref[rows + POOL_CARRY - POOL_BUF:rows + POOL_CARRY, :]

    merged = _sigmoid(proj(SEG_A)) * o + _sigmoid(proj(SEG_B)) * y_pool
    x1 = x + _dot(merged.astype(BF16), wout_ref[...])
    x1_ref[0] = x1
    h2 = _rms(x1, gffn_ref[...])
    _store_row_tiles(h2_ref, (), h2)
    _route(h2, wr_ref, br_ref, ti_ref, tw_ref)


def _const_spec(shape):
    zeros = (0,) * len(shape)
    return pl.BlockSpec(shape, lambda *_: zeros, pipeline_mode=pl.Buffered(1))


def _mixer_prompt(x, gmix, win, lbl, hgn, wpool, pscale, wout, gffn, wr, br):
    b, t, _ = x.shape
    rows = MIXER_ROWS
    nt = t // rows
    tok_spec = pl.BlockSpec((1, rows, D_MODEL), lambda i, j: (i, j, 0))
    flat_spec = pl.BlockSpec((rows * ROW_TILE, LANES), lambda i, j: (i * nt + j, 0))
    lane_spec = pl.BlockSpec((rows, ROUTER_LANES), lambda i, j: (i * nt + j, 0))
    return pl.pallas_call(
        _mixer_prompt_kernel,
        grid=(b, nt),
        in_specs=[
            tok_spec,
            _const_spec((1, D_MODEL)),
            _const_spec(win.shape),
            _const_spec(lbl.shape),
            _const_spec((1, HEAD_DIM)),
            _const_spec(wpool.shape),
            _const_spec((1, D_MODEL)),
            _const_spec(wout.shape),
            _const_spec((1, D_MODEL)),
            _const_spec(wr.shape),
            _const_spec(br.shape),
        ],
        out_specs=[
            tok_spec,
            flat_spec,
            lane_spec,
            lane_spec,
            pl.BlockSpec((1, HEADS, HEAD_DIM, HEAD_DIM), lambda i, j: (i, 0, 0, 0)),
            pl.BlockSpec((1, POOL_BUF, D_MODEL), lambda i, j: (i, 0, 0)),
        ],
        out_shape=[
            jax.ShapeDtypeStruct(x.shape, F32),
            jax.ShapeDtypeStruct((b * t * ROW_TILE, LANES), F32),
            jax.ShapeDtypeStruct((b * t, ROUTER_LANES), jnp.int32),
            jax.ShapeDtypeStruct((b * t, ROUTER_LANES), F32),
            jax.ShapeDtypeStruct((b, HEADS, HEAD_DIM, HEAD_DIM), F32),
            jax.ShapeDtypeStruct((b, POOL_BUF, D_MODEL), F32),
        ],
        scratch_shapes=[
            pltpu.VMEM((HEADS, HEAD_DIM, HEAD_DIM), F32),
            pltpu.VMEM((rows + POOL_CARRY, D_MODEL), F32),
            pltpu.VMEM((rows, D_MODEL), BF16),
            pltpu.VMEM((rows, D_MODEL), BF16),
            pltpu.VMEM((rows, D_MODEL), BF16),
            pltpu.VMEM((rows, D_MODEL), BF16),
            pltpu.VMEM((rows, D_MODEL), F32),
            pltpu.VMEM((rows, D_MODEL), F32),
        ],
        compiler_params=pltpu.CompilerParams(dimension_semantics=("arbitrary", "arbitrary"),
                                             vmem_limit_bytes=VMEM_LIMIT),
        name="mixer_prompt",
    )(x, gmix, win, lbl, hgn, wpool, pscale, wout, gffn, wr, br)


SAMPLE_GROUP = 8


def _mixer_sample_kernel(pool_cnt, x_ref, gmix_ref, win_ref, wqft_ref, lblt_ref, hgn_ref, wpool_ref, pscale_ref,
                         wout_ref, gffn_ref, wr_ref, br_ref, s_ref, pbuf_ref,
                         x1_ref, h2_ref, ti_ref, tw_ref, snew_ref, pnew_ref,
                         h_ref, znat_ref, o_ref, bsum_ref):
    step = pl.program_id(0)
    last = pl.num_programs(0) - 1
    r0 = pl.multiple_of(step * SAMPLE_GROUP, SAMPLE_GROUP)

    @pl.when(step == 0)
    def _():
        hf = _rms(x_ref[...], gmix_ref[...])
        h_ref[...] = hf
        h = hf.astype(BF16)
        for i, seg in enumerate((SEG_I, SEG_G, SEG_U, SEG_A, SEG_B)):
            znat_ref[:, i * D_MODEL:(i + 1) * D_MODEL] = _dot(h, win_ref[:, seg * D_MODEL:(seg + 1) * D_MODEL])

    hg = h_ref[pl.ds(r0, SAMPLE_GROUP), :].astype(BF16)
    qft = _dot_nt(wqft_ref[...], hg)
    lbl = lblt_ref[...]
    m = jnp.max(lbl, axis=1, keepdims=True)
    e = jnp.exp(lbl - m)
    lb = e[:, 0:1] / jnp.sum(e, axis=1, keepdims=True)
    q_raw = qft[0:D_MODEL]
    qt = q_raw * _sigmoid(q_raw)
    ft = lb + (1.0 - lb) * _sigmoid(qft[D_MODEL:2 * D_MODEL])
    kt = 1.0 - ft

    for j in range(SAMPLE_GROUP):
        v_row = znat_ref[pl.ds(r0 + j, 1), 0:D_MODEL]
        o_parts = []
        for hd in range(HEADS):
            rs = slice(hd * HEAD_DIM, (hd + 1) * HEAD_DIM)
            s_new = ft[rs, j:j + 1] * s_ref[j, hd] + kt[rs, j:j + 1] * v_row[:, rs]
            snew_ref[j, hd] = s_new
            o_parts.append(jnp.sum(qt[rs, j:j + 1] * s_new, axis=0, keepdims=True))
        o_ref[pl.ds(r0 + j, 1), :] = jnp.concatenate(o_parts, axis=-1)

    u_g = znat_ref[pl.ds(r0, SAMPLE_GROUP), 2 * D_MODEL:3 * D_MODEL]
    sums = []
    for g, w in enumerate(POOL_WINDOWS):
        acc = jnp.zeros((SAMPLE_GROUP, POOL_GW), F32)
        for j in range(1, w):
            row = POOL_BUF - j
            acc = acc + pbuf_ref[:, row * D_MODEL + g * POOL_GW:row * D_MODEL + (g + 1) * POOL_GW]
        sums.append(acc)
    bsum_ref[pl.ds(r0, SAMPLE_GROUP), :] = jnp.concatenate(sums, axis=-1)
    pnew_ref[:, 0:(POOL_BUF - 1) * D_MODEL] = pbuf_ref[:, D_MODEL:POOL_BUF * D_MODEL]
    pnew_ref[:, (POOL_BUF - 1) * D_MODEL:POOL_BUF * D_MODEL] = u_g

    @pl.when(step == last)
    def _():
        x = x_ref[...]
        g_raw = znat_ref[:, D_MODEL:2 * D_MODEL]
        u = znat_ref[:, 2 * D_MODEL:3 * D_MODEL]
        o = _head_norm_gate(o_ref[...], g_raw, hgn_ref[...])
        sw = bsum_ref[...] + u
        pooled = []
        for g, w in enumerate(POOL_WINDOWS):
            cols = slice(g * POOL_GW, (g + 1) * POOL_GW)
            dg = sw[:, cols] * (1.0 / pool_cnt[g]) - u[:, cols]
            pooled.append(_dot(dg.astype(BF16), wpool_ref[g]))
        y_pool = jnp.concatenate(pooled, axis=-1) * pscale_ref[...]
        merged = (_sigmoid(znat_ref[:, 3 * D_MODEL:4 * D_MODEL]) * o
                  + _sigmoid(znat_ref[:, 4 * D_MODEL:5 * D_MODEL]) * y_pool)
        x1 = x + _dot(merged.astype(BF16), wout_ref[...])
        x1_ref[...] = x1
        h2 = _rms(x1, gffn_ref[...])
        _store_row_tiles(h2_ref, (), h2)
        _route(h2, wr_ref, br_ref, ti_ref, tw_ref)


def _mixer_sample(x, gmix, win, wqft, lblt, hgn, wpool, pscale, wout, gffn, wr, br, state, pbuf, start_pos):
    n = x.shape[0]
    steps = n // SAMPLE_GROUP
    pool_cnt = tuple(float(min(start_pos + 1, w)) for w in POOL_WINDOWS)
    pbuf2 = pbuf.reshape(n, POOL_BUF * D_MODEL)
    full = _const_spec
    out = pl.pallas_call(
        functools.partial(_mixer_sample_kernel, pool_cnt),
        grid=(steps,),
        in_specs=[
            full((n, D_MODEL)),
            full((1, D_MODEL)),
            full(win.shape),
            full(wqft.shape),
            full(lblt.shape),
            full((1, HEAD_DIM)),
            full(wpool.shape),
            full((1, D_MODEL)),
            full(wout.shape),
            full((1, D_MODEL)),
            full(wr.shape),
            full(br.shape),
            pl.BlockSpec((SAMPLE_GROUP, HEADS, HEAD_DIM, HEAD_DIM), lambda i: (i, 0, 0, 0)),
            pl.BlockSpec((SAMPLE_GROUP, POOL_BUF * D_MODEL), lambda i: (i, 0)),
        ],
        out_specs=[
            pl.BlockSpec((n, D_MODEL), lambda i: (0, 0)),
            pl.BlockSpec((n * ROW_TILE, LANES), lambda i: (0, 0)),
            pl.BlockSpec((n, ROUTER_LANES), lambda i: (0, 0)),
            pl.BlockSpec((n, ROUTER_LANES), lambda i: (0, 0)),
            pl.BlockSpec((SAMPLE_GROUP, HEADS, HEAD_DIM, HEAD_DIM), lambda i: (i, 0, 0, 0)),
            pl.BlockSpec((SAMPLE_GROUP, POOL_BUF * D_MODEL), lambda i: (i, 0)),
        ],
        out_shape=[
            jax.ShapeDtypeStruct((n, D_MODEL), F32),
            jax.ShapeDtypeStruct((n * ROW_TILE, LANES), F32),
            jax.ShapeDtypeStruct((n, ROUTER_LANES), jnp.int32),
            jax.ShapeDtypeStruct((n, ROUTER_LANES), F32),
            jax.ShapeDtypeStruct(state.shape, F32),
            jax.ShapeDtypeStruct(pbuf2.shape, F32),
        ],
        scratch_shapes=[
            pltpu.VMEM((n, D_MODEL), F32),
            pltpu.VMEM((n, 5 * D_MODEL), F32),
            pltpu.VMEM((n, D_MODEL), F32),
            pltpu.VMEM((n, D_MODEL), F32),
        ],
        compiler_params=pltpu.CompilerParams(dimension_semantics=("arbitrary",), vmem_limit_bytes=VMEM_LIMIT),
        name="mixer_sample",
    )(x, gmix, win, wqft, lblt, hgn, wpool, pscale, wout, gffn, wr, br, state, pbuf2)
    x1, h2, ti, tw, snew, pnew = out
    return x1, h2, ti, tw, snew, pnew.reshape(n, POOL_BUF, D_MODEL)


def _moe_kernel(be_ref, off_ref, nv_ref, nu_ref, sa_ref, h2_hbm,
                wg_ref, bg_ref, wu_ref, bu_ref, wd_ref, bd_ref, y_hbm, xbuf, ybuf, wgb, wub, wdb, gsem, ssem):
    rows = MOE_ROWS
    i = pl.program_id(0)
    n_used = nu_ref[0]
    n_blocks = pl.num_programs(0)
    slot = i % 2
    prev = jnp.maximum(i - 1, 0)
    nv_prev = jnp.where(i >= 1, nv_ref[prev], 0)

    def tile_rows(first, n=1):
        start = first * ROW_TILE
        return pl.ds(start if isinstance(first, int) else pl.multiple_of(start, ROW_TILE), n * ROW_TILE)

    def gather_copy(a, r, s):
        tok = a >> 2
        return pltpu.make_async_copy(h2_hbm.at[tile_rows(tok), :], xbuf.at[s, tile_rows(r), :], gsem.at[s])

    def scatter_copy(a, r, n, s):
        return pltpu.make_async_copy(ybuf.at[s, tile_rows(r, n), :], y_hbm.at[tile_rows(a, n), :], ssem.at[s])

    def wait_gather(s):
        pltpu.make_async_copy(h2_hbm.at[tile_rows(0, rows), :], xbuf.at[s], gsem.at[s]).wait()

    def wait_scatter(n, s):
        size = rows
        while size >= 1:
            @pl.when((n & size) != 0)
            def _():
                scatter_copy(0, 0, size, s).wait()
            size //= 2

    def scatter_loop(blk, n, s):
        def body(r, c):
            scatter_copy(sa_ref[off_ref[blk] + r], r, 1, s).start()
            return c
        lax.fori_loop(0, n, body, 0)

    def step(full_prev):
        nxt = jnp.minimum(i + 1, n_blocks - 1)
        off_next = off_ref[nxt]
        off_prev = off_ref[prev]
        for r in range(rows):
            gather_copy(sa_ref[off_next + r], r, 1 - slot).start()
        xb = _load_row_tiles(xbuf, (slot,), rows).astype(BF16)
        gate = jnp.minimum(_dot(xb, wgb[...]) + bg_ref[0], SWIGLU_LIMIT)
        up = jnp.clip(_dot(xb, wub[...]) + bu_ref[0], -SWIGLU_LIMIT, SWIGLU_LIMIT)
        act = (up + 1.0) * gate * _sigmoid(SWIGLU_ALPHA * gate)
        _store_row_tiles(ybuf, (slot,), _dot(act.astype(BF16), wdb[...]) + bd_ref[0])
        if full_prev:
            for r in range(rows):
                scatter_copy(sa_ref[off_prev + r], r, 1, 1 - slot).start()
        else:
            scatter_loop(prev, nv_prev, 1 - slot)

    @pl.when(i == 0)
    def _():
        def body(r, c):
            gather_copy(sa_ref[off_ref[0] + r], r, 0).start()
            return c
        lax.fori_loop(0, rows, body, 0)

    @pl.when(i < n_used)
    def _():
        first_of_expert = jnp.logical_or(i == 0, be_ref[i] != be_ref[prev])

        @pl.when(first_of_expert)
        def _():
            wgb[...] = wg_ref[0].astype(BF16)
            wub[...] = wu_ref[0].astype(BF16)
            wdb[...] = wd_ref[0].astype(BF16)

        wait_gather(slot)

        @pl.when(i >= 2)
        def _():
            wait_scatter(nv_ref[jnp.maximum(i - 2, 0)], slot)

        @pl.when(nv_prev == rows)
        def _():
            step(True)

        @pl.when(nv_prev != rows)
        def _():
            step(False)

        @pl.when(i == n_used - 1)
        def _():
            scatter_loop(i, nv_ref[i], slot)
            wait_scatter(nv_prev, 1 - slot)
            wait_scatter(nv_ref[i], slot)
            wait_gather(1 - slot)


def _moe(h2, routing, wg, bg, wu, bu, wd, bd):
    rows = MOE_ROWS
    n_tokens = h2.shape[0] // ROW_TILE
    block_e, block_off, block_nv, n_used, sorted_a = routing
    n_blocks = block_e.shape[0]
    w_spec = pl.BlockSpec((1, D_MODEL, D_MODEL), lambda i, be, *_: (be[i], 0, 0))
    b_spec = pl.BlockSpec((1, 1, D_MODEL), lambda i, be, *_: (be[i], 0, 0))
    grid_spec = pltpu.PrefetchScalarGridSpec(
        num_scalar_prefetch=5,
        grid=(n_blocks,),
        in_specs=[pl.BlockSpec(memory_space=pl.ANY), w_spec, b_spec, w_spec, b_spec, w_spec, b_spec],
        out_specs=pl.BlockSpec(memory_space=pl.ANY),
        scratch_shapes=[
            pltpu.VMEM((2, rows * ROW_TILE, LANES), F32),
            pltpu.VMEM((2, rows * ROW_TILE, LANES), F32),
            pltpu.VMEM((D_MODEL, D_MODEL), BF16),
            pltpu.VMEM((D_MODEL, D_MODEL), BF16),
            pltpu.VMEM((D_MODEL, D_MODEL), BF16),
            pltpu.SemaphoreType.DMA((2,)),
            pltpu.SemaphoreType.DMA((2,)),
        ],
    )
    return pl.pallas_call(
        _moe_kernel,
        grid_spec=grid_spec,
        out_shape=jax.ShapeDtypeStruct((n_tokens * TOP_K * ROW_TILE, LANES), F32),
        compiler_params=pltpu.CompilerParams(dimension_semantics=("arbitrary",), vmem_limit_bytes=VMEM_LIMIT),
        name="moe_experts",
    )(block_e, block_off, block_nv, n_used, sorted_a, h2, wg, bg, wu, bu, wd, bd)


ASSIGN_BITS = 17


def _moe_routing(top_i):
    rows = MOE_ROWS
    n_assign = top_i.shape[0] * TOP_K
    assert n_assign <= 1 << ASSIGN_BITS
    n_blocks = -(-n_assign // rows) + N_EXPERTS
    flat_e = top_i.reshape(-1)
    keys = jnp.sort(flat_e * (1 << ASSIGN_BITS) + jnp.arange(n_assign, dtype=jnp.int32))
    smem_len = -(-(n_assign + rows) // 1024) * 1024
    sorted_a = jnp.pad(keys & ((1 << ASSIGN_BITS) - 1), (0, smem_len - n_assign))
    experts = jnp.arange(N_EXPERTS, dtype=jnp.int32)
    counts = jnp.sum((flat_e[:, None] == experts[None, :]).astype(jnp.int32), axis=0)
    nblk = (counts + rows - 1) // rows
    blk_end = jnp.cumsum(nblk)
    blk_start = blk_end - nblk
    row_start = jnp.cumsum(counts) - counts
    blk = jnp.arange(n_blocks, dtype=jnp.int32)
    owner = (blk[:, None] >= blk_start[None, :]) & (blk[:, None] < blk_end[None, :])
    pick = lambda v: jnp.sum(jnp.where(owner, v[None, :], 0), axis=1)
    within = (blk - pick(blk_start)) * rows
    used = blk < blk_end[-1]
    block_e = jnp.where(used, pick(experts), N_EXPERTS - 1).astype(jnp.int32)
    block_off = jnp.where(used, pick(row_start) + within, 0).astype(jnp.int32)
    block_nv = jnp.where(used, jnp.clip(pick(counts) - within, 0, rows), 0).astype(jnp.int32)
    n_used = blk_end[-1].astype(jnp.int32).reshape(1)
    return block_e, block_off, block_nv, n_used, sorted_a.astype(jnp.int32)


def _final_kernel(x1_ref, y_ref, tw_ref, p_ref, gple_ref, wpg_ref, wpp_ref, gfin_ref, out_ref):
    x = x1_ref[...]
    tw = tw_ref[...]
    rows = x.shape[0]
    for j in range(TOP_K):
        x = x + tw[:, j:j + 1] * _load_row_tiles(y_ref, (), rows, period=TOP_K * ROW_TILE, offset=j * ROW_TILE)
    gate = _sigmoid(_dot(_rms(x, gple_ref[...]).astype(BF16), wpg_ref[...]))
    x = x + gate * _dot(p_ref[...].astype(BF16), wpp_ref[...])
    out_ref[...] = _rms(x, gfin_ref[...])


def _final(x1, y4, tw, p, gple, wpg, wpp, gfin, rows, row_offset):
    n = x1.shape[0]
    off = row_offset // rows
    assert off * rows == row_offset and n % rows == 0
    ple = p.shape[-1]
    return pl.pallas_call(
        _final_kernel,
        grid=(n // rows,),
        in_specs=[
            pl.BlockSpec((rows, D_MODEL), lambda i: (i, 0)),
            pl.BlockSpec((rows * TOP_K * ROW_TILE, LANES), lambda i: (i + off, 0)),
            pl.BlockSpec((rows, ROUTER_LANES), lambda i: (i, 0)),
            pl.BlockSpec((rows, ple), lambda i: (i, 0)),
            _const_spec((1, D_MODEL)),
            _const_spec(wpg.shape),
            _const_spec(wpp.shape),
            _const_spec((1, D_MODEL)),
        ],
        out_specs=pl.BlockSpec((rows, D_MODEL), lambda i: (i, 0)),
        out_shape=jax.ShapeDtypeStruct((n, D_MODEL), F32),
        compiler_params=pltpu.CompilerParams(dimension_semantics=("arbitrary",), vmem_limit_bytes=VMEM_LIMIT),
        name="final",
    )(x1, y4, tw, p, gple, wpg, wpp, gfin)


def kernel(x_prompt, x_sample, p_prompt, p_sample, state_hgrn, state_pool, g_mix, w_in, hg_lb_logits, hg_norm, w_pool,
           pool_scale, w_out, g_ffn, w_router, b_router, w_gate, b_gate, w_up, b_up, w_down, b_down, g_ple,
           w_ple_gate, w_ple_proj, g_final):
    depth = w_in.shape[0]
    assert depth == 1, "single-layer step"
    b, t, _ = x_prompt.shape
    ns = x_sample.shape[0]
    assert x_sample.shape[1] == 1
    n_prompt = b * t
    n_tok = n_prompt + ns

    row = lambda a: a.reshape(1, -1)
    win = w_in[0].astype(BF16)
    wqft = win[:, :2 * D_MODEL].T
    lbl = hg_lb_logits.astype(F32)
    wpool = w_pool[0].astype(BF16)
    wout = w_out[0].astype(BF16)
    wr = jnp.pad(w_router[0], ((0, 0), (0, ROUTER_LANES - N_EXPERTS)))
    br = jnp.pad(b_router[0], (0, ROUTER_LANES - N_EXPERTS)).reshape(1, -1)
    shared = (row(hg_norm[0]), wpool, row(pool_scale[0]), wout, row(g_ffn[0]), wr, br)

    x1_p, h2_p, ti_p, tw_p, s_p, pool_p = _mixer_prompt(x_prompt, row(g_mix[0]), win, lbl, *shared)
    x1_s, h2_s, ti_s, tw_s, s_s, pool_s = _mixer_sample(
        x_sample.reshape(ns, D_MODEL), row(g_mix[0]), win, wqft, lbl.T, *shared,
        state_hgrn[0], state_pool[0], start_pos=PAST_LEN)

    top_i = jnp.concatenate([ti_p[:, :TOP_K], ti_s[:, :TOP_K]], axis=0)
    routing = _moe_routing(top_i)
    e3 = lambda a: a[0].reshape(N_EXPERTS, 1, D_MODEL)
    h2 = jnp.concatenate([h2_p, h2_s], axis=0)
    y4 = _moe(h2, routing, w_gate[0], e3(b_gate), w_up[0], e3(b_up), w_down[0], e3(b_down))

    wpg = w_ple_gate[0].astype(BF16)
    wpp = w_ple_proj[0].astype(BF16)
    fin = (row(g_ple[0]), wpg, wpp, row(g_final))
    y_p = _final(x1_p.reshape(n_prompt, D_MODEL), y4, tw_p, p_prompt[0].reshape(n_prompt, -1), *fin,
                 rows=FINAL_ROWS, row_offset=0)
    y_s = _final(x1_s, y4, tw_s, p_sample[0].reshape(ns, -1), *fin, rows=ns, row_offset=n_prompt)

    return (y_p.reshape(b, t, D_MODEL), y_s.reshape(ns, 1, D_MODEL), s_p[None], pool_p[None], s_s[None],
            pool_s[None])
```

```python
import functools

import jax
import jax.numpy as jnp
from jax import lax
from jax.experimental import pallas as pl
from jax.experimental.pallas import tpu as pltpu

F32 = jnp.float32
BF16 = jnp.bfloat16

D_MODEL = 1024
HEADS = 8
HEAD_DIM = 128
CHUNK = 32
POOL_WINDOWS = (2, 4, 8, 16)
POOL_GW = D_MODEL // len(POOL_WINDOWS)
POOL_BUF = 15
POOL_CARRY = 16
N_EXPERTS = 32
TOP_K = 4
ROUTER_LANES = 128
SWIGLU_LIMIT = 7.0
SWIGLU_ALPHA = 1.702
EPS = 1e-6
PAST_LEN = 16384

SEG_Q, SEG_F, SEG_I, SEG_G, SEG_U, SEG_A, SEG_B = range(7)

MIXER_ROWS = 512
MOE_ROWS = 256
MOE_COLS = 256
FINAL_ROWS = 512
VMEM_LIMIT = 56 * 1024 * 1024


def _rms(x, g):
    ms = jnp.mean(x * x, axis=-1, keepdims=True)
    return x * lax.rsqrt(ms + EPS) * g


def _sigmoid(x):
    return 1.0 / (1.0 + jnp.exp(-x))


def _dot(a, b):
    return jnp.dot(a, b, preferred_element_type=F32)


def _dot_nt(a, b):
    return lax.dot_general(a, b, (((1,), (1,)), ((), ())), preferred_element_type=F32)


def _dot_tn(a, b):
    return lax.dot_general(a, b, (((0,), (0,)), ((), ())), preferred_element_type=F32)


LANES = 128
ROW_TILE = D_MODEL // LANES


def _store_row_tiles(ref, lead, x, period=ROW_TILE, offset=0):
    rows = x.shape[0]
    for c in range(ROW_TILE):
        ref[lead + (pl.ds(offset + c, rows, stride=period), slice(None))] = x[:, c * LANES:(c + 1) * LANES]


def _load_row_tiles(ref, lead, rows, period=ROW_TILE, offset=0):
    return jnp.concatenate(
        [ref[lead + (pl.ds(offset + c, rows, stride=period), slice(None))] for c in range(ROW_TILE)], axis=-1)


def _split_bf16(x):
    hi = x.astype(BF16)
    lo = (x - hi.astype(F32)).astype(BF16)
    return hi, lo


def _forget_lower_bound(lbl):
    m = jnp.max(lbl, axis=0, keepdims=True)
    e = jnp.exp(lbl - m)
    return e[0:1] / jnp.sum(e, axis=0, keepdims=True)


def _head_norm_gate(o, g_raw, hgn):
    parts = []
    for h in range(HEADS):
        oh = o[:, h * HEAD_DIM:(h + 1) * HEAD_DIM]
        parts.append(_rms(oh, hgn))
    return jnp.concatenate(parts, axis=-1) * (g_raw * _sigmoid(g_raw))


def _route(h2, wr_ref, br_ref, ti_ref, tw_ref):
    rows = h2.shape[0]
    h_hi, h_lo = _split_bf16(h2)
    w_hi, w_lo = _split_bf16(wr_ref[...])
    logits = _dot(h_hi, w_hi) + _dot(h_lo, w_hi) + _dot(h_hi, w_lo) + br_ref[...]
    lane = lax.broadcasted_iota(jnp.int32, (rows, ROUTER_LANES), 1)
    neg = jnp.float32(-jnp.inf)
    l = jnp.where(lane < N_EXPERTS, logits, neg)
    ti = jnp.zeros((rows, ROUTER_LANES), jnp.int32)
    tw = jnp.zeros((rows, ROUTER_LANES), F32)
    m0 = None
    denom = None
    es = []
    for j in range(TOP_K):
        m = jnp.max(l, axis=-1, keepdims=True)
        idx = jnp.min(jnp.where(l == m, lane, ROUTER_LANES), axis=-1, keepdims=True)
        l = jnp.where(lane == idx, neg, l)
        if j == 0:
            m0 = m
        e = jnp.exp(m - m0)
        es.append(e)
        denom = e if denom is None else denom + e
        ti = jnp.where(lane == j, idx, ti)
    for j in range(TOP_K):
        tw = jnp.where(lane == j, es[j] / denom, tw)
    ti_ref[...] = ti
    tw_ref[...] = tw


def _mixer_prompt_kernel(x_ref, gmix_ref, win_ref, lbl_ref, hgn_ref, wpool_ref, pscale_ref, wout_ref, gffn_ref,
                         wr_ref, br_ref,
                         x1_ref, h2_ref, ti_ref, tw_ref, sfin_ref, ptail_ref,
                         st_ref, uext_ref, qe_ref, ke_ref, kd_ref, v_ref, o_ref, p_ref):
    rows = MIXER_ROWS
    n_chunks = rows // CHUNK
    t = pl.program_id(1)
    last_t = pl.num_programs(1) - 1

    @pl.when(t == 0)
    def _():
        st_ref[...] = jnp.zeros_like(st_ref)
        uext_ref[0:POOL_CARRY, :] = jnp.zeros((POOL_CARRY, D_MODEL), F32)

    @pl.when(t > 0)
    def _():
        uext_ref[0:POOL_CARRY, :] = uext_ref[rows:rows + POOL_CARRY, :]

    x = x_ref[0]
    h = _rms(x, gmix_ref[...]).astype(BF16)

    def proj(seg):
        return _dot(h, win_ref[:, seg * D_MODEL:(seg + 1) * D_MODEL])

    lb = _forget_lower_bound(lbl_ref[...])
    q_raw = proj(SEG_Q)
    q = q_raw * _sigmoid(q_raw)
    f = lb + (1.0 - lb) * _sigmoid(proj(SEG_F))
    k = 1.0 - f
    row_in_chunk = lax.broadcasted_iota(jnp.int32, (rows, D_MODEL), 0) % CHUNK
    p = f
    s = 1
    while s < CHUNK:
        p = p * jnp.where(row_in_chunk >= s, pltpu.roll(p, s, axis=0), 1.0)
        s *= 2
    p3 = p.reshape(n_chunks, CHUNK, D_MODEL)
    plast = p3[:, CHUNK - 1:CHUNK, :]
    qe_ref[...] = (q * p).astype(BF16)
    ke_ref[...] = (k / p).astype(BF16)
    kd_ref[...] = (k * (plast / p3).reshape(rows, D_MODEL)).astype(BF16)
    p_ref[...] = p
    v_ref[...] = proj(SEG_I).astype(BF16)

    causal = (lax.broadcasted_iota(jnp.int32, (CHUNK, CHUNK), 0)
              >= lax.broadcasted_iota(jnp.int32, (CHUNK, CHUNK), 1))

    for hd in range(HEADS):
        cols = slice(hd * HEAD_DIM, (hd + 1) * HEAD_DIM)
        st = st_ref[hd]
        for c in range(n_chunks):
            rs = slice(c * CHUNK, (c + 1) * CHUNK)
            qe = qe_ref[rs, cols]
            ke = ke_ref[rs, cols]
            kd = kd_ref[rs, cols]
            vv = v_ref[rs, cols]
            decay = p_ref[(c + 1) * CHUNK - 1:(c + 1) * CHUNK, cols]
            scores = jnp.where(causal, _dot_nt(qe, ke), 0.0)
            o_ref[rs, cols] = _dot_nt(qe, st.astype(BF16)) + _dot(scores.astype(BF16), vv)
            st = st * decay + _dot_tn(vv, kd)
        st_ref[hd] = st

    @pl.when(t == last_t)
    def _():
        for hd in range(HEADS):
            sfin_ref[0, hd] = st_ref[hd].T

    g_raw = proj(SEG_G)
    o = _head_norm_gate(o_ref[...], g_raw, hgn_ref[...])

    u = proj(SEG_U)
    uext_ref[POOL_CARRY:POOL_CARRY + rows, :] = u
    pos1 = t * rows + lax.broadcasted_iota(jnp.int32, (rows, 1), 0) + 1
    pooled = []
    for g, w in enumerate(POOL_WINDOWS):
        cols = slice(g * POOL_GW, (g + 1) * POOL_GW)
        sw = u[:, cols]
        for j in range(1, w):
            sw = sw + uext_ref[POOL_CARRY - j:POOL_CARRY - j + rows, cols]
        inv_cnt = 1.0 / jnp.minimum(pos1, w).astype(F32)
        dg = sw * inv_cnt - u[:, cols]
        pooled.append(_dot(dg.astype(BF16), wpool_ref[g]))
    y_pool = jnp.concatenate(pooled, axis=-1) * pscale_ref[...]

    @pl.when(t == last_t)
    def _():
        ptail_ref[0] = uext_ref[rows + POOL_CARRY - POOL_BUF:rows + POOL_CARRY, :]

    merged = _sigmoid(proj(SEG_A)) * o + _sigmoid(proj(SEG_B)) * y_pool
    x1 = x + _dot(merged.astype(BF16), wout_ref[...])
    x1_ref[0] = x1
    h2 = _rms(x1, gffn_ref[...])
    _store_row_tiles(h2_ref, (), h2)
    _route(h2, wr_ref, br_ref, ti_ref, tw_ref)


def _const_spec(shape):
    zeros = (0,) * len(shape)
    return pl.BlockSpec(shape, lambda *_: zeros, pipeline_mode=pl.Buffered(1))


def _mixer_prompt(x, gmix, win, lbl, hgn, wpool, pscale, wout, gffn, wr, br):
    b, t, _ = x.shape
    rows = MIXER_ROWS
    nt = t // rows
    tok_spec = pl.BlockSpec((1, rows, D_MODEL), lambda i, j: (i, j, 0))
    flat_spec = pl.BlockSpec((rows * ROW_TILE, LANES), lambda i, j: (i * nt + j, 0))
    lane_spec = pl.BlockSpec((rows, ROUTER_LANES), lambda i, j: (i * nt + j, 0))
    return pl.pallas_call(
        _mixer_prompt_kernel,
        grid=(b, nt),
        in_specs=[
            tok_spec,
            _const_spec((1, D_MODEL)),
            _const_spec(win.shape),
            _const_spec(lbl.shape),
            _const_spec((1, HEAD_DIM)),
            _const_spec(wpool.shape),
            _const_spec((1, D_MODEL)),
            _const_spec(wout.shape),
            _const_spec((1, D_MODEL)),
            _const_spec(wr.shape),
            _const_spec(br.shape),
        ],
        out_specs=[
            tok_spec,
            flat_spec,
            lane_spec,
            lane_spec,
            pl.BlockSpec((1, HEADS, HEAD_DIM, HEAD_DIM), lambda i, j: (i, 0, 0, 0)),
            pl.BlockSpec((1, POOL_BUF, D_MODEL), lambda i, j: (i, 0, 0)),
        ],
        out_shape=[
            jax.ShapeDtypeStruct(x.shape, F32),
            jax.ShapeDtypeStruct((b * t * ROW_TILE, LANES), F32),
            jax.ShapeDtypeStruct((b * t, ROUTER_LANES), jnp.int32),
            jax.ShapeDtypeStruct((b * t, ROUTER_LANES), F32),
            jax.ShapeDtypeStruct((b, HEADS, HEAD_DIM, HEAD_DIM), F32),
            jax.ShapeDtypeStruct((b, POOL_BUF, D_MODEL), F32),
        ],
        scratch_shapes=[
            pltpu.VMEM((HEADS, HEAD_DIM, HEAD_DIM), F32),
            pltpu.VMEM((rows + POOL_CARRY, D_MODEL), F32),
            pltpu.VMEM((rows, D_MODEL), BF16),
            pltpu.VMEM((rows, D_MODEL), BF16),
            pltpu.VMEM((rows, D_MODEL), BF16),
            pltpu.VMEM((rows, D_MODEL), BF16),
            pltpu.VMEM((rows, D_MODEL), F32),
            pltpu.VMEM((rows, D_MODEL), F32),
        ],
        compiler_params=pltpu.CompilerParams(dimension_semantics=("arbitrary", "arbitrary"),
                                             vmem_limit_bytes=VMEM_LIMIT),
        name="mixer_prompt",
    )(x, gmix, win, lbl, hgn, wpool, pscale, wout, gffn, wr, br)


SAMPLE_GROUP = 8


def _mixer_sample_kernel(pool_cnt, x_ref, gmix_ref, win_ref, wqft_ref, lblt_ref, hgn_ref, wpool_ref, pscale_ref,
                         wout_ref, gffn_ref, wr_ref, br_ref, s_ref, pbuf_ref,
                         x1_ref, h2_ref, ti_ref, tw_ref, snew_ref, pnew_ref,
                         h_ref, znat_ref, o_ref, bsum_ref):
    step = pl.program_id(0)
    last = pl.num_programs(0) - 1
    r0 = pl.multiple_of(step * SAMPLE_GROUP, SAMPLE_GROUP)

    @pl.when(step == 0)
    def _():
        hf = _rms(x_ref[...], gmix_ref[...])
        h_ref[...] = hf
        h = hf.astype(BF16)
        for i, seg in enumerate((SEG_I, SEG_G, SEG_U, SEG_A, SEG_B)):
            znat_ref[:, i * D_MODEL:(i + 1) * D_MODEL] = _dot(h, win_ref[:, seg * D_MODEL:(seg + 1) * D_MODEL])

    hg = h_ref[pl.ds(r0, SAMPLE_GROUP), :].astype(BF16)
    qft = _dot_nt(wqft_ref[...], hg)
    lbl = lblt_ref[...]
    m = jnp.max(lbl, axis=1, keepdims=True)
    e = jnp.exp(lbl - m)
    lb = e[:, 0:1] / jnp.sum(e, axis=1, keepdims=True)
    q_raw = qft[0:D_MODEL]
    qt = q_raw * _sigmoid(q_raw)
    ft = lb + (1.0 - lb) * _sigmoid(qft[D_MODEL:2 * D_MODEL])
    kt = 1.0 - ft

    for j in range(SAMPLE_GROUP):
        v_row = znat_ref[pl.ds(r0 + j, 1), 0:D_MODEL]
        o_parts = []
        for hd in range(HEADS):
            rs = slice(hd * HEAD_DIM, (hd + 1) * HEAD_DIM)
            s_new = ft[rs, j:j + 1] * s_ref[j, hd] + kt[rs, j:j + 1] * v_row[:, rs]
            snew_ref[j, hd] = s_new
            o_parts.append(jnp.sum(qt[rs, j:j + 1] * s_new, axis=0, keepdims=True))
        o_ref[pl.ds(r0 + j, 1), :] = jnp.concatenate(o_parts, axis=-1)

    u_g = znat_ref[pl.ds(r0, SAMPLE_GROUP), 2 * D_MODEL:3 * D_MODEL]
    sums = []
    for g, w in enumerate(POOL_WINDOWS):
        acc = jnp.zeros((SAMPLE_GROUP, POOL_GW), F32)
        for j in range(1, w):
            row = POOL_BUF - j
            acc = acc + pbuf_ref[:, row * D_MODEL + g * POOL_GW:row * D_MODEL + (g + 1) * POOL_GW]
        sums.append(acc)
    bsum_ref[pl.ds(r0, SAMPLE_GROUP), :] = jnp.concatenate(sums, axis=-1)
    pnew_ref[:, 0:(POOL_BUF - 1) * D_MODEL] = pbuf_ref[:, D_MODEL:POOL_BUF * D_MODEL]
    pnew_ref[:, (POOL_BUF - 1) * D_MODEL:POOL_BUF * D_MODEL] = u_g

    @pl.when(step == last)
    def _():
        x = x_ref[...]
        g_raw = znat_ref[:, D_MODEL:2 * D_MODEL]
        u = znat_ref[:, 2 * D_MODEL:3 * D_MODEL]
        o = _head_norm_gate(o_ref[...], g_raw, hgn_ref[...])
        sw = bsum_ref[...] + u
        pooled = []
        for g, w in enumerate(POOL_WINDOWS):
            cols = slice(g * POOL_GW, (g + 1) * POOL_GW)
            dg = sw[:, cols] * (1.0 / pool_cnt[g]) - u[:, cols]
            pooled.append(_dot(dg.astype(BF16), wpool_ref[g]))
        y_pool = jnp.concatenate(pooled, axis=-1) * pscale_ref[...]
        merged = (_sigmoid(znat_ref[:, 3 * D_MODEL:4 * D_MODEL]) * o
                  + _sigmoid(znat_ref[:, 4 * D_MODEL:5 * D_MODEL]) * y_pool)
        x1 = x + _dot(merged.astype(BF16), wout_ref[...])
        x1_ref[...] = x1
        h2 = _rms(x1, gffn_ref[...])
        _store_row_tiles(h2_ref, (), h2)
        _route(h2, wr_ref, br_ref, ti_ref, tw_ref)


def _mixer_sample(x, gmix, win, wqft, lblt, hgn, wpool, pscale, wout, gffn, wr, br, state, pbuf, start_pos):
    n = x.shape[0]
    steps = n // SAMPLE_GROUP
    pool_cnt = tuple(float(min(start_pos + 1, w)) for w in POOL_WINDOWS)
    pbuf2 = pbuf.reshape(n, POOL_BUF * D_MODEL)
    full = _const_spec
    out = pl.pallas_call(
        functools.partial(_mixer_sample_kernel, pool_cnt),
        grid=(steps,),
        in_specs=[
            full((n, D_MODEL)),
            full((1, D_MODEL)),
            full(win.shape),
            full(wqft.shape),
            full(lblt.shape),
            full((1, HEAD_DIM)),
            full(wpool.shape),
            full((1, D_MODEL)),
            full(wout.shape),
            full((1, D_MODEL)),
            full(wr.shape),
            full(br.shape),
            pl.BlockSpec((SAMPLE_GROUP, HEADS, HEAD_DIM, HEAD_DIM), lambda i: (i, 0, 0, 0)),
            pl.BlockSpec((SAMPLE_GROUP, POOL_BUF * D_MODEL), lambda i: (i, 0)),
        ],
        out_specs=[
            pl.BlockSpec((n, D_MODEL), lambda i: (0, 0)),
            pl.BlockSpec((n * ROW_TILE, LANES), lambda i: (0, 0)),
            pl.BlockSpec((n, ROUTER_LANES), lambda i: (0, 0)),
            pl.BlockSpec((n, ROUTER_LANES), lambda i: (0, 0)),
            pl.BlockSpec((SAMPLE_GROUP, HEADS, HEAD_DIM, HEAD_DIM), lambda i: (i, 0, 0, 0)),
            pl.BlockSpec((SAMPLE_GROUP, POOL_BUF * D_MODEL), lambda i: (i, 0)),
        ],
        out_shape=[
            jax.ShapeDtypeStruct((n, D_MODEL), F32),
            jax.ShapeDtypeStruct((n * ROW_TILE, LANES), F32),
            jax.ShapeDtypeStruct((n, ROUTER_LANES), jnp.int32),
            jax.ShapeDtypeStruct((n, ROUTER_LANES), F32),
            jax.ShapeDtypeStruct(state.shape, F32),
            jax.ShapeDtypeStruct(pbuf2.shape, F32),
        ],
        scratch_shapes=[
            pltpu.VMEM((n, D_MODEL), F32),
            pltpu.VMEM((n, 5 * D_MODEL), F32),
            pltpu.VMEM((n, D_MODEL), F32),
            pltpu.VMEM((n, D_MODEL), F32),
        ],
        compiler_params=pltpu.CompilerParams(dimension_semantics=("arbitrary",), vmem_limit_bytes=VMEM_LIMIT),
        name="mixer_sample",
    )(x, gmix, win, wqft, lblt, hgn, wpool, pscale, wout, gffn, wr, br, state, pbuf2)
    x1, h2, ti, tw, snew, pnew = out
    return x1, h2, ti, tw, snew, pnew.reshape(n, POOL_BUF, D_MODEL)


def _moe_kernel(be_ref, off_ref, nv_ref, nu_ref, sa_ref, h2_hbm,
                wg_ref, bg_ref, wu_ref, bu_ref, wd_ref, bd_ref, y_hbm,
                xbuf0, xbuf1, ybuf0, ybuf1, wgb, wub, wdb, gsem, ssem):
    rows = MOE_ROWS
    i = pl.program_id(0)
    n_used = nu_ref[0]
    n_blocks = pl.num_programs(0)
    prev = jnp.maximum(i - 1, 0)
    nv_prev = jnp.where(i >= 1, nv_ref[prev], 0)
    xbuf = (xbuf0, xbuf1)
    ybuf = (ybuf0, ybuf1)

    def tile_rows(first, n=1):
        start = first * ROW_TILE
        return pl.ds(start if isinstance(first, int) else pl.multiple_of(start, ROW_TILE), n * ROW_TILE)

    def gather_copy(a, r, s):
        tok = a >> 2
        return pltpu.make_async_copy(h2_hbm.at[tile_rows(tok), :], xbuf[s].at[tile_rows(r), :], gsem.at[s])

    def scatter_copy(a, r, n, s):
        return pltpu.make_async_copy(ybuf[s].at[tile_rows(r, n), :], y_hbm.at[tile_rows(a, n), :], ssem.at[s])

    def wait_gather(s):
        pltpu.make_async_copy(h2_hbm.at[tile_rows(0, rows), :], xbuf[s], gsem.at[s]).wait()

    def wait_scatter(n, s):
        size = rows
        while size >= 1:
            @pl.when((n & size) != 0)
            def _():
                scatter_copy(0, 0, size, s).wait()
            size //= 2

    def scatter_loop(blk, n, s):
        def body(r, c):
            scatter_copy(sa_ref[off_ref[blk] + r], r, 1, s).start()
            return c
        lax.fori_loop(0, n, body, 0)

    def step(cur, full_prev):
        oth = 1 - cur
        wait_gather(cur)

        @pl.when(i >= 2)
        def _():
            wait_scatter(nv_ref[jnp.maximum(i - 2, 0)], cur)

        off_next = off_ref[jnp.minimum(i + 1, n_blocks - 1)]
        off_prev = off_ref[prev]
        n_tiles = D_MODEL // MOE_COLS
        n_groups = 3 * n_tiles
        bounds = [rows * g // n_groups for g in range(n_groups + 1)]
        group = iter(range(n_groups))

        def start_copies():
            g = next(group)
            for r in range(bounds[g], bounds[g + 1]):
                gather_copy(sa_ref[off_next + r], r, oth).start()
                if full_prev:
                    scatter_copy(sa_ref[off_prev + r], r, 1, oth).start()

        xb = _load_row_tiles(xbuf[cur], (), rows).astype(BF16)
        gate, act = [], []
        for j in range(n_tiles):
            cs = slice(j * MOE_COLS, (j + 1) * MOE_COLS)
            start_copies()
            gate.append(jnp.minimum(_dot(xb, wgb[:, cs]) + bg_ref[0, :, cs], SWIGLU_LIMIT))
        for j in range(n_tiles):
            cs = slice(j * MOE_COLS, (j + 1) * MOE_COLS)
            start_copies()
            up = jnp.clip(_dot(xb, wub[:, cs]) + bu_ref[0, :, cs], -SWIGLU_LIMIT, SWIGLU_LIMIT)
            act.append(((up + 1.0) * gate[j] * _sigmoid(SWIGLU_ALPHA * gate[j])).astype(BF16))
        act = jnp.concatenate(act, axis=-1)
        for j in range(n_tiles):
            cs = slice(j * MOE_COLS, (j + 1) * MOE_COLS)
            start_copies()
            yj = _dot(act, wdb[:, cs]) + bd_ref[0, :, cs]
            for c in range(MOE_COLS // LANES):
                ybuf[cur][pl.ds(j * (MOE_COLS // LANES) + c, rows, stride=ROW_TILE), :] = yj[:, c * LANES:(c + 1) * LANES]
        if not full_prev:
            scatter_loop(prev, nv_prev, oth)

        @pl.when(i == n_used - 1)
        def _():
            scatter_loop(i, nv_ref[i], cur)
            wait_scatter(nv_prev, oth)
            wait_scatter(nv_ref[i], cur)
            wait_gather(oth)

    @pl.when(i == 0)
    def _():
        def body(r, c):
            gather_copy(sa_ref[off_ref[0] + r], r, 0).start()
            return c
        lax.fori_loop(0, rows, body, 0)

    @pl.when(i < n_used)
    def _():
        first_of_expert = jnp.logical_or(i == 0, be_ref[i] != be_ref[prev])

        @pl.when(first_of_expert)
        def _():
            wgb[...] = wg_ref[0].astype(BF16)
            wub[...] = wu_ref[0].astype(BF16)
            wdb[...] = wd_ref[0].astype(BF16)

        prev_is_full = nv_prev == rows
        for cur in (0, 1):
            for full_prev in (True, False):
                @pl.when(jnp.logical_and(i % 2 == cur, prev_is_full if full_prev else jnp.logical_not(prev_is_full)))
                def _():
                    step(cur, full_prev)


def _moe(h2, routing, wg, bg, wu, bu, wd, bd):
    rows = MOE_ROWS
    n_tokens = h2.shape[0] // ROW_TILE
    block_e, block_off, block_nv, n_used, sorted_a = routing
    n_blocks = block_e.shape[0]
    w_spec = pl.BlockSpec((1, D_MODEL, D_MODEL), lambda i, be, *_: (be[i], 0, 0))
    b_spec = pl.BlockSpec((1, 1, D_MODEL), lambda i, be, *_: (be[i], 0, 0))
    grid_spec = pltpu.PrefetchScalarGridSpec(
        num_scalar_prefetch=5,
        grid=(n_blocks,),
        in_specs=[pl.BlockSpec(memory_space=pl.ANY), w_spec, b_spec, w_spec, b_spec, w_spec, b_spec],
        out_specs=pl.BlockSpec(memory_space=pl.ANY),
        scratch_shapes=[
            pltpu.VMEM((rows * ROW_TILE, LANES), F32),
            pltpu.VMEM((rows * ROW_TILE, LANES), F32),
            pltpu.VMEM((rows * ROW_TILE, LANES), F32),
            pltpu.VMEM((rows * ROW_TILE, LANES), F32),
            pltpu.VMEM((D_MODEL, D_MODEL), BF16),
            pltpu.VMEM((D_MODEL, D_MODEL), BF16),
            pltpu.VMEM((D_MODEL, D_MODEL), BF16),
            pltpu.SemaphoreType.DMA((2,)),
            pltpu.SemaphoreType.DMA((2,)),
        ],
    )
    return pl.pallas_call(
        _moe_kernel,
        grid_spec=grid_spec,
        out_shape=jax.ShapeDtypeStruct((n_tokens * TOP_K * ROW_TILE, LANES), F32),
        compiler_params=pltpu.CompilerParams(dimension_semantics=("arbitrary",), vmem_limit_bytes=VMEM_LIMIT),
        name="moe_experts",
    )(block_e, block_off, block_nv, n_used, sorted_a, h2, wg, bg, wu, bu, wd, bd)


ASSIGN_BITS = 17


def _moe_routing(top_i):
    rows = MOE_ROWS
    n_assign = top_i.shape[0] * TOP_K
    assert n_assign <= 1 << ASSIGN_BITS
    n_blocks = -(-n_assign // rows) + N_EXPERTS
    flat_e = top_i.reshape(-1)
    keys = jnp.sort(flat_e * (1 << ASSIGN_BITS) + jnp.arange(n_assign, dtype=jnp.int32))
    smem_len = -(-(n_assign + rows) // 1024) * 1024
    sorted_a = jnp.pad(keys & ((1 << ASSIGN_BITS) - 1), (0, smem_len - n_assign))
    experts = jnp.arange(N_EXPERTS, dtype=jnp.int32)
    counts = jnp.sum((flat_e[:, None] == experts[None, :]).astype(jnp.int32), axis=0)
    nblk = (counts + rows - 1) // rows
    blk_end = jnp.cumsum(nblk)
    blk_start = blk_end - nblk
    row_start = jnp.cumsum(counts) - counts
    blk = jnp.arange(n_blocks, dtype=jnp.int32)
    owner = (blk[:, None] >= blk_start[None, :]) & (blk[:, None] < blk_end[None, :])
    pick = lambda v: jnp.sum(jnp.where(owner, v[None, :], 0), axis=1)
    within = (blk - pick(blk_start)) * rows
    used = blk < blk_end[-1]
    block_e = jnp.where(used, pick(experts), N_EXPERTS - 1).astype(jnp.int32)
    block_off = jnp.where(used, pick(row_start) + within, 0).astype(jnp.int32)
    block_nv = jnp.where(used, jnp.clip(pick(counts) - within, 0, rows), 0).astype(jnp.int32)
    n_used = blk_end[-1].astype(jnp.int32).reshape(1)
    return block_e, block_off, block_nv, n_used, sorted_a.astype(jnp.int32)


def _final_kernel(x1_ref, y_ref, tw_ref, p_ref, gple_ref, wpg_ref, wpp_ref, gfin_ref, out_ref):
    x = x1_ref[...]
    tw = tw_ref[...]
    rows = x.shape[0]
    for j in range(TOP_K):
        x = x + tw[:, j:j + 1] * _load_row_tiles(y_ref, (), rows, period=TOP_K * ROW_TILE, offset=j * ROW_TILE)
    gate = _sigmoid(_dot(_rms(x, gple_ref[...]).astype(BF16), wpg_ref[...]))
    x = x + gate * _dot(p_ref[...].astype(BF16), wpp_ref[...])
    out_ref[...] = _rms(x, gfin_ref[...])


def _final(x1, y4, tw, p, gple, wpg, wpp, gfin, rows, row_offset):
    n = x1.shape[0]
    off = row_offset // rows
    assert off * rows == row_offset and n % rows == 0
    ple = p.shape[-1]
    return pl.pallas_call(
        _final_kernel,
        grid=(n // rows,),
        in_specs=[
            pl.BlockSpec((rows, D_MODEL), lambda i: (i, 0)),
            pl.BlockSpec((rows * TOP_K * ROW_TILE, LANES), lambda i: (i + off, 0)),
            pl.BlockSpec((rows, ROUTER_LANES), lambda i: (i, 0)),
            pl.BlockSpec((rows, ple), lambda i: (i, 0)),
            _const_spec((1, D_MODEL)),
            _const_spec(wpg.shape),
            _const_spec(wpp.shape),
            _const_spec((1, D_MODEL)),
        ],
        out_specs=pl.BlockSpec((rows, D_MODEL), lambda i: (i, 0)),
        out_shape=jax.ShapeDtypeStruct((n, D_MODEL), F32),
        compiler_params=pltpu.CompilerParams(dimension_semantics=("arbitrary",), vmem_limit_bytes=VMEM_LIMIT),
        name="final",
    )(x1, y4, tw, p, gple, wpg, wpp, gfin)


def kernel(x_prompt, x_sample, p_prompt, p_sample, state_hgrn, state_pool, g_mix, w_in, hg_lb_logits, hg_norm, w_pool,
           pool_scale, w_out, g_ffn, w_router, b_router, w_gate, b_gate, w_up, b_up, w_down, b_down, g_ple,
           w_ple_gate, w_ple_proj, g_final):
    depth = w_in.shape[0]
    assert depth == 1, "single-layer step"
    b, t, _ = x_prompt.shape
    ns = x_sample.shape[0]
    assert x_sample.shape[1] == 1
    n_prompt = b * t
    n_tok = n_prompt + ns

    row = lambda a: a.reshape(1, -1)
    win = w_in[0].astype(BF16)
    wqft = win[:, :2 * D_MODEL].T
    lbl = hg_lb_logits.astype(F32)
    wpool = w_pool[0].astype(BF16)
    wout = w_out[0].astype(BF16)
    wr = jnp.pad(w_router[0], ((0, 0), (0, ROUTER_LANES - N_EXPERTS)))
    br = jnp.pad(b_router[0], (0, ROUTER_LANES - N_EXPERTS)).reshape(1, -1)
    shared = (row(hg_norm[0]), wpool, row(pool_scale[0]), wout, row(g_ffn[0]), wr, br)

    x1_p, h2_p, ti_p, tw_p, s_p, pool_p = _mixer_prompt(x_prompt, row(g_mix[0]), win, lbl, *shared)
    x1_s, h2_s, ti_s, tw_s, s_s, pool_s = _mixer_sample(
        x_sample.reshape(ns, D_MODEL), row(g_mix[0]), win, wqft, lbl.T, *shared,
        state_hgrn[0], state_pool[0], start_pos=PAST_LEN)

    top_i = jnp.concatenate([ti_p[:, :TOP_K], ti_s[:, :TOP_K]], axis=0)
    routing = _moe_routing(top_i)
    e3 = lambda a: a[0].reshape(N_EXPERTS, 1, D_MODEL)
    h2 = jnp.concatenate([h2_p, h2_s], axis=0)
    y4 = _moe(h2, routing, w_gate[0], e3(b_gate), w_up[0], e3(b_up), w_down[0], e3(b_down))

    wpg = w_ple_gate[0].astype(BF16)
    wpp = w_ple_proj[0].astype(BF16)
    fin = (row(g_ple[0]), wpg, wpp, row(g_final))
    y_p = _final(x1_p.reshape(n_prompt, D_MODEL), y4, tw_p, p_prompt[0].reshape(n_prompt, -1), *fin,
                 rows=FINAL_ROWS, row_offset=0)
    y_s = _final(x1_s, y4, tw_s, p_sample[0].reshape(ns, -1), *fin, rows=ns, row_offset=n_prompt)

    return (y_p.reshape(b, t, D_MODEL), y_s.reshape(ns, 1, D_MODEL), s_p[None], pool_p[None], s_s[None],
            pool_s[None])
```

```python
import functools

import jax
import jax.numpy as jnp
from jax import lax
from jax.experimental import pallas as pl
from jax.experimental.pallas import tpu as pltpu

F32 = jnp.float32
BF16 = jnp.bfloat16

D_MODEL = 1024
HEADS = 8
HEAD_DIM = 128
CHUNK = 32
POOL_WINDOWS = (2, 4, 8, 16)
POOL_GW = D_MODEL // len(POOL_WINDOWS)
POOL_BUF = 15
POOL_CARRY = 16
N_EXPERTS = 32
TOP_K = 4
ROUTER_LANES = 128
SWIGLU_LIMIT = 7.0
SWIGLU_ALPHA = 1.702
EPS = 1e-6
PAST_LEN = 16384

SEG_Q, SEG_F, SEG_I, SEG_G, SEG_U, SEG_A, SEG_B = range(7)

MIXER_ROWS = 512
MOE_ROWS = 256
FINAL_ROWS = 512
VMEM_LIMIT = 56 * 1024 * 1024


def _rms(x, g):
    ms = jnp.mean(x * x, axis=-1, keepdims=True)
    return x * lax.rsqrt(ms + EPS) * g


def _sigmoid(x):
    return 1.0 / (1.0 + jnp.exp(-x))


def _dot(a, b):
    return jnp.dot(a, b, preferred_element_type=F32)


def _dot_nt(a, b):
    return lax.dot_general(a, b, (((1,), (1,)), ((), ())), preferred_element_type=F32)


def _dot_tn(a, b):
    return lax.dot_general(a, b, (((0,), (0,)), ((), ())), preferred_element_type=F32)


LANES = 128
ROW_TILE = D_MODEL // LANES


def _store_row_tiles(ref, lead, x, period=ROW_TILE, offset=0):
    rows = x.shape[0]
    for c in range(ROW_TILE):
        ref[lead + (pl.ds(offset + c, rows, stride=period), slice(None))] = x[:, c * LANES:(c + 1) * LANES]


def _load_row_tiles(ref, lead, rows, period=ROW_TILE, offset=0):
    return jnp.concatenate(
        [ref[lead + (pl.ds(offset + c, rows, stride=period), slice(None))] for c in range(ROW_TILE)], axis=-1)


def _split_bf16(x):
    hi = x.astype(BF16)
    lo = (x - hi.astype(F32)).astype(BF16)
    return hi, lo


def _forget_lower_bound(lbl):
    m = jnp.max(lbl, axis=0, keepdims=True)
    e = jnp.exp(lbl - m)
    return e[0:1] / jnp.sum(e, axis=0, keepdims=True)


def _head_norm_gate(o, g_raw, hgn):
    parts = []
    for h in range(HEADS):
        oh = o[:, h * HEAD_DIM:(h + 1) * HEAD_DIM]
        parts.append(_rms(oh, hgn))
    return jnp.concatenate(parts, axis=-1) * (g_raw * _sigmoid(g_raw))


def _route(h2, wr_ref, br_ref, ti_ref, tw_ref):
    rows = h2.shape[0]
    h_hi, h_lo = _split_bf16(h2)
    w_hi, w_lo = _split_bf16(wr_ref[...])
    logits = _dot(h_hi, w_hi) + _dot(h_lo, w_hi) + _dot(h_hi, w_lo) + br_ref[...]
    lane = lax.broadcasted_iota(jnp.int32, (rows, ROUTER_LANES), 1)
    neg = jnp.float32(-jnp.inf)
    l = jnp.where(lane < N_EXPERTS, logits, neg)
    ti = jnp.zeros((rows, ROUTER_LANES), jnp.int32)
    tw = jnp.zeros((rows, ROUTER_LANES), F32)
    m0 = None
    denom = None
    es = []
    for j in range(TOP_K):
        m = jnp.max(l, axis=-1, keepdims=True)
        idx = jnp.min(jnp.where(l == m, lane, ROUTER_LANES), axis=-1, keepdims=True)
        l = jnp.where(lane == idx, neg, l)
        if j == 0:
            m0 = m
        e = jnp.exp(m - m0)
        es.append(e)
        denom = e if denom is None else denom + e
        ti = jnp.where(lane == j, idx, ti)
    for j in range(TOP_K):
        tw = jnp.where(lane == j, es[j] / denom, tw)
    ti_ref[...] = ti
    tw_ref[...] = tw


def _mixer_prompt_kernel(x_ref, gmix_ref, win_ref, lbl_ref, hgn_ref, wpool_ref, pscale_ref, wout_ref, gffn_ref,
                         wr_ref, br_ref,
                         x1_ref, h2_ref, ti_ref, tw_ref, sfin_ref, ptail_ref,
                         st_ref, uext_ref, qe_ref, ke_ref, kd_ref, v_ref, o_ref, p_ref):
    rows = MIXER_ROWS
    n_chunks = rows // CHUNK
    t = pl.program_id(1)
    last_t = pl.num_programs(1) - 1

    @pl.when(t == 0)
    def _():
        st_ref[...] = jnp.zeros_like(st_ref)
        uext_ref[0:POOL_CARRY, :] = jnp.zeros((POOL_CARRY, D_MODEL), F32)

    @pl.when(t > 0)
    def _():
        uext_ref[0:POOL_CARRY, :] = uext_ref[rows:rows + POOL_CARRY, :]

    x = x_ref[0]
    h = _rms(x, gmix_ref[...]).astype(BF16)

    def proj(seg):
        return _dot(h, win_ref[:, seg * D_MODEL:(seg + 1) * D_MODEL])

    lb = _forget_lower_bound(lbl_ref[...])
    q_raw = proj(SEG_Q)
    q = q_raw * _sigmoid(q_raw)
    f = lb + (1.0 - lb) * _sigmoid(proj(SEG_F))
    k = 1.0 - f
    row_in_chunk = lax.broadcasted_iota(jnp.int32, (rows, D_MODEL), 0) % CHUNK
    p = f
    s = 1
    while s < CHUNK:
        p = p * jnp.where(row_in_chunk >= s, pltpu.roll(p, s, axis=0), 1.0)
        s *= 2
    p3 = p.reshape(n_chunks, CHUNK, D_MODEL)
    plast = p3[:, CHUNK - 1:CHUNK, :]
    qe_ref[...] = (q * p).astype(BF16)
    ke_ref[...] = (k / p).astype(BF16)
    kd_ref[...] = (k * (plast / p3).reshape(rows, D_MODEL)).astype(BF16)
    p_ref[...] = p
    v_ref[...] = proj(SEG_I).astype(BF16)

    causal = (lax.broadcasted_iota(jnp.int32, (CHUNK, CHUNK), 0)
              >= lax.broadcasted_iota(jnp.int32, (CHUNK, CHUNK), 1))

    for hd in range(HEADS):
        cols = slice(hd * HEAD_DIM, (hd + 1) * HEAD_DIM)
        st = st_ref[hd]
        for c in range(n_chunks):
            rs = slice(c * CHUNK, (c + 1) * CHUNK)
            qe = qe_ref[rs, cols]
            ke = ke_ref[rs, cols]
            kd = kd_ref[rs, cols]
            vv = v_ref[rs, cols]
            decay = p_ref[(c + 1) * CHUNK - 1:(c + 1) * CHUNK, cols]
            scores = jnp.where(causal, _dot_nt(qe, ke), 0.0)
            o_ref[rs, cols] = _dot_nt(qe, st.astype(BF16)) + _dot(scores.astype(BF16), vv)
            st = st * decay + _dot_tn(vv, kd)
        st_ref[hd] = st

    @pl.when(t == last_t)
    def _():
        for hd in range(HEADS):
            sfin_ref[0, hd] = st_ref[hd].T

    g_raw = proj(SEG_G)
    o = _head_norm_gate(o_ref[...], g_raw, hgn_ref[...])

    u = proj(SEG_U)
    uext_ref[POOL_CARRY:POOL_CARRY + rows, :] = u
    pos1 = t * rows + lax.broadcasted_iota(jnp.int32, (rows, 1), 0) + 1
    pooled = []
    for g, w in enumerate(POOL_WINDOWS):
        cols = slice(g * POOL_GW, (g + 1) * POOL_GW)
        sw = u[:, cols]
        for j in range(1, w):
            sw = sw + uext_ref[POOL_CARRY - j:POOL_CARRY - j + rows, cols]
        inv_cnt = 1.0 / jnp.minimum(pos1, w).astype(F32)
        dg = sw * inv_cnt - u[:, cols]
        pooled.append(_dot(dg.astype(BF16), wpool_ref[g]))
    y_pool = jnp.concatenate(pooled, axis=-1) * pscale_ref[...]

    @pl.when(t == last_t)
    def _():
        ptail_ref[0] = uext_ref[rows + POOL_CARRY - POOL_BUF:rows + POOL_CARRY, :]

    merged = _sigmoid(proj(SEG_A)) * o + _sigmoid(proj(SEG_B)) * y_pool
    x1 = x + _dot(merged.astype(BF16), wout_ref[...])
    x1_ref[0] = x1
    h2 = _rms(x1, gffn_ref[...])
    _store_row_tiles(h2_ref, (), h2)
    _route(h2, wr_ref, br_ref, ti_ref, tw_ref)


def _const_spec(shape):
    zeros = (0,) * len(shape)
    return pl.BlockSpec(shape, lambda *_: zeros, pipeline_mode=pl.Buffered(1))


def _mixer_prompt(x, gmix, win, lbl, hgn, wpool, pscale, wout, gffn, wr, br):
    b, t, _ = x.shape
    rows = MIXER_ROWS
    nt = t // rows
    tok_spec = pl.BlockSpec((1, rows, D_MODEL), lambda i, j: (i, j, 0))
    flat_spec = pl.BlockSpec((rows * ROW_TILE, LANES), lambda i, j: (i * nt + j, 0))
    lane_spec = pl.BlockSpec((rows, ROUTER_LANES), lambda i, j: (i * nt + j, 0))
    return pl.pallas_call(
        _mixer_prompt_kernel,
        grid=(b, nt),
        in_specs=[
            tok_spec,
            _const_spec((1, D_MODEL)),
            _const_spec(win.shape),
            _const_spec(lbl.shape),
            _const_spec((1, HEAD_DIM)),
            _const_spec(wpool.shape),
            _const_spec((1, D_MODEL)),
            _const_spec(wout.shape),
            _const_spec((1, D_MODEL)),
            _const_spec(wr.shape),
            _const_spec(br.shape),
        ],
        out_specs=[
            tok_spec,
            flat_spec,
            lane_spec,
            lane_spec,
            pl.BlockSpec((1, HEADS, HEAD_DIM, HEAD_DIM), lambda i, j: (i, 0, 0, 0)),
            pl.BlockSpec((1, POOL_BUF, D_MODEL), lambda i, j: (i, 0, 0)),
        ],
        out_shape=[
            jax.ShapeDtypeStruct(x.shape, F32),
            jax.ShapeDtypeStruct((b * t * ROW_TILE, LANES), F32),
            jax.ShapeDtypeStruct((b * t, ROUTER_LANES), jnp.int32),
            jax.ShapeDtypeStruct((b * t, ROUTER_LANES), F32),
            jax.ShapeDtypeStruct((b, HEADS, HEAD_DIM, HEAD_DIM), F32),
            jax.ShapeDtypeStruct((b, POOL_BUF, D_MODEL), F32),
        ],
        scratch_shapes=[
            pltpu.VMEM((HEADS, HEAD_DIM, HEAD_DIM), F32),
            pltpu.VMEM((rows + POOL_CARRY, D_MODEL), F32),
            pltpu.VMEM((rows, D_MODEL), BF16),
            pltpu.VMEM((rows, D_MODEL), BF16),
            pltpu.VMEM((rows, D_MODEL), BF16),
            pltpu.VMEM((rows, D_MODEL), BF16),
            pltpu.VMEM((rows, D_MODEL), F32),
            pltpu.VMEM((rows, D_MODEL), F32),
        ],
        compiler_params=pltpu.CompilerParams(dimension_semantics=("arbitrary", "arbitrary"),
                                             vmem_limit_bytes=VMEM_LIMIT),
        name="mixer_prompt",
    )(x, gmix, win, lbl, hgn, wpool, pscale, wout, gffn, wr, br)


SAMPLE_GROUP = 8


def _mixer_sample_kernel(pool_cnt, x_ref, gmix_ref, win_ref, wqft_ref, lblt_ref, hgn_ref, wpool_ref, pscale_ref,
                         wout_ref, gffn_ref, wr_ref, br_ref, s_ref, pbuf_ref,
                         x1_ref, h2_ref, ti_ref, tw_ref, snew_ref, pnew_ref,
                         h_ref, znat_ref, o_ref, bsum_ref):
    step = pl.program_id(0)
    last = pl.num_programs(0) - 1
    r0 = pl.multiple_of(step * SAMPLE_GROUP, SAMPLE_GROUP)

    @pl.when(step == 0)
    def _():
        hf = _rms(x_ref[...], gmix_ref[...])
        h_ref[...] = hf
        h = hf.astype(BF16)
        for i, seg in enumerate((SEG_I, SEG_G, SEG_U, SEG_A, SEG_B)):
            znat_ref[:, i * D_MODEL:(i + 1) * D_MODEL] = _dot(h, win_ref[:, seg * D_MODEL:(seg + 1) * D_MODEL])

    hg = h_ref[pl.ds(r0, SAMPLE_GROUP), :].astype(BF16)
    qft = _dot_nt(wqft_ref[...], hg)
    lbl = lblt_ref[...]
    m = jnp.max(lbl, axis=1, keepdims=True)
    e = jnp.exp(lbl - m)
    lb = e[:, 0:1] / jnp.sum(e, axis=1, keepdims=True)
    q_raw = qft[0:D_MODEL]
    qt = q_raw * _sigmoid(q_raw)
    ft = lb + (1.0 - lb) * _sigmoid(qft[D_MODEL:2 * D_MODEL])
    kt = 1.0 - ft

    for j in range(SAMPLE_GROUP):
        v_row = znat_ref[pl.ds(r0 + j, 1), 0:D_MODEL]
        o_parts = []
        for hd in range(HEADS):
            rs = slice(hd * HEAD_DIM, (hd + 1) * HEAD_DIM)
            s_new = ft[rs, j:j + 1] * s_ref[j, hd] + kt[rs, j:j + 1] * v_row[:, rs]
            snew_ref[j, hd] = s_new
            o_parts.append(jnp.sum(qt[rs, j:j + 1] * s_new, axis=0, keepdims=True))
        o_ref[pl.ds(r0 + j, 1), :] = jnp.concatenate(o_parts, axis=-1)

    u_g = znat_ref[pl.ds(r0, SAMPLE_GROUP), 2 * D_MODEL:3 * D_MODEL]
    sums = []
    for g, w in enumerate(POOL_WINDOWS):
        acc = jnp.zeros((SAMPLE_GROUP, POOL_GW), F32)
        for j in range(1, w):
            row = POOL_BUF - j
            acc = acc + pbuf_ref[:, row * D_MODEL + g * POOL_GW:row * D_MODEL + (g + 1) * POOL_GW]
        sums.append(acc)
    bsum_ref[pl.ds(r0, SAMPLE_GROUP), :] = jnp.concatenate(sums, axis=-1)
    pnew_ref[:, 0:(POOL_BUF - 1) * D_MODEL] = pbuf_ref[:, D_MODEL:POOL_BUF * D_MODEL]
    pnew_ref[:, (POOL_BUF - 1) * D_MODEL:POOL_BUF * D_MODEL] = u_g

    @pl.when(step == last)
    def _():
        x = x_ref[...]
        g_raw = znat_ref[:, D_MODEL:2 * D_MODEL]
        u = znat_ref[:, 2 * D_MODEL:3 * D_MODEL]
        o = _head_norm_gate(o_ref[...], g_raw, hgn_ref[...])
        sw = bsum_ref[...] + u
        pooled = []
        for g, w in enumerate(POOL_WINDOWS):
            cols = slice(g * POOL_GW, (g + 1) * POOL_GW)
            dg = sw[:, cols] * (1.0 / pool_cnt[g]) - u[:, cols]
            pooled.append(_dot(dg.astype(BF16), wpool_ref[g]))
        y_pool = jnp.concatenate(pooled, axis=-1) * pscale_ref[...]
        merged = (_sigmoid(znat_ref[:, 3 * D_MODEL:4 * D_MODEL]) * o
                  + _sigmoid(znat_ref[:, 4 * D_MODEL:5 * D_MODEL]) * y_pool)
        x1 = x + _dot(merged.astype(BF16), wout_ref[...])
        x1_ref[...] = x1
        h2 = _rms(x1, gffn_ref[...])
        _store_row_tiles(h2_ref, (), h2)
        _route(h2, wr_ref, br_ref, ti_ref, tw_ref)


def _mixer_sample(x, gmix, win, wqft, lblt, hgn, wpool, pscale, wout, gffn, wr, br, state, pbuf, start_pos):
    n = x.shape[0]
    steps = n // SAMPLE_GROUP
    pool_cnt = tuple(float(min(start_pos + 1, w)) for w in POOL_WINDOWS)
    pbuf2 = pbuf.reshape(n, POOL_BUF * D_MODEL)
    full = _const_spec
    out = pl.pallas_call(
        functools.partial(_mixer_sample_kernel, pool_cnt),
        grid=(steps,),
        in_specs=[
            full((n, D_MODEL)),
            full((1, D_MODEL)),
            full(win.shape),
            full(wqft.shape),
            full(lblt.shape),
            full((1, HEAD_DIM)),
            full(wpool.shape),
            full((1, D_MODEL)),
            full(wout.shape),
            full((1, D_MODEL)),
            full(wr.shape),
            full(br.shape),
            pl.BlockSpec((SAMPLE_GROUP, HEADS, HEAD_DIM, HEAD_DIM), lambda i: (i, 0, 0, 0)),
            pl.BlockSpec((SAMPLE_GROUP, POOL_BUF * D_MODEL), lambda i: (i, 0)),
        ],
        out_specs=[
            pl.BlockSpec((n, D_MODEL), lambda i: (0, 0)),
            pl.BlockSpec((n * ROW_TILE, LANES), lambda i: (0, 0)),
            pl.BlockSpec((n, ROUTER_LANES), lambda i: (0, 0)),
            pl.BlockSpec((n, ROUTER_LANES), lambda i: (0, 0)),
            pl.BlockSpec((SAMPLE_GROUP, HEADS, HEAD_DIM, HEAD_DIM), lambda i: (i, 0, 0, 0)),
            pl.BlockSpec((SAMPLE_GROUP, POOL_BUF * D_MODEL), lambda i: (i, 0)),
        ],
        out_shape=[
            jax.ShapeDtypeStruct((n, D_MODEL), F32),
            jax.ShapeDtypeStruct((n * ROW_TILE, LANES), F32),
            jax.ShapeDtypeStruct((n, ROUTER_LANES), jnp.int32),
            jax.ShapeDtypeStruct((n, ROUTER_LANES), F32),
            jax.ShapeDtypeStruct(state.shape, F32),
            jax.ShapeDtypeStruct(pbuf2.shape, F32),
        ],
        scratch_shapes=[
            pltpu.VMEM((n, D_MODEL), F32),
            pltpu.VMEM((n, 5 * D_MODEL), F32),
            pltpu.VMEM((n, D_MODEL), F32),
            pltpu.VMEM((n, D_MODEL), F32),
        ],
        compiler_params=pltpu.CompilerParams(dimension_semantics=("arbitrary",), vmem_limit_bytes=VMEM_LIMIT),
        name="mixer_sample",
    )(x, gmix, win, wqft, lblt, hgn, wpool, pscale, wout, gffn, wr, br, state, pbuf2)
    x1, h2, ti, tw, snew, pnew = out
    return x1, h2, ti, tw, snew, pnew.reshape(n, POOL_BUF, D_MODEL)


def _moe_kernel(be_ref, off_ref, nv_ref, nu_ref, sa_ref, h2_hbm,
                wg_ref, bg_ref, wu_ref, bu_ref, wd_ref, bd_ref, y_hbm,
                xbuf0, xbuf1, ybuf0, ybuf1, wgb, wub, wdb, xb_ref, gate_ref, act_ref, gsem, ssem):
    rows = MOE_ROWS
    i = pl.program_id(0)
    n_used = nu_ref[0]
    n_blocks = pl.num_programs(0)
    prev = jnp.maximum(i - 1, 0)
    nv_prev = jnp.where(i >= 1, nv_ref[prev], 0)
    xbuf = (xbuf0, xbuf1)
    ybuf = (ybuf0, ybuf1)

    def tile_rows(first, n=1):
        start = first * ROW_TILE
        return pl.ds(start if isinstance(first, int) else pl.multiple_of(start, ROW_TILE), n * ROW_TILE)

    def gather_copy(a, r, s):
        tok = a >> 2
        return pltpu.make_async_copy(h2_hbm.at[tile_rows(tok), :], xbuf[s].at[tile_rows(r), :], gsem.at[s])

    def scatter_copy(a, r, n, s):
        return pltpu.make_async_copy(ybuf[s].at[tile_rows(r, n), :], y_hbm.at[tile_rows(a, n), :], ssem.at[s])

    def wait_gather(s):
        pltpu.make_async_copy(h2_hbm.at[tile_rows(0, rows), :], xbuf[s], gsem.at[s]).wait()

    def wait_scatter(n, s):
        size = rows
        while size >= 1:
            @pl.when((n & size) != 0)
            def _():
                scatter_copy(0, 0, size, s).wait()
            size //= 2

    def scatter_loop(blk, n, s):
        def body(r, c):
            scatter_copy(sa_ref[off_ref[blk] + r], r, 1, s).start()
            return c
        lax.fori_loop(0, n, body, 0)

    n_phases = 3

    def start_copies(phase, cur, full_prev):
        oth = 1 - cur
        off_next = off_ref[jnp.minimum(i + 1, n_blocks - 1)]
        off_prev = off_ref[prev]
        n_gather_phases = n_phases - 1
        if phase < n_gather_phases:
            for r in range(rows * phase // n_gather_phases, rows * (phase + 1) // n_gather_phases):
                gather_copy(sa_ref[off_next + r], r, oth).start(priority=r % 2)
        if full_prev:
            for r in range(rows * phase // n_phases, rows * (phase + 1) // n_phases):
                scatter_copy(sa_ref[off_prev + r], r, 1, oth).start(priority=(r + 1) % 2)

    def phase_load(cur, full_prev):
        wait_gather(cur)

        @pl.when(i >= 2)
        def _():
            wait_scatter(nv_ref[jnp.maximum(i - 2, 0)], cur)

        xb_ref[...] = _load_row_tiles(xbuf[cur], (), rows).astype(BF16)

    def phase_gate(cur, full_prev):
        start_copies(0, cur, full_prev)
        gate_ref[...] = jnp.minimum(_dot(xb_ref[...], wgb[...]) + bg_ref[0], SWIGLU_LIMIT)

    def phase_up(cur, full_prev):
        start_copies(1, cur, full_prev)
        up = jnp.clip(_dot(xb_ref[...], wub[...]) + bu_ref[0], -SWIGLU_LIMIT, SWIGLU_LIMIT)
        gate = gate_ref[...]
        act_ref[...] = ((up + 1.0) * gate * _sigmoid(SWIGLU_ALPHA * gate)).astype(BF16)

    def phase_down(cur, full_prev):
        oth = 1 - cur
        start_copies(2, cur, full_prev)
        _store_row_tiles(ybuf[cur], (), _dot(act_ref[...], wdb[...]) + bd_ref[0])
        if not full_prev:
            scatter_loop(prev, nv_prev, oth)

        @pl.when(i == n_used - 1)
        def _():
            scatter_loop(i, nv_ref[i], cur)
            wait_scatter(nv_prev, oth)
            wait_scatter(nv_ref[i], cur)
            wait_gather(oth)

    @pl.when(i == 0)
    def _():
        def body(r, c):
            gather_copy(sa_ref[off_ref[0] + r], r, 0).start()
            return c
        lax.fori_loop(0, rows, body, 0)

    @pl.when(i < n_used)
    def _():
        first_of_expert = jnp.logical_or(i == 0, be_ref[i] != be_ref[prev])

        @pl.when(first_of_expert)
        def _():
            wgb[...] = wg_ref[0].astype(BF16)
            wub[...] = wu_ref[0].astype(BF16)
            wdb[...] = wd_ref[0].astype(BF16)

        prev_is_full = nv_prev == rows
        for phase in (phase_load, phase_gate, phase_up, phase_down):
            for cur in (0, 1):
                for full_prev in (True, False):
                    full_cond = prev_is_full if full_prev else jnp.logical_not(prev_is_full)

                    @pl.when(jnp.logical_and(i % 2 == cur, full_cond))
                    def _():
                        phase(cur, full_prev)


def _moe(h2, routing, wg, bg, wu, bu, wd, bd):
    rows = MOE_ROWS
    n_tokens = h2.shape[0] // ROW_TILE
    block_e, block_off, block_nv, n_used, sorted_a = routing
    n_blocks = block_e.shape[0]
    w_spec = pl.BlockSpec((1, D_MODEL, D_MODEL), lambda i, be, *_: (be[i], 0, 0))
    b_spec = pl.BlockSpec((1, 1, D_MODEL), lambda i, be, *_: (be[i], 0, 0))
    grid_spec = pltpu.PrefetchScalarGridSpec(
        num_scalar_prefetch=5,
        grid=(n_blocks,),
        in_specs=[pl.BlockSpec(memory_space=pl.ANY), w_spec, b_spec, w_spec, b_spec, w_spec, b_spec],
        out_specs=pl.BlockSpec(memory_space=pl.ANY),
        scratch_shapes=[
            pltpu.VMEM((rows * ROW_TILE, LANES), F32),
            pltpu.VMEM((rows * ROW_TILE, LANES), F32),
            pltpu.VMEM((rows * ROW_TILE, LANES), F32),
            pltpu.VMEM((rows * ROW_TILE, LANES), F32),
            pltpu.VMEM((D_MODEL, D_MODEL), BF16),
            pltpu.VMEM((D_MODEL, D_MODEL), BF16),
            pltpu.VMEM((D_MODEL, D_MODEL), BF16),
            pltpu.VMEM((rows, D_MODEL), BF16),
            pltpu.VMEM((rows, D_MODEL), F32),
            pltpu.VMEM((rows, D_MODEL), BF16),
            pltpu.SemaphoreType.DMA((2,)),
            pltpu.SemaphoreType.DMA((2,)),
        ],
    )
    return pl.pallas_call(
        _moe_kernel,
        grid_spec=grid_spec,
        out_shape=jax.ShapeDtypeStruct((n_tokens * TOP_K * ROW_TILE, LANES), F32),
        compiler_params=pltpu.CompilerParams(dimension_semantics=("arbitrary",), vmem_limit_bytes=VMEM_LIMIT),
        name="moe_experts",
    )(block_e, block_off, block_nv, n_used, sorted_a, h2, wg, bg, wu, bu, wd, bd)


ASSIGN_BITS = 17


def _moe_routing(top_i):
    rows = MOE_ROWS
    n_assign = top_i.shape[0] * TOP_K
    assert n_assign <= 1 << ASSIGN_BITS
    n_blocks = -(-n_assign // rows) + N_EXPERTS
    flat_e = top_i.reshape(-1)
    keys = jnp.sort(flat_e * (1 << ASSIGN_BITS) + jnp.arange(n_assign, dtype=jnp.int32))
    smem_len = -(-(n_assign + rows) // 1024) * 1024
    sorted_a = jnp.pad(keys & ((1 << ASSIGN_BITS) - 1), (0, smem_len - n_assign))
    experts = jnp.arange(N_EXPERTS, dtype=jnp.int32)
    counts = jnp.sum((flat_e[:, None] == experts[None, :]).astype(jnp.int32), axis=0)
    nblk = (counts + rows - 1) // rows
    blk_end = jnp.cumsum(nblk)
    blk_start = blk_end - nblk
    row_start = jnp.cumsum(counts) - counts
    blk = jnp.arange(n_blocks, dtype=jnp.int32)
    owner = (blk[:, None] >= blk_start[None, :]) & (blk[:, None] < blk_end[None, :])
    pick = lambda v: jnp.sum(jnp.where(owner, v[None, :], 0), axis=1)
    within = (blk - pick(blk_start)) * rows
    used = blk < blk_end[-1]
    block_e = jnp.where(used, pick(experts), N_EXPERTS - 1).astype(jnp.int32)
    block_off = jnp.where(used, pick(row_start) + within, 0).astype(jnp.int32)
    block_nv = jnp.where(used, jnp.clip(pick(counts) - within, 0, rows), 0).astype(jnp.int32)
    n_used = blk_end[-1].astype(jnp.int32).reshape(1)
    return block_e, block_off, block_nv, n_used, sorted_a.astype(jnp.int32)


def _final_kernel(x1_ref, y_ref, tw_ref, p_ref, gple_ref, wpg_ref, wpp_ref, gfin_ref, out_ref):
    x = x1_ref[...]
    tw = tw_ref[...]
    rows = x.shape[0]
    for j in range(TOP_K):
        x = x + tw[:, j:j + 1] * _load_row_tiles(y_ref, (), rows, period=TOP_K * ROW_TILE, offset=j * ROW_TILE)
    gate = _sigmoid(_dot(_rms(x, gple_ref[...]).astype(BF16), wpg_ref[...]))
    x = x + gate * _dot(p_ref[...].astype(BF16), wpp_ref[...])
    out_ref[...] = _rms(x, gfin_ref[...])


def _final(x1, y4, tw, p, gple, wpg, wpp, gfin, rows, row_offset):
    n = x1.shape[0]
    off = row_offset // rows
    assert off * rows == row_offset and n % rows == 0
    ple = p.shape[-1]
    return pl.pallas_call(
        _final_kernel,
        grid=(n // rows,),
        in_specs=[
            pl.BlockSpec((rows, D_MODEL), lambda i: (i, 0)),
            pl.BlockSpec((rows * TOP_K * ROW_TILE, LANES), lambda i: (i + off, 0)),
            pl.BlockSpec((rows, ROUTER_LANES), lambda i: (i, 0)),
            pl.BlockSpec((rows, ple), lambda i: (i, 0)),
            _const_spec((1, D_MODEL)),
            _const_spec(wpg.shape),
            _const_spec(wpp.shape),
            _const_spec((1, D_MODEL)),
        ],
        out_specs=pl.BlockSpec((rows, D_MODEL), lambda i: (i, 0)),
        out_shape=jax.ShapeDtypeStruct((n, D_MODEL), F32),
        compiler_params=pltpu.CompilerParams(dimension_semantics=("arbitrary",), vmem_limit_bytes=VMEM_LIMIT),
        name="final",
    )(x1, y4, tw, p, gple, wpg, wpp, gfin)


def kernel(x_prompt, x_sample, p_prompt, p_sample, state_hgrn, state_pool, g_mix, w_in, hg_lb_logits, hg_norm, w_pool,
           pool_scale, w_out, g_ffn, w_router, b_router, w_gate, b_gate, w_up, b_up, w_down, b_down, g_ple,
           w_ple_gate, w_ple_proj, g_final):
    depth = w_in.shape[0]
    assert depth == 1, "single-layer step"
    b, t, _ = x_prompt.shape
    ns = x_sample.shape[0]
    assert x_sample.shape[1] == 1
    n_prompt = b * t
    n_tok = n_prompt + ns

    row = lambda a: a.reshape(1, -1)
    win = w_in[0].astype(BF16)
    wqft = win[:, :2 * D_MODEL].T
    lbl = hg_lb_logits.astype(F32)
    wpool = w_pool[0].astype(BF16)
    wout = w_out[0].astype(BF16)
    wr = jnp.pad(w_router[0], ((0, 0), (0, ROUTER_LANES - N_EXPERTS)))
    br = jnp.pad(b_router[0], (0, ROUTER_LANES - N_EXPERTS)).reshape(1, -1)
    shared = (row(hg_norm[0]), wpool, row(pool_scale[0]), wout, row(g_ffn[0]), wr, br)

    x1_p, h2_p, ti_p, tw_p, s_p, pool_p = _mixer_prompt(x_prompt, row(g_mix[0]), win, lbl, *shared)
    x1_s, h2_s, ti_s, tw_s, s_s, pool_s = _mixer_sample(
        x_sample.reshape(ns, D_MODEL), row(g_mix[0]), win, wqft, lbl.T, *shared,
        state_hgrn[0], state_pool[0], start_pos=PAST_LEN)

    top_i = jnp.concatenate([ti_p[:, :TOP_K], ti_s[:, :TOP_K]], axis=0)
    routing = _moe_routing(top_i)
    e3 = lambda a: a[0].reshape(N_EXPERTS, 1, D_MODEL)
    h2 = jnp.concatenate([h2_p, h2_s], axis=0)
    y4 = _moe(h2, routing, w_gate[0], e3(b_gate), w_up[0], e3(b_up), w_down[0], e3(b_down))

    wpg = w_ple_gate[0].astype(BF16)
    wpp = w_ple_proj[0].astype(BF16)
    fin = (row(g_ple[0]), wpg, wpp, row(g_final))
    y_p = _final(x1_p.reshape(n_prompt, D_MODEL), y4, tw_p, p_prompt[0].reshape(n_prompt, -1), *fin,
                 rows=FINAL_ROWS, row_offset=0)
    y_s = _final(x1_s, y4, tw_s, p_sample[0].reshape(ns, -1), *fin, rows=ns, row_offset=n_prompt)

    return (y_p.reshape(b, t, D_MODEL), y_s.reshape(ns, 1, D_MODEL), s_p[None], pool_p[None], s_s[None],
            pool_s[None])
```

```python
import functools

import jax
import jax.numpy as jnp
from jax import lax
from jax.experimental import pallas as pl
from jax.experimental.pallas import tpu as pltpu

F32 = jnp.float32
BF16 = jnp.bfloat16

D_MODEL = 1024
HEADS = 8
HEAD_DIM = 128
CHUNK = 32
POOL_WINDOWS = (2, 4, 8, 16)
POOL_GW = D_MODEL // len(POOL_WINDOWS)
POOL_BUF = 15
POOL_CARRY = 16
N_EXPERTS = 32
TOP_K = 4
ROUTER_LANES = 128
SWIGLU_LIMIT = 7.0
SWIGLU_ALPHA = 1.702
EPS = 1e-6
PAST_LEN = 16384

SEG_Q, SEG_F, SEG_I, SEG_G, SEG_U, SEG_A, SEG_B = range(7)

MIXER_ROWS = 512
MOE_ROWS = 256
MOE_DEPTH = 3
FINAL_ROWS = 512
VMEM_LIMIT = 56 * 1024 * 1024


def _rms(x, g):
    ms = jnp.mean(x * x, axis=-1, keepdims=True)
    return x * lax.rsqrt(ms + EPS) * g


def _sigmoid(x):
    return 1.0 / (1.0 + jnp.exp(-x))


def _dot(a, b):
    return jnp.dot(a, b, preferred_element_type=F32)


def _dot_nt(a, b):
    return lax.dot_general(a, b, (((1,), (1,)), ((), ())), preferred_element_type=F32)


def _dot_tn(a, b):
    return lax.dot_general(a, b, (((0,), (0,)), ((), ())), preferred_element_type=F32)


LANES = 128
ROW_TILE = D_MODEL // LANES


def _store_row_tiles(ref, lead, x, period=ROW_TILE, offset=0):
    rows = x.shape[0]
    for c in range(ROW_TILE):
        ref[lead + (pl.ds(offset + c, rows, stride=period), slice(None))] = x[:, c * LANES:(c + 1) * LANES]


def _load_row_tiles(ref, lead, rows, period=ROW_TILE, offset=0):
    return jnp.concatenate(
        [ref[lead + (pl.ds(offset + c, rows, stride=period), slice(None))] for c in range(ROW_TILE)], axis=-1)


def _split_bf16(x):
    hi = x.astype(BF16)
    lo = (x - hi.astype(F32)).astype(BF16)
    return hi, lo


def _forget_lower_bound(lbl):
    m = jnp.max(lbl, axis=0, keepdims=True)
    e = jnp.exp(lbl - m)
    return e[0:1] / jnp.sum(e, axis=0, keepdims=True)


def _head_norm_gate(o, g_raw, hgn):
    parts = []
    for h in range(HEADS):
        oh = o[:, h * HEAD_DIM:(h + 1) * HEAD_DIM]
        parts.append(_rms(oh, hgn))
    return jnp.concatenate(parts, axis=-1) * (g_raw * _sigmoid(g_raw))


def _route(h2, wr_ref, br_ref, ti_ref, tw_ref):
    rows = h2.shape[0]
    h_hi, h_lo = _split_bf16(h2)
    w_hi, w_lo = _split_bf16(wr_ref[...])
    logits = _dot(h_hi, w_hi) + _dot(h_lo, w_hi) + _dot(h_hi, w_lo) + br_ref[...]
    lane = lax.broadcasted_iota(jnp.int32, (rows, ROUTER_LANES), 1)
    neg = jnp.float32(-jnp.inf)
    l = jnp.where(lane < N_EXPERTS, logits, neg)
    ti = jnp.zeros((rows, ROUTER_LANES), jnp.int32)
    tw = jnp.zeros((rows, ROUTER_LANES), F32)
    m0 = None
    denom = None
    es = []
    for j in range(TOP_K):
        m = jnp.max(l, axis=-1, keepdims=True)
        idx = jnp.min(jnp.where(l == m, lane, ROUTER_LANES), axis=-1, keepdims=True)
        l = jnp.where(lane == idx, neg, l)
        if j == 0:
            m0 = m
        e = jnp.exp(m - m0)
        es.append(e)
        denom = e if denom is None else denom + e
        ti = jnp.where(lane == j, idx, ti)
    for j in range(TOP_K):
        tw = jnp.where(lane == j, es[j] / denom, tw)
    ti_ref[...] = ti
    tw_ref[...] = tw


def _mixer_prompt_kernel(x_ref, gmix_ref, win_ref, lbl_ref, hgn_ref, wpool_ref, pscale_ref, wout_ref, gffn_ref,
                         wr_ref, br_ref,
                         x1_ref, h2_ref, ti_ref, tw_ref, sfin_ref, ptail_ref,
                         st_ref, uext_ref, qe_ref, ke_ref, kd_ref, v_ref, o_ref, p_ref):
    rows = MIXER_ROWS
    n_chunks = rows // CHUNK
    t = pl.program_id(1)
    last_t = pl.num_programs(1) - 1

    @pl.when(t == 0)
    def _():
        st_ref[...] = jnp.zeros_like(st_ref)
        uext_ref[0:POOL_CARRY, :] = jnp.zeros((POOL_CARRY, D_MODEL), F32)

    @pl.when(t > 0)
    def _():
        uext_ref[0:POOL_CARRY, :] = uext_ref[rows:rows + POOL_CARRY, :]

    x = x_ref[0]
    h = _rms(x, gmix_ref[...]).astype(BF16)

    def proj(seg):
        return _dot(h, win_ref[:, seg * D_MODEL:(seg + 1) * D_MODEL])

    lb = _forget_lower_bound(lbl_ref[...])
    q_raw = proj(SEG_Q)
    q = q_raw * _sigmoid(q_raw)
    f = lb + (1.0 - lb) * _sigmoid(proj(SEG_F))
    k = 1.0 - f
    row_in_chunk = lax.broadcasted_iota(jnp.int32, (rows, D_MODEL), 0) % CHUNK
    p = f
    s = 1
    while s < CHUNK:
        p = p * jnp.where(row_in_chunk >= s, pltpu.roll(p, s, axis=0), 1.0)
        s *= 2
    p3 = p.reshape(n_chunks, CHUNK, D_MODEL)
    plast = p3[:, CHUNK - 1:CHUNK, :]
    qe_ref[...] = (q * p).astype(BF16)
    ke_ref[...] = (k / p).astype(BF16)
    kd_ref[...] = (k * (plast / p3).reshape(rows, D_MODEL)).astype(BF16)
    p_ref[...] = p
    v_ref[...] = proj(SEG_I).astype(BF16)

    causal = (lax.broadcasted_iota(jnp.int32, (CHUNK, CHUNK), 0)
              >= lax.broadcasted_iota(jnp.int32, (CHUNK, CHUNK), 1))

    for hd in range(HEADS):
        cols = slice(hd * HEAD_DIM, (hd + 1) * HEAD_DIM)
        st = st_ref[hd]
        for c in range(n_chunks):
            rs = slice(c * CHUNK, (c + 1) * CHUNK)
            qe = qe_ref[rs, cols]
            ke = ke_ref[rs, cols]
            kd = kd_ref[rs, cols]
            vv = v_ref[rs, cols]
            decay = p_ref[(c + 1) * CHUNK - 1:(c + 1) * CHUNK, cols]
            scores = jnp.where(causal, _dot_nt(qe, ke), 0.0)
            o_ref[rs, cols] = _dot_nt(qe, st.astype(BF16)) + _dot(scores.astype(BF16), vv)
            st = st * decay + _dot_tn(vv, kd)
        st_ref[hd] = st

    @pl.when(t == last_t)
    def _():
        for hd in range(HEADS):
            sfin_ref[0, hd] = st_ref[hd].T

    g_raw = proj(SEG_G)
    o = _head_norm_gate(o_ref[...], g_raw, hgn_ref[...])

    u = proj(SEG_U)
    uext_ref[POOL_CARRY:POOL_CARRY + rows, :] = u
    pos1 = t * rows + lax.broadcasted_iota(jnp.int32, (rows, 1), 0) + 1
    pooled = []
    for g, w in enumerate(POOL_WINDOWS):
        cols = slice(g * POOL_GW, (g + 1) * POOL_GW)
        sw = u[:, cols]
        for j in range(1, w):
            sw = sw + uext_ref[POOL_CARRY - j:POOL_CARRY - j + rows, cols]
        inv_cnt = 1.0 / jnp.minimum(pos1, w).astype(F32)
        dg = sw * inv_cnt - u[:, cols]
        pooled.append(_dot(dg.astype(BF16), wpool_ref[g]))
    y_pool = jnp.concatenate(pooled, axis=-1) * pscale_ref[...]

    @pl.when(t == last_t)
    def _():
        ptail_ref[0] = uext_ref[rows + POOL_CARRY - POOL_BUF:rows + POOL_CARRY, :]

    merged = _sigmoid(proj(SEG_A)) * o + _sigmoid(proj(SEG_B)) * y_pool
    x1 = x + _dot(merged.astype(BF16), wout_ref[...])
    x1_ref[0] = x1
    h2 = _rms(x1, gffn_ref[...])
    _store_row_tiles(h2_ref, (), h2)
    _route(h2, wr_ref, br_ref, ti_ref, tw_ref)


def _const_spec(shape):
    zeros = (0,) * len(shape)
    return pl.BlockSpec(shape, lambda *_: zeros, pipeline_mode=pl.Buffered(1))


def _mixer_prompt(x, gmix, win, lbl, hgn, wpool, pscale, wout, gffn, wr, br):
    b, t, _ = x.shape
    rows = MIXER_ROWS
    nt = t // rows
    tok_spec = pl.BlockSpec((1, rows, D_MODEL), lambda i, j: (i, j, 0))
    flat_spec = pl.BlockSpec((rows * ROW_TILE, LANES), lambda i, j: (i * nt + j, 0))
    lane_spec = pl.BlockSpec((rows, ROUTER_LANES), lambda i, j: (i * nt + j, 0))
    return pl.pallas_call(
        _mixer_prompt_kernel,
        grid=(b, nt),
        in_specs=[
            tok_spec,
            _const_spec((1, D_MODEL)),
            _const_spec(win.shape),
            _const_spec(lbl.shape),
            _const_spec((1, HEAD_DIM)),
            _const_spec(wpool.shape),
            _const_spec((1, D_MODEL)),
            _const_spec(wout.shape),
            _const_spec((1, D_MODEL)),
            _const_spec(wr.shape),
            _const_spec(br.shape),
        ],
        out_specs=[
            tok_spec,
            flat_spec,
            lane_spec,
            lane_spec,
            pl.BlockSpec((1, HEADS, HEAD_DIM, HEAD_DIM), lambda i, j: (i, 0, 0, 0)),
            pl.BlockSpec((1, POOL_BUF, D_MODEL), lambda i, j: (i, 0, 0)),
        ],
        out_shape=[
            jax.ShapeDtypeStruct(x.shape, F32),
            jax.ShapeDtypeStruct((b * t * ROW_TILE, LANES), F32),
            jax.ShapeDtypeStruct((b * t, ROUTER_LANES), jnp.int32),
            jax.ShapeDtypeStruct((b * t, ROUTER_LANES), F32),
            jax.ShapeDtypeStruct((b, HEADS, HEAD_DIM, HEAD_DIM), F32),
            jax.ShapeDtypeStruct((b, POOL_BUF, D_MODEL), F32),
        ],
        scratch_shapes=[
            pltpu.VMEM((HEADS, HEAD_DIM, HEAD_DIM), F32),
            pltpu.VMEM((rows + POOL_CARRY, D_MODEL), F32),
            pltpu.VMEM((rows, D_MODEL), BF16),
            pltpu.VMEM((rows, D_MODEL), BF16),
            pltpu.VMEM((rows, D_MODEL), BF16),
            pltpu.VMEM((rows, D_MODEL), BF16),
            pltpu.VMEM((rows, D_MODEL), F32),
            pltpu.VMEM((rows, D_MODEL), F32),
        ],
        compiler_params=pltpu.CompilerParams(dimension_semantics=("arbitrary", "arbitrary"),
                                             vmem_limit_bytes=VMEM_LIMIT),
        name="mixer_prompt",
    )(x, gmix, win, lbl, hgn, wpool, pscale, wout, gffn, wr, br)


SAMPLE_GROUP = 8


def _mixer_sample_kernel(pool_cnt, x_ref, gmix_ref, win_ref, wqft_ref, lblt_ref, hgn_ref, wpool_ref, pscale_ref,
                         wout_ref, gffn_ref, wr_ref, br_ref, s_ref, pbuf_ref,
                         x1_ref, h2_ref, ti_ref, tw_ref, snew_ref, pnew_ref,
                         h_ref, znat_ref, o_ref, bsum_ref):
    step = pl.program_id(0)
    last = pl.num_programs(0) - 1
    r0 = pl.multiple_of(step * SAMPLE_GROUP, SAMPLE_GROUP)

    @pl.when(step == 0)
    def _():
        hf = _rms(x_ref[...], gmix_ref[...])
        h_ref[...] = hf
        h = hf.astype(BF16)
        for i, seg in enumerate((SEG_I, SEG_G, SEG_U, SEG_A, SEG_B)):
            znat_ref[:, i * D_MODEL:(i + 1) * D_MODEL] = _dot(h, win_ref[:, seg * D_MODEL:(seg + 1) * D_MODEL])

    hg = h_ref[pl.ds(r0, SAMPLE_GROUP), :].astype(BF16)
    qft = _dot_nt(wqft_ref[...], hg)
    lbl = lblt_ref[...]
    m = jnp.max(lbl, axis=1, keepdims=True)
    e = jnp.exp(lbl - m)
    lb = e[:, 0:1] / jnp.sum(e, axis=1, keepdims=True)
    q_raw = qft[0:D_MODEL]
    qt = q_raw * _sigmoid(q_raw)
    ft = lb + (1.0 - lb) * _sigmoid(qft[D_MODEL:2 * D_MODEL])
    kt = 1.0 - ft

    for j in range(SAMPLE_GROUP):
        v_row = znat_ref[pl.ds(r0 + j, 1), 0:D_MODEL]
        o_parts = []
        for hd in range(HEADS):
            rs = slice(hd * HEAD_DIM, (hd + 1) * HEAD_DIM)
            s_new = ft[rs, j:j + 1] * s_ref[j, hd] + kt[rs, j:j + 1] * v_row[:, rs]
            snew_ref[j, hd] = s_new
            o_parts.append(jnp.sum(qt[rs, j:j + 1] * s_new, axis=0, keepdims=True))
        o_ref[pl.ds(r0 + j, 1), :] = jnp.concatenate(o_parts, axis=-1)

    u_g = znat_ref[pl.ds(r0, SAMPLE_GROUP), 2 * D_MODEL:3 * D_MODEL]
    sums = []
    for g, w in enumerate(POOL_WINDOWS):
        acc = jnp.zeros((SAMPLE_GROUP, POOL_GW), F32)
        for j in range(1, w):
            row = POOL_BUF - j
            acc = acc + pbuf_ref[:, row * D_MODEL + g * POOL_GW:row * D_MODEL + (g + 1) * POOL_GW]
        sums.append(acc)
    bsum_ref[pl.ds(r0, SAMPLE_GROUP), :] = jnp.concatenate(sums, axis=-1)
    pnew_ref[:, 0:(POOL_BUF - 1) * D_MODEL] = pbuf_ref[:, D_MODEL:POOL_BUF * D_MODEL]
    pnew_ref[:, (POOL_BUF - 1) * D_MODEL:POOL_BUF * D_MODEL] = u_g

    @pl.when(step == last)
    def _():
        x = x_ref[...]
        g_raw = znat_ref[:, D_MODEL:2 * D_MODEL]
        u = znat_ref[:, 2 * D_MODEL:3 * D_MODEL]
        o = _head_norm_gate(o_ref[...], g_raw, hgn_ref[...])
        sw = bsum_ref[...] + u
        pooled = []
        for g, w in enumerate(POOL_WINDOWS):
            cols = slice(g * POOL_GW, (g + 1) * POOL_GW)
            dg = sw[:, cols] * (1.0 / pool_cnt[g]) - u[:, cols]
            pooled.append(_dot(dg.astype(BF16), wpool_ref[g]))
        y_pool = jnp.concatenate(pooled, axis=-1) * pscale_ref[...]
        merged = (_sigmoid(znat_ref[:, 3 * D_MODEL:4 * D_MODEL]) * o
                  + _sigmoid(znat_ref[:, 4 * D_MODEL:5 * D_MODEL]) * y_pool)
        x1 = x + _dot(merged.astype(BF16), wout_ref[...])
        x1_ref[...] = x1
        h2 = _rms(x1, gffn_ref[...])
        _store_row_tiles(h2_ref, (), h2)
        _route(h2, wr_ref, br_ref, ti_ref, tw_ref)


def _mixer_sample(x, gmix, win, wqft, lblt, hgn, wpool, pscale, wout, gffn, wr, br, state, pbuf, start_pos):
    n = x.shape[0]
    steps = n // SAMPLE_GROUP
    pool_cnt = tuple(float(min(start_pos + 1, w)) for w in POOL_WINDOWS)
    pbuf2 = pbuf.reshape(n, POOL_BUF * D_MODEL)
    full = _const_spec
    out = pl.pallas_call(
        functools.partial(_mixer_sample_kernel, pool_cnt),
        grid=(steps,),
        in_specs=[
            full((n, D_MODEL)),
            full((1, D_MODEL)),
            full(win.shape),
            full(wqft.shape),
            full(lblt.shape),
            full((1, HEAD_DIM)),
            full(wpool.shape),
            full((1, D_MODEL)),
            full(wout.shape),
            full((1, D_MODEL)),
            full(wr.shape),
            full(br.shape),
            pl.BlockSpec((SAMPLE_GROUP, HEADS, HEAD_DIM, HEAD_DIM), lambda i: (i, 0, 0, 0)),
            pl.BlockSpec((SAMPLE_GROUP, POOL_BUF * D_MODEL), lambda i: (i, 0)),
        ],
        out_specs=[
            pl.BlockSpec((n, D_MODEL), lambda i: (0, 0)),
            pl.BlockSpec((n * ROW_TILE, LANES), lambda i: (0, 0)),
            pl.BlockSpec((n, ROUTER_LANES), lambda i: (0, 0)),
            pl.BlockSpec((n, ROUTER_LANES), lambda i: (0, 0)),
            pl.BlockSpec((SAMPLE_GROUP, HEADS, HEAD_DIM, HEAD_DIM), lambda i: (i, 0, 0, 0)),
            pl.BlockSpec((SAMPLE_GROUP, POOL_BUF * D_MODEL), lambda i: (i, 0)),
        ],
        out_shape=[
            jax.ShapeDtypeStruct((n, D_MODEL), F32),
            jax.ShapeDtypeStruct((n * ROW_TILE, LANES), F32),
            jax.ShapeDtypeStruct((n, ROUTER_LANES), jnp.int32),
            jax.ShapeDtypeStruct((n, ROUTER_LANES), F32),
            jax.ShapeDtypeStruct(state.shape, F32),
            jax.ShapeDtypeStruct(pbuf2.shape, F32),
        ],
        scratch_shapes=[
            pltpu.VMEM((n, D_MODEL), F32),
            pltpu.VMEM((n, 5 * D_MODEL), F32),
            pltpu.VMEM((n, D_MODEL), F32),
            pltpu.VMEM((n, D_MODEL), F32),
        ],
        compiler_params=pltpu.CompilerParams(dimension_semantics=("arbitrary",), vmem_limit_bytes=VMEM_LIMIT),
        name="mixer_sample",
    )(x, gmix, win, wqft, lblt, hgn, wpool, pscale, wout, gffn, wr, br, state, pbuf2)
    x1, h2, ti, tw, snew, pnew = out
    return x1, h2, ti, tw, snew, pnew.reshape(n, POOL_BUF, D_MODEL)


def _moe_kernel(be_ref, off_ref, nv_ref, nu_ref, sa_ref, h2_hbm,
                wg_ref, bg_ref, wu_ref, bu_ref, wd_ref, bd_ref, y_hbm,
                xbuf0, xbuf1, xbuf2, ybuf0, ybuf1, ybuf2, wgb, wub, wdb, xb_ref, gate_ref, act_ref, gsem, ssem):
    rows = MOE_ROWS
    i = pl.program_id(0)
    n_used = nu_ref[0]
    n_blocks = pl.num_programs(0)
    nv_back = lambda k: jnp.where(i >= k, nv_ref[jnp.maximum(i - k, 0)], 0)
    prev = jnp.maximum(i - 1, 0)
    nv_prev = nv_back(1)
    xbuf = (xbuf0, xbuf1, xbuf2)
    ybuf = (ybuf0, ybuf1, ybuf2)

    def tile_rows(first, n=1):
        start = first * ROW_TILE
        return pl.ds(start if isinstance(first, int) else pl.multiple_of(start, ROW_TILE), n * ROW_TILE)

    def gather_copy(a, r, s):
        tok = a >> 2
        return pltpu.make_async_copy(h2_hbm.at[tile_rows(tok), :], xbuf[s].at[tile_rows(r), :], gsem.at[s])

    def scatter_copy(a, r, n, s):
        return pltpu.make_async_copy(ybuf[s].at[tile_rows(r, n), :], y_hbm.at[tile_rows(a, n), :], ssem.at[s])

    def wait_gather(s):
        pltpu.make_async_copy(h2_hbm.at[tile_rows(0, rows), :], xbuf[s], gsem.at[s]).wait()

    def wait_scatter(n, s):
        size = rows
        while size >= 1:
            @pl.when((n & size) != 0)
            def _():
                scatter_copy(0, 0, size, s).wait()
            size //= 2

    def scatter_loop(blk, n, s):
        def body(r, c):
            scatter_copy(sa_ref[off_ref[blk] + r], r, 1, s).start()
            return c
        lax.fori_loop(0, n, body, 0)

    n_phases = 3

    def start_copies(phase, cur, full_prev):
        ahead, behind = (cur + 2) % MOE_DEPTH, (cur + MOE_DEPTH - 1) % MOE_DEPTH
        off_ahead = off_ref[jnp.minimum(i + 2, n_blocks - 1)]
        off_prev = off_ref[prev]
        for r in range(rows * phase // n_phases, rows * (phase + 1) // n_phases):
            gather_copy(sa_ref[off_ahead + r], r, ahead).start(priority=r % 2)
            if full_prev:
                scatter_copy(sa_ref[off_prev + r], r, 1, behind).start(priority=(r + 1) % 2)

    def phase_load(cur, full_prev):
        wait_gather(cur)
        xb_ref[...] = _load_row_tiles(xbuf[cur], (), rows).astype(BF16)

    def phase_gate(cur, full_prev):
        start_copies(0, cur, full_prev)
        gate_ref[...] = jnp.minimum(_dot(xb_ref[...], wgb[...]) + bg_ref[0], SWIGLU_LIMIT)

    def phase_up(cur, full_prev):
        start_copies(1, cur, full_prev)
        up = jnp.clip(_dot(xb_ref[...], wub[...]) + bu_ref[0], -SWIGLU_LIMIT, SWIGLU_LIMIT)
        gate = gate_ref[...]
        act_ref[...] = ((up + 1.0) * gate * _sigmoid(SWIGLU_ALPHA * gate)).astype(BF16)

    def phase_down(cur, full_prev):
        nxt, behind = (cur + 1) % MOE_DEPTH, (cur + MOE_DEPTH - 1) % MOE_DEPTH
        start_copies(2, cur, full_prev)
        wait_scatter(nv_back(MOE_DEPTH), cur)
        _store_row_tiles(ybuf[cur], (), _dot(act_ref[...], wdb[...]) + bd_ref[0])
        if not full_prev:
            scatter_loop(prev, nv_prev, behind)

        @pl.when(i == n_used - 1)
        def _():
            scatter_loop(i, nv_ref[i], cur)
            wait_scatter(nv_back(2), nxt)
            wait_scatter(nv_prev, behind)
            wait_scatter(nv_ref[i], cur)
            wait_gather(nxt)
            wait_gather((cur + 2) % MOE_DEPTH)

    @pl.when(i == 0)
    def _():
        for blk in range(MOE_DEPTH - 1):
            def body(r, c):
                gather_copy(sa_ref[off_ref[blk] + r], r, blk).start()
                return c
            lax.fori_loop(0, rows, body, 0)

    @pl.when(i < n_used)
    def _():
        first_of_expert = jnp.logical_or(i == 0, be_ref[i] != be_ref[prev])

        @pl.when(first_of_expert)
        def _():
            wgb[...] = wg_ref[0].astype(BF16)
            wub[...] = wu_ref[0].astype(BF16)
            wdb[...] = wd_ref[0].astype(BF16)

        prev_is_full = nv_prev == rows
        for phase in (phase_load, phase_gate, phase_up, phase_down):
            for cur in range(MOE_DEPTH):
                for full_prev in (True, False):
                    full_cond = prev_is_full if full_prev else jnp.logical_not(prev_is_full)

                    @pl.when(jnp.logical_and(i % MOE_DEPTH == cur, full_cond))
                    def _():
                        phase(cur, full_prev)


def _moe(h2, routing, wg, bg, wu, bu, wd, bd):
    rows = MOE_ROWS
    n_tokens = h2.shape[0] // ROW_TILE
    block_e, block_off, block_nv, n_used, sorted_a = routing
    n_blocks = block_e.shape[0]
    w_spec = pl.BlockSpec((1, D_MODEL, D_MODEL), lambda i, be, *_: (be[i], 0, 0))
    b_spec = pl.BlockSpec((1, 1, D_MODEL), lambda i, be, *_: (be[i], 0, 0))
    grid_spec = pltpu.PrefetchScalarGridSpec(
        num_scalar_prefetch=5,
        grid=(n_blocks,),
        in_specs=[pl.BlockSpec(memory_space=pl.ANY), w_spec, b_spec, w_spec, b_spec, w_spec, b_spec],
        out_specs=pl.BlockSpec(memory_space=pl.ANY),
        scratch_shapes=[
            *[pltpu.VMEM((rows * ROW_TILE, LANES), F32)] * (2 * MOE_DEPTH),
            pltpu.VMEM((D_MODEL, D_MODEL), BF16),
            pltpu.VMEM((D_MODEL, D_MODEL), BF16),
            pltpu.VMEM((D_MODEL, D_MODEL), BF16),
            pltpu.VMEM((rows, D_MODEL), BF16),
            pltpu.VMEM((rows, D_MODEL), F32),
            pltpu.VMEM((rows, D_MODEL), BF16),
            pltpu.SemaphoreType.DMA((MOE_DEPTH,)),
            pltpu.SemaphoreType.DMA((MOE_DEPTH,)),
        ],
    )
    return pl.pallas_call(
        _moe_kernel,
        grid_spec=grid_spec,
        out_shape=jax.ShapeDtypeStruct((n_tokens * TOP_K * ROW_TILE, LANES), F32),
        compiler_params=pltpu.CompilerParams(dimension_semantics=("arbitrary",), vmem_limit_bytes=VMEM_LIMIT),
        name="moe_experts",
    )(block_e, block_off, block_nv, n_used, sorted_a, h2, wg, bg, wu, bu, wd, bd)


ASSIGN_BITS = 17


def _moe_routing(top_i):
    rows = MOE_ROWS
    n_assign = top_i.shape[0] * TOP_K
    assert n_assign <= 1 << ASSIGN_BITS
    n_blocks = -(-n_assign // rows) + N_EXPERTS
    flat_e = top_i.reshape(-1)
    keys = jnp.sort(flat_e * (1 << ASSIGN_BITS) + jnp.arange(n_assign, dtype=jnp.int32))
    smem_len = -(-(n_assign + rows) // 1024) * 1024
    sorted_a = jnp.pad(keys & ((1 << ASSIGN_BITS) - 1), (0, smem_len - n_assign))
    experts = jnp.arange(N_EXPERTS, dtype=jnp.int32)
    counts = jnp.sum((flat_e[:, None] == experts[None, :]).astype(jnp.int32), axis=0)
    nblk = (counts + rows - 1) // rows
    blk_end = jnp.cumsum(nblk)
    blk_start = blk_end - nblk
    row_start = jnp.cumsum(counts) - counts
    blk = jnp.arange(n_blocks, dtype=jnp.int32)
    owner = (blk[:, None] >= blk_start[None, :]) & (blk[:, None] < blk_end[None, :])
    pick = lambda v: jnp.sum(jnp.where(owner, v[None, :], 0), axis=1)
    within = (blk - pick(blk_start)) * rows
    used = blk < blk_end[-1]
    block_e = jnp.where(used, pick(experts), N_EXPERTS - 1).astype(jnp.int32)
    block_off = jnp.where(used, pick(row_start) + within, 0).astype(jnp.int32)
    block_nv = jnp.where(used, jnp.clip(pick(counts) - within, 0, rows), 0).astype(jnp.int32)
    n_used = blk_end[-1].astype(jnp.int32).reshape(1)
    return block_e, block_off, block_nv, n_used, sorted_a.astype(jnp.int32)


def _final_kernel(x1_ref, y_ref, tw_ref, p_ref, gple_ref, wpg_ref, wpp_ref, gfin_ref, out_ref):
    x = x1_ref[...]
    tw = tw_ref[...]
    rows = x.shape[0]
    for j in range(TOP_K):
        x = x + tw[:, j:j + 1] * _load_row_tiles(y_ref, (), rows, period=TOP_K * ROW_TILE, offset=j * ROW_TILE)
    gate = _sigmoid(_dot(_rms(x, gple_ref[...]).astype(BF16), wpg_ref[...]))
    x = x + gate * _dot(p_ref[...].astype(BF16), wpp_ref[...])
    out_ref[...] = _rms(x, gfin_ref[...])


def _final(x1, y4, tw, p, gple, wpg, wpp, gfin, rows, row_offset):
    n = x1.shape[0]
    off = row_offset // rows
    assert off * rows == row_offset and n % rows == 0
    ple = p.shape[-1]
    return pl.pallas_call(
        _final_kernel,
        grid=(n // rows,),
        in_specs=[
            pl.BlockSpec((rows, D_MODEL), lambda i: (i, 0)),
            pl.BlockSpec((rows * TOP_K * ROW_TILE, LANES), lambda i: (i + off, 0)),
            pl.BlockSpec((rows, ROUTER_LANES), lambda i: (i, 0)),
            pl.BlockSpec((rows, ple), lambda i: (i, 0)),
            _const_spec((1, D_MODEL)),
            _const_spec(wpg.shape),
            _const_spec(wpp.shape),
            _const_spec((1, D_MODEL)),
        ],
        out_specs=pl.BlockSpec((rows, D_MODEL), lambda i: (i, 0)),
        out_shape=jax.ShapeDtypeStruct((n, D_MODEL), F32),
        compiler_params=pltpu.CompilerParams(dimension_semantics=("arbitrary",), vmem_limit_bytes=VMEM_LIMIT),
        name="final",
    )(x1, y4, tw, p, gple, wpg, wpp, gfin)


def kernel(x_prompt, x_sample, p_prompt, p_sample, state_hgrn, state_pool, g_mix, w_in, hg_lb_logits, hg_norm, w_pool,
           pool_scale, w_out, g_ffn, w_router, b_router, w_gate, b_gate, w_up, b_up, w_down, b_down, g_ple,
           w_ple_gate, w_ple_proj, g_final):
    depth = w_in.shape[0]
    assert depth == 1, "single-layer step"
    b, t, _ = x_prompt.shape
    ns = x_sample.shape[0]
    assert x_sample.shape[1] == 1
    n_prompt = b * t
    n_tok = n_prompt + ns

    row = lambda a: a.reshape(1, -1)
    win = w_in[0].astype(BF16)
    wqft = win[:, :2 * D_MODEL].T
    lbl = hg_lb_logits.astype(F32)
    wpool = w_pool[0].astype(BF16)
    wout = w_out[0].astype(BF16)
    wr = jnp.pad(w_router[0], ((0, 0), (0, ROUTER_LANES - N_EXPERTS)))
    br = jnp.pad(b_router[0], (0, ROUTER_LANES - N_EXPERTS)).reshape(1, -1)
    shared = (row(hg_norm[0]), wpool, row(pool_scale[0]), wout, row(g_ffn[0]), wr, br)

    x1_p, h2_p, ti_p, tw_p, s_p, pool_p = _mixer_prompt(x_prompt, row(g_mix[0]), win, lbl, *shared)
    x1_s, h2_s, ti_s, tw_s, s_s, pool_s = _mixer_sample(
        x_sample.reshape(ns, D_MODEL), row(g_mix[0]), win, wqft, lbl.T, *shared,
        state_hgrn[0], state_pool[0], start_pos=PAST_LEN)

    top_i = jnp.concatenate([ti_p[:, :TOP_K], ti_s[:, :TOP_K]], axis=0)
    routing = _moe_routing(top_i)
    e3 = lambda a: a[0].reshape(N_EXPERTS, 1, D_MODEL)
    h2 = jnp.concatenate([h2_p, h2_s], axis=0)
    y4 = _moe(h2, routing, w_gate[0], e3(b_gate), w_up[0], e3(b_up), w_down[0], e3(b_down))

    wpg = w_ple_gate[0].astype(BF16)
    wpp = w_ple_proj[0].astype(BF16)
    fin = (row(g_ple[0]), wpg, wpp, row(g_final))
    y_p = _final(x1_p.reshape(n_prompt, D_MODEL), y4, tw_p, p_prompt[0].reshape(n_prompt, -1), *fin,
                 rows=FINAL_ROWS, row_offset=0)
    y_s = _final(x1_s, y4, tw_s, p_sample[0].reshape(ns, -1), *fin, rows=ns, row_offset=n_prompt)

    return (y_p.reshape(b, t, D_MODEL), y_s.reshape(ns, 1, D_MODEL), s_p[None], pool_p[None], s_s[None],
            pool_s[None])
```

```python
import functools

import jax
import jax.numpy as jnp
from jax import lax
from jax.experimental import pallas as pl
from jax.experimental.pallas import tpu as pltpu

F32 = jnp.float32
BF16 = jnp.bfloat16

D_MODEL = 1024
HEADS = 8
HEAD_DIM = 128
CHUNK = 32
POOL_WINDOWS = (2, 4, 8, 16)
POOL_GW = D_MODEL // len(POOL_WINDOWS)
POOL_BUF = 15
POOL_CARRY = 16
N_EXPERTS = 32
TOP_K = 4
ROUTER_LANES = 128
SWIGLU_LIMIT = 7.0
SWIGLU_ALPHA = 1.702
EPS = 1e-6
PAST_LEN = 16384

SEG_Q, SEG_F, SEG_I, SEG_G, SEG_U, SEG_A, SEG_B = range(7)

MIXER_ROWS = 512
MOE_ROWS = 256
MOE_DEPTH = 3
FINAL_ROWS = 512
VMEM_LIMIT = 56 * 1024 * 1024


def _rms(x, g):
    ms = jnp.mean(x * x, axis=-1, keepdims=True)
    return x * lax.rsqrt(ms + EPS) * g


def _sigmoid(x):
    return 1.0 / (1.0 + jnp.exp(-x))


def _dot(a, b):
    return jnp.dot(a, b, preferred_element_type=F32)


def _dot_nt(a, b):
    return lax.dot_general(a, b, (((1,), (1,)), ((), ())), preferred_element_type=F32)


def _dot_tn(a, b):
    return lax.dot_general(a, b, (((0,), (0,)), ((), ())), preferred_element_type=F32)


LANES = 128
ROW_TILE = D_MODEL // LANES


def _store_row_tiles(ref, lead, x, period=ROW_TILE, offset=0):
    rows = x.shape[0]
    for c in range(ROW_TILE):
        ref[lead + (pl.ds(offset + c, rows, stride=period), slice(None))] = x[:, c * LANES:(c + 1) * LANES]


def _load_row_tiles(ref, lead, rows, period=ROW_TILE, offset=0):
    return jnp.concatenate(
        [ref[lead + (pl.ds(offset + c, rows, stride=period), slice(None))] for c in range(ROW_TILE)], axis=-1)


def _split_bf16(x):
    hi = x.astype(BF16)
    lo = (x - hi.astype(F32)).astype(BF16)
    return hi, lo


def _forget_lower_bound(lbl):
    m = jnp.max(lbl, axis=0, keepdims=True)
    e = jnp.exp(lbl - m)
    return e[0:1] / jnp.sum(e, axis=0, keepdims=True)


def _head_norm_gate(o, g_raw, hgn):
    parts = []
    for h in range(HEADS):
        oh = o[:, h * HEAD_DIM:(h + 1) * HEAD_DIM]
        parts.append(_rms(oh, hgn))
    return jnp.concatenate(parts, axis=-1) * (g_raw * _sigmoid(g_raw))


def _route(h2, wr_ref, br_ref, ti_ref, tw_ref):
    rows = h2.shape[0]
    h_hi, h_lo = _split_bf16(h2)
    w_hi, w_lo = _split_bf16(wr_ref[...])
    logits = _dot(h_hi, w_hi) + _dot(h_lo, w_hi) + _dot(h_hi, w_lo) + br_ref[...]
    lane = lax.broadcasted_iota(jnp.int32, (rows, ROUTER_LANES), 1)
    neg = jnp.float32(-jnp.inf)
    l = jnp.where(lane < N_EXPERTS, logits, neg)
    ti = jnp.zeros((rows, ROUTER_LANES), jnp.int32)
    tw = jnp.zeros((rows, ROUTER_LANES), F32)
    m0 = None
    denom = None
    es = []
    for j in range(TOP_K):
        m = jnp.max(l, axis=-1, keepdims=True)
        idx = jnp.min(jnp.where(l == m, lane, ROUTER_LANES), axis=-1, keepdims=True)
        l = jnp.where(lane == idx, neg, l)
        if j == 0:
            m0 = m
        e = jnp.exp(m - m0)
        es.append(e)
        denom = e if denom is None else denom + e
        ti = jnp.where(lane == j, idx, ti)
    for j in range(TOP_K):
        tw = jnp.where(lane == j, es[j] / denom, tw)
    ti_ref[...] = ti
    tw_ref[...] = tw


def _mixer_prompt_kernel(x_ref, gmix_ref, win_ref, lbl_ref, hgn_ref, wpool_ref, pscale_ref, wout_ref, gffn_ref,
                         wr_ref, br_ref,
                         x1_ref, h2_ref, ti_ref, tw_ref, sfin_ref, ptail_ref,
                         st_ref, uext_ref, qe_ref, ke_ref, kd_ref, v_ref, o_ref, p_ref, h_ref):
    rows = MIXER_ROWS
    n_chunks = rows // CHUNK
    t = pl.program_id(1)
    last_t = pl.num_programs(1) - 1

    @pl.when(t == 0)
    def _():
        st_ref[...] = jnp.zeros_like(st_ref)
        uext_ref[0:POOL_CARRY, :] = jnp.zeros((POOL_CARRY, D_MODEL), F32)

    @pl.when(t > 0)
    def _():
        uext_ref[0:POOL_CARRY, :] = uext_ref[rows:rows + POOL_CARRY, :]

    h_ref[...] = _rms(x_ref[0], gmix_ref[...]).astype(BF16)

    def proj(seg):
        return _dot(h_ref[...], win_ref[:, seg * D_MODEL:(seg + 1) * D_MODEL])

    lb = _forget_lower_bound(lbl_ref[...])
    q_raw = proj(SEG_Q)
    q = q_raw * _sigmoid(q_raw)
    f = lb + (1.0 - lb) * _sigmoid(proj(SEG_F))
    k = 1.0 - f
    row_in_chunk = lax.broadcasted_iota(jnp.int32, (rows, D_MODEL), 0) % CHUNK
    p = f
    s = 1
    while s < CHUNK:
        p = p * jnp.where(row_in_chunk >= s, pltpu.roll(p, s, axis=0), 1.0)
        s *= 2
    p3 = p.reshape(n_chunks, CHUNK, D_MODEL)
    plast = p3[:, CHUNK - 1:CHUNK, :]
    qe_ref[...] = (q * p).astype(BF16)
    ke_ref[...] = (k / p).astype(BF16)
    kd_ref[...] = (k * (plast / p3).reshape(rows, D_MODEL)).astype(BF16)
    p_ref[...] = p
    v_ref[...] = proj(SEG_I).astype(BF16)

    causal = (lax.broadcasted_iota(jnp.int32, (CHUNK, CHUNK), 0)
              >= lax.broadcasted_iota(jnp.int32, (CHUNK, CHUNK), 1))

    for hd in range(HEADS):
        cols = slice(hd * HEAD_DIM, (hd + 1) * HEAD_DIM)
        st = st_ref[hd]
        for c in range(n_chunks):
            rs = slice(c * CHUNK, (c + 1) * CHUNK)
            qe = qe_ref[rs, cols]
            ke = ke_ref[rs, cols]
            kd = kd_ref[rs, cols]
            vv = v_ref[rs, cols]
            decay = p_ref[(c + 1) * CHUNK - 1:(c + 1) * CHUNK, cols]
            scores = jnp.where(causal, _dot_nt(qe, ke), 0.0)
            o_ref[rs, cols] = _dot_nt(qe, st.astype(BF16)) + _dot(scores.astype(BF16), vv)
            st = st * decay + _dot_tn(vv, kd)
        st_ref[hd] = st

    @pl.when(t == last_t)
    def _():
        for hd in range(HEADS):
            sfin_ref[0, hd] = st_ref[hd].T

    g_raw = proj(SEG_G)
    o = _head_norm_gate(o_ref[...], g_raw, hgn_ref[...])

    uext_ref[POOL_CARRY:POOL_CARRY + rows, :] = proj(SEG_U)
    pos1 = t * rows + lax.broadcasted_iota(jnp.int32, (rows, 1), 0) + 1
    pooled = []
    for g, w in enumerate(POOL_WINDOWS):
        cols = slice(g * POOL_GW, (g + 1) * POOL_GW)
        sw = uext_ref[:, cols]
        s = 1
        while s < w:
            sw = sw + pltpu.roll(sw, s, axis=0)
            s *= 2
        inv_cnt = 1.0 / jnp.minimum(pos1, w).astype(F32)
        dg = sw[POOL_CARRY:, :] * inv_cnt - uext_ref[POOL_CARRY:POOL_CARRY + rows, cols]
        pooled.append(_dot(dg.astype(BF16), wpool_ref[g]))
    y_pool = jnp.concatenate(pooled, axis=-1) * pscale_ref[...]

    @pl.when(t == last_t)
    def _():
        ptail_ref[0] = uext_ref[rows + POOL_CARRY - POOL_BUF:rows + POOL_CARRY, :]

    merged = _sigmoid(proj(SEG_A)) * o + _sigmoid(proj(SEG_B)) * y_pool
    x1 = x_ref[0] + _dot(merged.astype(BF16), wout_ref[...])
    x1_ref[0] = x1
    h2 = _rms(x1, gffn_ref[...])
    _store_row_tiles(h2_ref, (), h2)
    _route(h2, wr_ref, br_ref, ti_ref, tw_ref)


def _const_spec(shape):
    zeros = (0,) * len(shape)
    return pl.BlockSpec(shape, lambda *_: zeros, pipeline_mode=pl.Buffered(1))


def _mixer_prompt(x, gmix, win, lbl, hgn, wpool, pscale, wout, gffn, wr, br):
    b, t, _ = x.shape
    rows = MIXER_ROWS
    nt = t // rows
    tok_spec = pl.BlockSpec((1, rows, D_MODEL), lambda i, j: (i, j, 0))
    flat_spec = pl.BlockSpec((rows * ROW_TILE, LANES), lambda i, j: (i * nt + j, 0))
    lane_spec = pl.BlockSpec((rows, ROUTER_LANES), lambda i, j: (i * nt + j, 0))
    return pl.pallas_call(
        _mixer_prompt_kernel,
        grid=(b, nt),
        in_specs=[
            tok_spec,
            _const_spec((1, D_MODEL)),
            _const_spec(win.shape),
            _const_spec(lbl.shape),
            _const_spec((1, HEAD_DIM)),
            _const_spec(wpool.shape),
            _const_spec((1, D_MODEL)),
            _const_spec(wout.shape),
            _const_spec((1, D_MODEL)),
            _const_spec(wr.shape),
            _const_spec(br.shape),
        ],
        out_specs=[
            tok_spec,
            flat_spec,
            lane_spec,
            lane_spec,
            pl.BlockSpec((1, HEADS, HEAD_DIM, HEAD_DIM), lambda i, j: (i, 0, 0, 0)),
            pl.BlockSpec((1, POOL_BUF, D_MODEL), lambda i, j: (i, 0, 0)),
        ],
        out_shape=[
            jax.ShapeDtypeStruct(x.shape, F32),
            jax.ShapeDtypeStruct((b * t * ROW_TILE, LANES), F32),
            jax.ShapeDtypeStruct((b * t, ROUTER_LANES), jnp.int32),
            jax.ShapeDtypeStruct((b * t, ROUTER_LANES), F32),
            jax.ShapeDtypeStruct((b, HEADS, HEAD_DIM, HEAD_DIM), F32),
            jax.ShapeDtypeStruct((b, POOL_BUF, D_MODEL), F32),
        ],
        scratch_shapes=[
            pltpu.VMEM((HEADS, HEAD_DIM, HEAD_DIM), F32),
            pltpu.VMEM((rows + POOL_CARRY, D_MODEL), F32),
            pltpu.VMEM((rows, D_MODEL), BF16),
            pltpu.VMEM((rows, D_MODEL), BF16),
            pltpu.VMEM((rows, D_MODEL), BF16),
            pltpu.VMEM((rows, D_MODEL), BF16),
            pltpu.VMEM((rows, D_MODEL), F32),
            pltpu.VMEM((rows, D_MODEL), F32),
            pltpu.VMEM((rows, D_MODEL), BF16),
        ],
        compiler_params=pltpu.CompilerParams(dimension_semantics=("arbitrary", "arbitrary"),
                                             vmem_limit_bytes=VMEM_LIMIT),
        name="mixer_prompt",
    )(x, gmix, win, lbl, hgn, wpool, pscale, wout, gffn, wr, br)


SAMPLE_GROUP = 8


def _mixer_sample_kernel(pool_cnt, x_ref, gmix_ref, win_ref, wqft_ref, lblt_ref, hgn_ref, wpool_ref, pscale_ref,
                         wout_ref, gffn_ref, wr_ref, br_ref, s_ref, pbuf_ref,
                         x1_ref, h2_ref, ti_ref, tw_ref, snew_ref, pnew_ref,
                         h_ref, znat_ref, o_ref, bsum_ref):
    step = pl.program_id(0)
    last = pl.num_programs(0) - 1
    r0 = pl.multiple_of(step * SAMPLE_GROUP, SAMPLE_GROUP)

    @pl.when(step == 0)
    def _():
        hf = _rms(x_ref[...], gmix_ref[...])
        h_ref[...] = hf
        h = hf.astype(BF16)
        for i, seg in enumerate((SEG_I, SEG_G, SEG_U, SEG_A, SEG_B)):
            znat_ref[:, i * D_MODEL:(i + 1) * D_MODEL] = _dot(h, win_ref[:, seg * D_MODEL:(seg + 1) * D_MODEL])

    hg = h_ref[pl.ds(r0, SAMPLE_GROUP), :].astype(BF16)
    qft = _dot_nt(wqft_ref[...], hg)
    lbl = lblt_ref[...]
    m = jnp.max(lbl, axis=1, keepdims=True)
    e = jnp.exp(lbl - m)
    lb = e[:, 0:1] / jnp.sum(e, axis=1, keepdims=True)
    q_raw = qft[0:D_MODEL]
    qt = q_raw * _sigmoid(q_raw)
    ft = lb + (1.0 - lb) * _sigmoid(qft[D_MODEL:2 * D_MODEL])
    kt = 1.0 - ft

    for j in range(SAMPLE_GROUP):
        v_row = znat_ref[pl.ds(r0 + j, 1), 0:D_MODEL]
        o_parts = []
        for hd in range(HEADS):
            rs = slice(hd * HEAD_DIM, (hd + 1) * HEAD_DIM)
            s_new = ft[rs, j:j + 1] * s_ref[j, hd] + kt[rs, j:j + 1] * v_row[:, rs]
            snew_ref[j, hd] = s_new
            o_parts.append(jnp.sum(qt[rs, j:j + 1] * s_new, axis=0, keepdims=True))
        o_ref[pl.ds(r0 + j, 1), :] = jnp.concatenate(o_parts, axis=-1)

    u_g = znat_ref[pl.ds(r0, SAMPLE_GROUP), 2 * D_MODEL:3 * D_MODEL]
    sums = []
    for g, w in enumerate(POOL_WINDOWS):
        acc = jnp.zeros((SAMPLE_GROUP, POOL_GW), F32)
        for j in range(1, w):
            row = POOL_BUF - j
            acc = acc + pbuf_ref[:, row * D_MODEL + g * POOL_GW:row * D_MODEL + (g + 1) * POOL_GW]
        sums.append(acc)
    bsum_ref[pl.ds(r0, SAMPLE_GROUP), :] = jnp.concatenate(sums, axis=-1)
    pnew_ref[:, 0:(POOL_BUF - 1) * D_MODEL] = pbuf_ref[:, D_MODEL:POOL_BUF * D_MODEL]
    pnew_ref[:, (POOL_BUF - 1) * D_MODEL:POOL_BUF * D_MODEL] = u_g

    @pl.when(step == last)
    def _():
        x = x_ref[...]
        g_raw = znat_ref[:, D_MODEL:2 * D_MODEL]
        u = znat_ref[:, 2 * D_MODEL:3 * D_MODEL]
        o = _head_norm_gate(o_ref[...], g_raw, hgn_ref[...])
        sw = bsum_ref[...] + u
        pooled = []
        for g, w in enumerate(POOL_WINDOWS):
            cols = slice(g * POOL_GW, (g + 1) * POOL_GW)
            dg = sw[:, cols] * (1.0 / pool_cnt[g]) - u[:, cols]
            pooled.append(_dot(dg.astype(BF16), wpool_ref[g]))
        y_pool = jnp.concatenate(pooled, axis=-1) * pscale_ref[...]
        merged = (_sigmoid(znat_ref[:, 3 * D_MODEL:4 * D_MODEL]) * o
                  + _sigmoid(znat_ref[:, 4 * D_MODEL:5 * D_MODEL]) * y_pool)
        x1 = x + _dot(merged.astype(BF16), wout_ref[...])
        x1_ref[...] = x1
        h2 = _rms(x1, gffn_ref[...])
        _store_row_tiles(h2_ref, (), h2)
        _route(h2, wr_ref, br_ref, ti_ref, tw_ref)


def _mixer_sample(x, gmix, win, wqft, lblt, hgn, wpool, pscale, wout, gffn, wr, br, state, pbuf, start_pos):
    n = x.shape[0]
    steps = n // SAMPLE_GROUP
    pool_cnt = tuple(float(min(start_pos + 1, w)) for w in POOL_WINDOWS)
    pbuf2 = pbuf.reshape(n, POOL_BUF * D_MODEL)
    full = _const_spec
    out = pl.pallas_call(
        functools.partial(_mixer_sample_kernel, pool_cnt),
        grid=(steps,),
        in_specs=[
            full((n, D_MODEL)),
            full((1, D_MODEL)),
            full(win.shape),
            full(wqft.shape),
            full(lblt.shape),
            full((1, HEAD_DIM)),
            full(wpool.shape),
            full((1, D_MODEL)),
            full(wout.shape),
            full((1, D_MODEL)),
            full(wr.shape),
            full(br.shape),
            pl.BlockSpec((SAMPLE_GROUP, HEADS, HEAD_DIM, HEAD_DIM), lambda i: (i, 0, 0, 0)),
            pl.BlockSpec((SAMPLE_GROUP, POOL_BUF * D_MODEL), lambda i: (i, 0)),
        ],
        out_specs=[
            pl.BlockSpec((n, D_MODEL), lambda i: (0, 0)),
            pl.BlockSpec((n * ROW_TILE, LANES), lambda i: (0, 0)),
            pl.BlockSpec((n, ROUTER_LANES), lambda i: (0, 0)),
            pl.BlockSpec((n, ROUTER_LANES), lambda i: (0, 0)),
            pl.BlockSpec((SAMPLE_GROUP, HEADS, HEAD_DIM, HEAD_DIM), lambda i: (i, 0, 0, 0)),
            pl.BlockSpec((SAMPLE_GROUP, POOL_BUF * D_MODEL), lambda i: (i, 0)),
        ],
        out_shape=[
            jax.ShapeDtypeStruct((n, D_MODEL), F32),
            jax.ShapeDtypeStruct((n * ROW_TILE, LANES), F32),
            jax.ShapeDtypeStruct((n, ROUTER_LANES), jnp.int32),
            jax.ShapeDtypeStruct((n, ROUTER_LANES), F32),
            jax.ShapeDtypeStruct(state.shape, F32),
            jax.ShapeDtypeStruct(pbuf2.shape, F32),
        ],
        scratch_shapes=[
            pltpu.VMEM((n, D_MODEL), F32),
            pltpu.VMEM((n, 5 * D_MODEL), F32),
            pltpu.VMEM((n, D_MODEL), F32),
            pltpu.VMEM((n, D_MODEL), F32),
        ],
        compiler_params=pltpu.CompilerParams(dimension_semantics=("arbitrary",), vmem_limit_bytes=VMEM_LIMIT),
        name="mixer_sample",
    )(x, gmix, win, wqft, lblt, hgn, wpool, pscale, wout, gffn, wr, br, state, pbuf2)
    x1, h2, ti, tw, snew, pnew = out
    return x1, h2, ti, tw, snew, pnew.reshape(n, POOL_BUF, D_MODEL)


def _moe_kernel(n_tokens, be_ref, off_ref, nv_ref, nu_ref, sa_ref, h2_hbm,
                wg_ref, bg_ref, wu_ref, bu_ref, wd_ref, bd_ref, y_hbm,
                xbuf0, xbuf1, xbuf2, ybuf0, ybuf1, ybuf2, wgb, wub, wdb, xb_ref, gate_ref, act_ref, gsem, ssem):
    rows = MOE_ROWS
    i = pl.program_id(0)
    n_used = nu_ref[0]
    n_blocks = pl.num_programs(0)
    nv_back = lambda k: jnp.where(i >= k, nv_ref[jnp.maximum(i - k, 0)], 0)
    prev = jnp.maximum(i - 1, 0)
    nv_prev = nv_back(1)
    xbuf = (xbuf0, xbuf1, xbuf2)
    ybuf = (ybuf0, ybuf1, ybuf2)

    def tile_rows(first, n=1):
        start = first * ROW_TILE
        return pl.ds(start if isinstance(first, int) else pl.multiple_of(start, ROW_TILE), n * ROW_TILE)

    def gather_copy(a, r, s):
        tok = a >> 2
        return pltpu.make_async_copy(h2_hbm.at[tile_rows(tok), :], xbuf[s].at[tile_rows(r), :], gsem.at[s])

    def scatter_copy(a, r, n, s):
        dst = (a & (TOP_K - 1)) * n_tokens + (a >> 2)
        return pltpu.make_async_copy(ybuf[s].at[tile_rows(r, n), :], y_hbm.at[tile_rows(dst, n), :], ssem.at[s])

    def wait_gather(s):
        pltpu.make_async_copy(h2_hbm.at[tile_rows(0, rows), :], xbuf[s], gsem.at[s]).wait()

    def wait_scatter(n, s):
        size = rows
        while size >= 1:
            @pl.when((n & size) != 0)
            def _():
                scatter_copy(0, 0, size, s).wait()
            size //= 2

    def scatter_loop(blk, n, s):
        def body(r, c):
            scatter_copy(sa_ref[off_ref[blk] + r], r, 1, s).start()
            return c
        lax.fori_loop(0, n, body, 0)

    n_phases = 3

    def start_copies(phase, cur, full_prev):
        ahead, behind = (cur + 2) % MOE_DEPTH, (cur + MOE_DEPTH - 1) % MOE_DEPTH
        off_ahead = off_ref[jnp.minimum(i + 2, n_blocks - 1)]
        off_prev = off_ref[prev]
        for r in range(rows * phase // n_phases, rows * (phase + 1) // n_phases):
            gather_copy(sa_ref[off_ahead + r], r, ahead).start(priority=r % 2)
            if full_prev:
                scatter_copy(sa_ref[off_prev + r], r, 1, behind).start(priority=(r + 1) % 2)

    def phase_load(cur, full_prev):
        wait_gather(cur)
        xb_ref[...] = _load_row_tiles(xbuf[cur], (), rows).astype(BF16)

    def phase_gate(cur, full_prev):
        start_copies(0, cur, full_prev)
        gate_ref[...] = jnp.minimum(_dot(xb_ref[...], wgb[...]) + bg_ref[0], SWIGLU_LIMIT)

    def phase_up(cur, full_prev):
        start_copies(1, cur, full_prev)
        up = jnp.clip(_dot(xb_ref[...], wub[...]) + bu_ref[0], -SWIGLU_LIMIT, SWIGLU_LIMIT)
        gate = gate_ref[...]
        act_ref[...] = ((up + 1.0) * gate * _sigmoid(SWIGLU_ALPHA * gate)).astype(BF16)

    def phase_down(cur, full_prev):
        nxt, behind = (cur + 1) % MOE_DEPTH, (cur + MOE_DEPTH - 1) % MOE_DEPTH
        start_copies(2, cur, full_prev)
        wait_scatter(nv_back(MOE_DEPTH), cur)
        _store_row_tiles(ybuf[cur], (), _dot(act_ref[...], wdb[...]) + bd_ref[0])
        if not full_prev:
            scatter_loop(prev, nv_prev, behind)

        @pl.when(i == n_used - 1)
        def _():
            scatter_loop(i, nv_ref[i], cur)
            wait_scatter(nv_back(2), nxt)
            wait_scatter(nv_prev, behind)
            wait_scatter(nv_ref[i], cur)
            wait_gather(nxt)
            wait_gather((cur + 2) % MOE_DEPTH)

    @pl.when(i == 0)
    def _():
        for blk in range(MOE_DEPTH - 1):
            def body(r, c):
                gather_copy(sa_ref[off_ref[blk] + r], r, blk).start()
                return c
            lax.fori_loop(0, rows, body, 0)

    @pl.when(i < n_used)
    def _():
        first_of_expert = jnp.logical_or(i == 0, be_ref[i] != be_ref[prev])

        @pl.when(first_of_expert)
        def _():
            wgb[...] = wg_ref[0].astype(BF16)
            wub[...] = wu_ref[0].astype(BF16)
            wdb[...] = wd_ref[0].astype(BF16)

        prev_is_full = nv_prev == rows
        for phase in (phase_load, phase_gate, phase_up, phase_down):
            for cur in range(MOE_DEPTH):
                for full_prev in (True, False):
                    full_cond = prev_is_full if full_prev else jnp.logical_not(prev_is_full)

                    @pl.when(jnp.logical_and(i % MOE_DEPTH == cur, full_cond))
                    def _():
                        phase(cur, full_prev)


def _moe(h2, routing, wg, bg, wu, bu, wd, bd):
    rows = MOE_ROWS
    n_tokens = h2.shape[0] // ROW_TILE
    block_e, block_off, block_nv, n_used, sorted_a = routing
    n_blocks = block_e.shape[0]
    w_spec = pl.BlockSpec((1, D_MODEL, D_MODEL), lambda i, be, *_: (be[i], 0, 0))
    b_spec = pl.BlockSpec((1, 1, D_MODEL), lambda i, be, *_: (be[i], 0, 0))
    grid_spec = pltpu.PrefetchScalarGridSpec(
        num_scalar_prefetch=5,
        grid=(n_blocks,),
        in_specs=[pl.BlockSpec(memory_space=pl.ANY), w_spec, b_spec, w_spec, b_spec, w_spec, b_spec],
        out_specs=pl.BlockSpec(memory_space=pl.ANY),
        scratch_shapes=[
            *[pltpu.VMEM((rows * ROW_TILE, LANES), F32)] * (2 * MOE_DEPTH),
            pltpu.VMEM((D_MODEL, D_MODEL), BF16),
            pltpu.VMEM((D_MODEL, D_MODEL), BF16),
            pltpu.VMEM((D_MODEL, D_MODEL), BF16),
            pltpu.VMEM((rows, D_MODEL), BF16),
            pltpu.VMEM((rows, D_MODEL), F32),
            pltpu.VMEM((rows, D_MODEL), BF16),
            pltpu.SemaphoreType.DMA((MOE_DEPTH,)),
            pltpu.SemaphoreType.DMA((MOE_DEPTH,)),
        ],
    )
    return pl.pallas_call(
        functools.partial(_moe_kernel, n_tokens),
        grid_spec=grid_spec,
        out_shape=jax.ShapeDtypeStruct((n_tokens * TOP_K * ROW_TILE, LANES), F32),
        compiler_params=pltpu.CompilerParams(dimension_semantics=("arbitrary",), vmem_limit_bytes=VMEM_LIMIT),
        name="moe_experts",
    )(block_e, block_off, block_nv, n_used, sorted_a, h2, wg, bg, wu, bu, wd, bd)


ASSIGN_BITS = 17


def _moe_routing(top_i):
    rows = MOE_ROWS
    n_assign = top_i.shape[0] * TOP_K
    assert n_assign <= 1 << ASSIGN_BITS
    n_blocks = -(-n_assign // rows) + N_EXPERTS
    flat_e = top_i.reshape(-1)
    keys = jnp.sort(flat_e * (1 << ASSIGN_BITS) + jnp.arange(n_assign, dtype=jnp.int32))
    smem_len = -(-(n_assign + rows) // 1024) * 1024
    sorted_a = jnp.pad(keys & ((1 << ASSIGN_BITS) - 1), (0, smem_len - n_assign))
    experts = jnp.arange(N_EXPERTS, dtype=jnp.int32)
    counts = jnp.sum((flat_e[:, None] == experts[None, :]).astype(jnp.int32), axis=0)
    nblk = (counts + rows - 1) // rows
    blk_end = jnp.cumsum(nblk)
    blk_start = blk_end - nblk
    row_start = jnp.cumsum(counts) - counts
    blk = jnp.arange(n_blocks, dtype=jnp.int32)
    owner = (blk[:, None] >= blk_start[None, :]) & (blk[:, None] < blk_end[None, :])
    pick = lambda v: jnp.sum(jnp.where(owner, v[None, :], 0), axis=1)
    within = (blk - pick(blk_start)) * rows
    used = blk < blk_end[-1]
    block_e = jnp.where(used, pick(experts), N_EXPERTS - 1).astype(jnp.int32)
    block_off = jnp.where(used, pick(row_start) + within, 0).astype(jnp.int32)
    block_nv = jnp.where(used, jnp.clip(pick(counts) - within, 0, rows), 0).astype(jnp.int32)
    n_used = blk_end[-1].astype(jnp.int32).reshape(1)
    return block_e, block_off, block_nv, n_used, sorted_a.astype(jnp.int32)


def _final_kernel(x1_ref, y_ref, tw_ref, p_ref, gple_ref, wpg_ref, wpp_ref, gfin_ref, out_ref):
    x = x1_ref[...]
    tw = tw_ref[...]
    rows = x.shape[0]
    for j in range(TOP_K):
        x = x + tw[:, j:j + 1] * _load_row_tiles(y_ref, (j,), rows)
    gate = _sigmoid(_dot(_rms(x, gple_ref[...]).astype(BF16), wpg_ref[...]))
    x = x + gate * _dot(p_ref[...].astype(BF16), wpp_ref[...])
    out_ref[...] = _rms(x, gfin_ref[...])


def _final(x1, y4, tw, p, gple, wpg, wpp, gfin, rows, row_offset):
    n = x1.shape[0]
    off = row_offset // rows
    assert off * rows == row_offset and n % rows == 0
    ple = p.shape[-1]
    return pl.pallas_call(
        _final_kernel,
        grid=(n // rows,),
        in_specs=[
            pl.BlockSpec((rows, D_MODEL), lambda i: (i, 0)),
            pl.BlockSpec((TOP_K, rows * ROW_TILE, LANES), lambda i: (0, i + off, 0)),
            pl.BlockSpec((rows, ROUTER_LANES), lambda i: (i, 0)),
            pl.BlockSpec((rows, ple), lambda i: (i, 0)),
            _const_spec((1, D_MODEL)),
            _const_spec(wpg.shape),
            _const_spec(wpp.shape),
            _const_spec((1, D_MODEL)),
        ],
        out_specs=pl.BlockSpec((rows, D_MODEL), lambda i: (i, 0)),
        out_shape=jax.ShapeDtypeStruct((n, D_MODEL), F32),
        compiler_params=pltpu.CompilerParams(dimension_semantics=("arbitrary",), vmem_limit_bytes=VMEM_LIMIT),
        name="final",
    )(x1, y4, tw, p, gple, wpg, wpp, gfin)


def kernel(x_prompt, x_sample, p_prompt, p_sample, state_hgrn, state_pool, g_mix, w_in, hg_lb_logits, hg_norm, w_pool,
           pool_scale, w_out, g_ffn, w_router, b_router, w_gate, b_gate, w_up, b_up, w_down, b_down, g_ple,
           w_ple_gate, w_ple_proj, g_final):
    depth = w_in.shape[0]
    assert depth == 1, "single-layer step"
    b, t, _ = x_prompt.shape
    ns = x_sample.shape[0]
    assert x_sample.shape[1] == 1
    n_prompt = b * t
    n_tok = n_prompt + ns

    row = lambda a: a.reshape(1, -1)
    win = w_in[0].astype(BF16)
    wqft = win[:, :2 * D_MODEL].T
    lbl = hg_lb_logits.astype(F32)
    wpool = w_pool[0].astype(BF16)
    wout = w_out[0].astype(BF16)
    wr = jnp.pad(w_router[0], ((0, 0), (0, ROUTER_LANES - N_EXPERTS)))
    br = jnp.pad(b_router[0], (0, ROUTER_LANES - N_EXPERTS)).reshape(1, -1)
    shared = (row(hg_norm[0]), wpool, row(pool_scale[0]), wout, row(g_ffn[0]), wr, br)

    x1_p, h2_p, ti_p, tw_p, s_p, pool_p = _mixer_prompt(x_prompt, row(g_mix[0]), win, lbl, *shared)
    x1_s, h2_s, ti_s, tw_s, s_s, pool_s = _mixer_sample(
        x_sample.reshape(ns, D_MODEL), row(g_mix[0]), win, wqft, lbl.T, *shared,
        state_hgrn[0], state_pool[0], start_pos=PAST_LEN)

    top_i = jnp.concatenate([ti_p[:, :TOP_K], ti_s[:, :TOP_K]], axis=0)
    routing = _moe_routing(top_i)
    e3 = lambda a: a[0].reshape(N_EXPERTS, 1, D_MODEL)
    h2 = jnp.concatenate([h2_p, h2_s], axis=0)
    y4 = _moe(h2, routing, w_gate[0], e3(b_gate), w_up[0], e3(b_up), w_down[0], e3(b_down))
    y4 = y4.reshape(TOP_K, n_tok * ROW_TILE, LANES)

    wpg = w_ple_gate[0].astype(BF16)
    wpp = w_ple_proj[0].astype(BF16)
    fin = (row(g_ple[0]), wpg, wpp, row(g_final))
    y_p = _final(x1_p.reshape(n_prompt, D_MODEL), y4, tw_p, p_prompt[0].reshape(n_prompt, -1), *fin,
                 rows=FINAL_ROWS, row_offset=0)
    y_s = _final(x1_s, y4, tw_s, p_sample[0].reshape(ns, -1), *fin, rows=ns, row_offset=n_prompt)

    return (y_p.reshape(b, t, D_MODEL), y_s.reshape(ns, 1, D_MODEL), s_p[None], pool_p[None], s_s[None],
            pool_s[None])
```

```python
import functools

import jax
import jax.numpy as jnp
from jax import lax
from jax.experimental import pallas as pl
from jax.experimental.pallas import tpu as pltpu

F32 = jnp.float32
BF16 = jnp.bfloat16

D_MODEL = 1024
HEADS = 8
HEAD_DIM = 128
CHUNK = 32
POOL_WINDOWS = (2, 4, 8, 16)
POOL_GW = D_MODEL // len(POOL_WINDOWS)
POOL_BUF = 15
POOL_CARRY = 16
N_EXPERTS = 32
TOP_K = 4
ROUTER_LANES = 128
SWIGLU_LIMIT = 7.0
SWIGLU_ALPHA = 1.702
EPS = 1e-6
PAST_LEN = 16384

SEG_Q, SEG_F, SEG_I, SEG_G, SEG_U, SEG_A, SEG_B = range(7)

MIXER_ROWS = 512
MOE_ROWS = 256
MOE_DEPTH = 3
FINAL_ROWS = 512
VMEM_LIMIT = 56 * 1024 * 1024


def _rms(x, g):
    ms = jnp.mean(x * x, axis=-1, keepdims=True)
    return x * lax.rsqrt(ms + EPS) * g


def _sigmoid(x):
    return 1.0 / (1.0 + jnp.exp(-x))


def _dot(a, b):
    return jnp.dot(a, b, preferred_element_type=F32)


def _dot_nt(a, b):
    return lax.dot_general(a, b, (((1,), (1,)), ((), ())), preferred_element_type=F32)


def _dot_tn(a, b):
    return lax.dot_general(a, b, (((0,), (0,)), ((), ())), preferred_element_type=F32)


LANES = 128
ROW_TILE = D_MODEL // LANES


def _store_row_tiles(ref, lead, x, period=ROW_TILE, offset=0):
    rows = x.shape[0]
    for c in range(ROW_TILE):
        ref[lead + (pl.ds(offset + c, rows, stride=period), slice(None))] = x[:, c * LANES:(c + 1) * LANES]


def _load_row_tiles(ref, lead, rows, period=ROW_TILE, offset=0):
    return jnp.concatenate(
        [ref[lead + (pl.ds(offset + c, rows, stride=period), slice(None))] for c in range(ROW_TILE)], axis=-1)


def _split_bf16(x):
    hi = x.astype(BF16)
    lo = (x - hi.astype(F32)).astype(BF16)
    return hi, lo


def _forget_lower_bound(lbl):
    m = jnp.max(lbl, axis=0, keepdims=True)
    e = jnp.exp(lbl - m)
    return e[0:1] / jnp.sum(e, axis=0, keepdims=True)


def _head_norm_gate(o, g_raw, hgn):
    parts = []
    for h in range(HEADS):
        oh = o[:, h * HEAD_DIM:(h + 1) * HEAD_DIM]
        parts.append(_rms(oh, hgn))
    return jnp.concatenate(parts, axis=-1) * (g_raw * _sigmoid(g_raw))


def _route(h2, wr_ref, br_ref, ti_ref, tw_ref):
    rows = h2.shape[0]
    h_hi, h_lo = _split_bf16(h2)
    w_hi, w_lo = _split_bf16(wr_ref[...])
    logits = _dot(h_hi, w_hi) + _dot(h_lo, w_hi) + _dot(h_hi, w_lo) + br_ref[...]
    lane = lax.broadcasted_iota(jnp.int32, (rows, ROUTER_LANES), 1)
    neg = jnp.float32(-jnp.inf)
    l = jnp.where(lane < N_EXPERTS, logits, neg)
    ti = jnp.zeros((rows, ROUTER_LANES), jnp.int32)
    tw = jnp.zeros((rows, ROUTER_LANES), F32)
    m0 = None
    denom = None
    es = []
    for j in range(TOP_K):
        m = jnp.max(l, axis=-1, keepdims=True)
        idx = jnp.min(jnp.where(l == m, lane, ROUTER_LANES), axis=-1, keepdims=True)
        l = jnp.where(lane == idx, neg, l)
        if j == 0:
            m0 = m
        e = jnp.exp(m - m0)
        es.append(e)
        denom = e if denom is None else denom + e
        ti = jnp.where(lane == j, idx, ti)
    for j in range(TOP_K):
        tw = jnp.where(lane == j, es[j] / denom, tw)
    ti_ref[...] = ti
    tw_ref[...] = tw


def _mixer_prompt_kernel(nt, x_ref, gmix_ref, win_ref, lbl_ref, hgn_ref, wpool_ref, pscale_ref, wout_ref, gffn_ref,
                         wr_ref, br_ref, h2s_ref, x1_ref, h2_ref, *rest):
    s = pl.program_id(0)
    n_main = pl.num_programs(0) - 1

    @pl.when(s < n_main)
    def _():
        _mixer_prompt_block(s % nt, nt - 1, x_ref, gmix_ref, win_ref, lbl_ref, hgn_ref, wpool_ref, pscale_ref,
                            wout_ref, gffn_ref, wr_ref, br_ref, x1_ref, h2_ref, *rest)

    @pl.when(s == n_main)
    def _():
        h2_ref[0:h2s_ref.shape[0], :] = h2s_ref[...]


def _mixer_prompt_block(t, last_t, x_ref, gmix_ref, win_ref, lbl_ref, hgn_ref, wpool_ref, pscale_ref, wout_ref,
                        gffn_ref, wr_ref, br_ref,
                        x1_ref, h2_ref, ti_ref, tw_ref, sfin_ref, ptail_ref,
                        st_ref, uext_ref, qe_ref, ke_ref, kd_ref, v_ref, o_ref, p_ref, h_ref):
    rows = MIXER_ROWS
    n_chunks = rows // CHUNK

    @pl.when(t == 0)
    def _():
        st_ref[...] = jnp.zeros_like(st_ref)
        uext_ref[0:POOL_CARRY, :] = jnp.zeros((POOL_CARRY, D_MODEL), F32)

    @pl.when(t > 0)
    def _():
        uext_ref[0:POOL_CARRY, :] = uext_ref[rows:rows + POOL_CARRY, :]

    h_ref[...] = _rms(x_ref[0], gmix_ref[...]).astype(BF16)

    def proj(seg):
        return _dot(h_ref[...], win_ref[:, seg * D_MODEL:(seg + 1) * D_MODEL])

    lb = _forget_lower_bound(lbl_ref[...])
    q_raw = proj(SEG_Q)
    q = q_raw * _sigmoid(q_raw)
    f = lb + (1.0 - lb) * _sigmoid(proj(SEG_F))
    k = 1.0 - f
    row_in_chunk = lax.broadcasted_iota(jnp.int32, (rows, D_MODEL), 0) % CHUNK
    p = f
    s = 1
    while s < CHUNK:
        p = p * jnp.where(row_in_chunk >= s, pltpu.roll(p, s, axis=0), 1.0)
        s *= 2
    p3 = p.reshape(n_chunks, CHUNK, D_MODEL)
    plast = p3[:, CHUNK - 1:CHUNK, :]
    qe_ref[...] = (q * p).astype(BF16)
    ke_ref[...] = (k / p).astype(BF16)
    kd_ref[...] = (k * (plast / p3).reshape(rows, D_MODEL)).astype(BF16)
    p_ref[...] = p
    v_ref[...] = proj(SEG_I).astype(BF16)

    causal = (lax.broadcasted_iota(jnp.int32, (CHUNK, CHUNK), 0)
              >= lax.broadcasted_iota(jnp.int32, (CHUNK, CHUNK), 1))

    for hd in range(HEADS):
        cols = slice(hd * HEAD_DIM, (hd + 1) * HEAD_DIM)
        st = st_ref[hd]
        for c in range(n_chunks):
            rs = slice(c * CHUNK, (c + 1) * CHUNK)
            qe = qe_ref[rs, cols]
            ke = ke_ref[rs, cols]
            kd = kd_ref[rs, cols]
            vv = v_ref[rs, cols]
            decay = p_ref[(c + 1) * CHUNK - 1:(c + 1) * CHUNK, cols]
            scores = jnp.where(causal, _dot_nt(qe, ke), 0.0)
            o_ref[rs, cols] = _dot_nt(qe, st.astype(BF16)) + _dot(scores.astype(BF16), vv)
            st = st * decay + _dot_tn(vv, kd)
        st_ref[hd] = st

    @pl.when(t == last_t)
    def _():
        for hd in range(HEADS):
            sfin_ref[0, hd] = st_ref[hd].T

    g_raw = proj(SEG_G)
    o = _head_norm_gate(o_ref[...], g_raw, hgn_ref[...])

    uext_ref[POOL_CARRY:POOL_CARRY + rows, :] = proj(SEG_U)
    pos1 = t * rows + lax.broadcasted_iota(jnp.int32, (rows, 1), 0) + 1
    pooled = []
    for g, w in enumerate(POOL_WINDOWS):
        cols = slice(g * POOL_GW, (g + 1) * POOL_GW)
        sw = uext_ref[:, cols]
        s = 1
        while s < w:
            sw = sw + pltpu.roll(sw, s, axis=0)
            s *= 2
        inv_cnt = 1.0 / jnp.minimum(pos1, w).astype(F32)
        dg = sw[POOL_CARRY:, :] * inv_cnt - uext_ref[POOL_CARRY:POOL_CARRY + rows, cols]
        pooled.append(_dot(dg.astype(BF16), wpool_ref[g]))
    y_pool = jnp.concatenate(pooled, axis=-1) * pscale_ref[...]

    @pl.when(t == last_t)
    def _():
        ptail_ref[0] = uext_ref[rows + POOL_CARRY - POOL_BUF:rows + POOL_CARRY, :]

    merged = _sigmoid(proj(SEG_A)) * o + _sigmoid(proj(SEG_B)) * y_pool
    x1 = x_ref[0] + _dot(merged.astype(BF16), wout_ref[...])
    x1_ref[0] = x1
    h2 = _rms(x1, gffn_ref[...])
    _store_row_tiles(h2_ref, (), h2)
    _route(h2, wr_ref, br_ref, ti_ref, tw_ref)


def _const_spec(shape):
    zeros = (0,) * len(shape)
    return pl.BlockSpec(shape, lambda *_: zeros, pipeline_mode=pl.Buffered(1))


def _mixer_prompt(x, gmix, win, lbl, hgn, wpool, pscale, wout, gffn, wr, br, h2_tail):
    b, t, _ = x.shape
    rows = MIXER_ROWS
    nt = t // rows
    n_main = b * nt
    main = lambda s: jnp.minimum(s, n_main - 1)
    tok_spec = pl.BlockSpec((1, rows, D_MODEL), lambda s: (main(s) // nt, main(s) % nt, 0))
    flat_spec = pl.BlockSpec((rows * ROW_TILE, LANES), lambda s: (s, 0))
    lane_spec = pl.BlockSpec((rows, ROUTER_LANES), lambda s: (main(s), 0))
    assert h2_tail.shape[0] <= rows * ROW_TILE
    return pl.pallas_call(
        functools.partial(_mixer_prompt_kernel, nt),
        grid=(n_main + 1,),
        in_specs=[
            tok_spec,
            _const_spec((1, D_MODEL)),
            _const_spec(win.shape),
            _const_spec(lbl.shape),
            _const_spec((1, HEAD_DIM)),
            _const_spec(wpool.shape),
            _const_spec((1, D_MODEL)),
            _const_spec(wout.shape),
            _const_spec((1, D_MODEL)),
            _const_spec(wr.shape),
            _const_spec(br.shape),
            _const_spec(h2_tail.shape),
        ],
        out_specs=[
            tok_spec,
            flat_spec,
            lane_spec,
            lane_spec,
            pl.BlockSpec((1, HEADS, HEAD_DIM, HEAD_DIM), lambda s: (main(s) // nt, 0, 0, 0)),
            pl.BlockSpec((1, POOL_BUF, D_MODEL), lambda s: (main(s) // nt, 0, 0)),
        ],
        out_shape=[
            jax.ShapeDtypeStruct(x.shape, F32),
            jax.ShapeDtypeStruct((b * t * ROW_TILE + h2_tail.shape[0], LANES), F32),
            jax.ShapeDtypeStruct((b * t, ROUTER_LANES), jnp.int32),
            jax.ShapeDtypeStruct((b * t, ROUTER_LANES), F32),
            jax.ShapeDtypeStruct((b, HEADS, HEAD_DIM, HEAD_DIM), F32),
            jax.ShapeDtypeStruct((b, POOL_BUF, D_MODEL), F32),
        ],
        scratch_shapes=[
            pltpu.VMEM((HEADS, HEAD_DIM, HEAD_DIM), F32),
            pltpu.VMEM((rows + POOL_CARRY, D_MODEL), F32),
            pltpu.VMEM((rows, D_MODEL), BF16),
            pltpu.VMEM((rows, D_MODEL), BF16),
            pltpu.VMEM((rows, D_MODEL), BF16),
            pltpu.VMEM((rows, D_MODEL), BF16),
            pltpu.VMEM((rows, D_MODEL), F32),
            pltpu.VMEM((rows, D_MODEL), F32),
            pltpu.VMEM((rows, D_MODEL), BF16),
        ],
        compiler_params=pltpu.CompilerParams(dimension_semantics=("arbitrary",), vmem_limit_bytes=VMEM_LIMIT),
        name="mixer_prompt",
    )(x, gmix, win, lbl, hgn, wpool, pscale, wout, gffn, wr, br, h2_tail)


SAMPLE_GROUP = 8


def _mixer_sample_kernel(pool_cnt, x_ref, gmix_ref, win_ref, wqft_ref, lblt_ref, hgn_ref, wpool_ref, pscale_ref,
                         wout_ref, gffn_ref, wr_ref, br_ref, s_ref, pbuf_ref,
                         x1_ref, h2_ref, ti_ref, tw_ref, snew_ref, pnew_ref,
                         h_ref, znat_ref, o_ref, bsum_ref):
    step = pl.program_id(0)
    last = pl.num_programs(0) - 1
    r0 = pl.multiple_of(step * SAMPLE_GROUP, SAMPLE_GROUP)

    @pl.when(step == 0)
    def _():
        hf = _rms(x_ref[...], gmix_ref[...])
        h_ref[...] = hf
        h = hf.astype(BF16)
        for i, seg in enumerate((SEG_I, SEG_G, SEG_U, SEG_A, SEG_B)):
            znat_ref[:, i * D_MODEL:(i + 1) * D_MODEL] = _dot(h, win_ref[:, seg * D_MODEL:(seg + 1) * D_MODEL])

    hg = h_ref[pl.ds(r0, SAMPLE_GROUP), :].astype(BF16)
    qft = _dot_nt(wqft_ref[...], hg)
    lbl = lblt_ref[...]
    m = jnp.max(lbl, axis=1, keepdims=True)
    e = jnp.exp(lbl - m)
    lb = e[:, 0:1] / jnp.sum(e, axis=1, keepdims=True)
    q_raw = qft[0:D_MODEL]
    qt = q_raw * _sigmoid(q_raw)
    ft = lb + (1.0 - lb) * _sigmoid(qft[D_MODEL:2 * D_MODEL])
    kt = 1.0 - ft

    for j in range(SAMPLE_GROUP):
        v_row = znat_ref[pl.ds(r0 + j, 1), 0:D_MODEL]
        o_parts = []
        for hd in range(HEADS):
            rs = slice(hd * HEAD_DIM, (hd + 1) * HEAD_DIM)
            s_new = ft[rs, j:j + 1] * s_ref[j, hd] + kt[rs, j:j + 1] * v_row[:, rs]
            snew_ref[j, hd] = s_new
            o_parts.append(jnp.sum(qt[rs, j:j + 1] * s_new, axis=0, keepdims=True))
        o_ref[pl.ds(r0 + j, 1), :] = jnp.concatenate(o_parts, axis=-1)

    u_g = znat_ref[pl.ds(r0, SAMPLE_GROUP), 2 * D_MODEL:3 * D_MODEL]
    sums = []
    for g, w in enumerate(POOL_WINDOWS):
        acc = jnp.zeros((SAMPLE_GROUP, POOL_GW), F32)
        for j in range(1, w):
            row = POOL_BUF - j
            acc = acc + pbuf_ref[:, row * D_MODEL + g * POOL_GW:row * D_MODEL + (g + 1) * POOL_GW]
        sums.append(acc)
    bsum_ref[pl.ds(r0, SAMPLE_GROUP), :] = jnp.concatenate(sums, axis=-1)
    pnew_ref[:, 0:(POOL_BUF - 1) * D_MODEL] = pbuf_ref[:, D_MODEL:POOL_BUF * D_MODEL]
    pnew_ref[:, (POOL_BUF - 1) * D_MODEL:POOL_BUF * D_MODEL] = u_g

    @pl.when(step == last)
    def _():
        x = x_ref[...]
        g_raw = znat_ref[:, D_MODEL:2 * D_MODEL]
        u = znat_ref[:, 2 * D_MODEL:3 * D_MODEL]
        o = _head_norm_gate(o_ref[...], g_raw, hgn_ref[...])
        sw = bsum_ref[...] + u
        pooled = []
        for g, w in enumerate(POOL_WINDOWS):
            cols = slice(g * POOL_GW, (g + 1) * POOL_GW)
            dg = sw[:, cols] * (1.0 / pool_cnt[g]) - u[:, cols]
            pooled.append(_dot(dg.astype(BF16), wpool_ref[g]))
        y_pool = jnp.concatenate(pooled, axis=-1) * pscale_ref[...]
        merged = (_sigmoid(znat_ref[:, 3 * D_MODEL:4 * D_MODEL]) * o
                  + _sigmoid(znat_ref[:, 4 * D_MODEL:5 * D_MODEL]) * y_pool)
        x1 = x + _dot(merged.astype(BF16), wout_ref[...])
        x1_ref[...] = x1
        h2 = _rms(x1, gffn_ref[...])
        _store_row_tiles(h2_ref, (), h2)
        _route(h2, wr_ref, br_ref, ti_ref, tw_ref)


def _mixer_sample(x, gmix, win, wqft, lblt, hgn, wpool, pscale, wout, gffn, wr, br, state, pbuf, start_pos):
    n = x.shape[0]
    steps = n // SAMPLE_GROUP
    pool_cnt = tuple(float(min(start_pos + 1, w)) for w in POOL_WINDOWS)
    pbuf2 = pbuf.reshape(n, POOL_BUF * D_MODEL)
    full = _const_spec
    out = pl.pallas_call(
        functools.partial(_mixer_sample_kernel, pool_cnt),
        grid=(steps,),
        in_specs=[
            full((n, D_MODEL)),
            full((1, D_MODEL)),
            full(win.shape),
            full(wqft.shape),
            full(lblt.shape),
            full((1, HEAD_DIM)),
            full(wpool.shape),
            full((1, D_MODEL)),
            full(wout.shape),
            full((1, D_MODEL)),
            full(wr.shape),
            full(br.shape),
            pl.BlockSpec((SAMPLE_GROUP, HEADS, HEAD_DIM, HEAD_DIM), lambda i: (i, 0, 0, 0)),
            pl.BlockSpec((SAMPLE_GROUP, POOL_BUF * D_MODEL), lambda i: (i, 0)),
        ],
        out_specs=[
            pl.BlockSpec((n, D_MODEL), lambda i: (0, 0)),
            pl.BlockSpec((n * ROW_TILE, LANES), lambda i: (0, 0)),
            pl.BlockSpec((n, ROUTER_LANES), lambda i: (0, 0)),
            pl.BlockSpec((n, ROUTER_LANES), lambda i: (0, 0)),
            pl.BlockSpec((SAMPLE_GROUP, HEADS, HEAD_DIM, HEAD_DIM), lambda i: (i, 0, 0, 0)),
            pl.BlockSpec((SAMPLE_GROUP, POOL_BUF * D_MODEL), lambda i: (i, 0)),
        ],
        out_shape=[
            jax.ShapeDtypeStruct((n, D_MODEL), F32),
            jax.ShapeDtypeStruct((n * ROW_TILE, LANES), F32),
            jax.ShapeDtypeStruct((n, ROUTER_LANES), jnp.int32),
            jax.ShapeDtypeStruct((n, ROUTER_LANES), F32),
            jax.ShapeDtypeStruct(state.shape, F32),
            jax.ShapeDtypeStruct(pbuf2.shape, F32),
        ],
        scratch_shapes=[
            pltpu.VMEM((n, D_MODEL), F32),
            pltpu.VMEM((n, 5 * D_MODEL), F32),
            pltpu.VMEM((n, D_MODEL), F32),
            pltpu.VMEM((n, D_MODEL), F32),
        ],
        compiler_params=pltpu.CompilerParams(dimension_semantics=("arbitrary",), vmem_limit_bytes=VMEM_LIMIT),
        name="mixer_sample",
    )(x, gmix, win, wqft, lblt, hgn, wpool, pscale, wout, gffn, wr, br, state, pbuf2)
    x1, h2, ti, tw, snew, pnew = out
    return x1, h2, ti, tw, snew, pnew.reshape(n, POOL_BUF, D_MODEL)


def _moe_kernel(n_tokens, be_ref, off_ref, nv_ref, nu_ref, sa_ref, h2_hbm,
                wg_ref, bg_ref, wu_ref, bu_ref, wd_ref, bd_ref, y_hbm,
                xbuf0, xbuf1, xbuf2, ybuf0, ybuf1, ybuf2, wgb, wub, wdb, xb_ref, gate_ref, act_ref, gsem, ssem):
    rows = MOE_ROWS
    i = pl.program_id(0)
    n_used = nu_ref[0]
    n_blocks = pl.num_programs(0)
    nv_back = lambda k: jnp.where(i >= k, nv_ref[jnp.maximum(i - k, 0)], 0)
    prev = jnp.maximum(i - 1, 0)
    nv_prev = nv_back(1)
    xbuf = (xbuf0, xbuf1, xbuf2)
    ybuf = (ybuf0, ybuf1, ybuf2)

    def tile_rows(first, n=1):
        start = first * ROW_TILE
        return pl.ds(start if isinstance(first, int) else pl.multiple_of(start, ROW_TILE), n * ROW_TILE)

    def gather_copy(a, r, s):
        tok = a >> 2
        return pltpu.make_async_copy(h2_hbm.at[tile_rows(tok), :], xbuf[s].at[tile_rows(r), :], gsem.at[s])

    def scatter_copy(a, r, n, s):
        dst = (a & (TOP_K - 1)) * n_tokens + (a >> 2)
        return pltpu.make_async_copy(ybuf[s].at[tile_rows(r, n), :], y_hbm.at[tile_rows(dst, n), :], ssem.at[s])

    def wait_gather(s):
        pltpu.make_async_copy(h2_hbm.at[tile_rows(0, rows), :], xbuf[s], gsem.at[s]).wait()

    def wait_scatter(n, s):
        size = rows
        while size >= 1:
            @pl.when((n & size) != 0)
            def _():
                scatter_copy(0, 0, size, s).wait()
            size //= 2

    def scatter_loop(blk, n, s):
        def body(r, c):
            scatter_copy(sa_ref[off_ref[blk] + r], r, 1, s).start()
            return c
        lax.fori_loop(0, n, body, 0)

    n_phases = 3

    def start_copies(phase, cur, full_prev):
        ahead, behind = (cur + 2) % MOE_DEPTH, (cur + MOE_DEPTH - 1) % MOE_DEPTH
        off_ahead = off_ref[jnp.minimum(i + 2, n_blocks - 1)]
        off_prev = off_ref[prev]
        for r in range(rows * phase // n_phases, rows * (phase + 1) // n_phases):
            gather_copy(sa_ref[off_ahead + r], r, ahead).start(priority=1)
            if full_prev:
                scatter_copy(sa_ref[off_prev + r], r, 1, behind).start(priority=0)

    def phase_load(cur, full_prev):
        wait_gather(cur)
        xb_ref[...] = _load_row_tiles(xbuf[cur], (), rows).astype(BF16)

    def phase_gate(cur, full_prev):
        start_copies(0, cur, full_prev)
        gate_ref[...] = jnp.minimum(_dot(xb_ref[...], wgb[...]) + bg_ref[0], SWIGLU_LIMIT)

    def phase_up(cur, full_prev):
        start_copies(1, cur, full_prev)
        up = jnp.clip(_dot(xb_ref[...], wub[...]) + bu_ref[0], -SWIGLU_LIMIT, SWIGLU_LIMIT)
        gate = gate_ref[...]
        act_ref[...] = ((up + 1.0) * gate * _sigmoid(SWIGLU_ALPHA * gate)).astype(BF16)

    def phase_down(cur, full_prev):
        nxt, behind = (cur + 1) % MOE_DEPTH, (cur + MOE_DEPTH - 1) % MOE_DEPTH
        start_copies(2, cur, full_prev)
        wait_scatter(nv_back(MOE_DEPTH), cur)
        _store_row_tiles(ybuf[cur], (), _dot(act_ref[...], wdb[...]) + bd_ref[0])
        if not full_prev:
            scatter_loop(prev, nv_prev, behind)

        @pl.when(i == n_used - 1)
        def _():
            scatter_loop(i, nv_ref[i], cur)
            wait_scatter(nv_back(2), nxt)
            wait_scatter(nv_prev, behind)
            wait_scatter(nv_ref[i], cur)
            wait_gather(nxt)
            wait_gather((cur + 2) % MOE_DEPTH)

    @pl.when(i == 0)
    def _():
        for blk in range(MOE_DEPTH - 1):
            def body(r, c):
                gather_copy(sa_ref[off_ref[blk] + r], r, blk).start()
                return c
            lax.fori_loop(0, rows, body, 0)

    @pl.when(i < n_used)
    def _():
        first_of_expert = jnp.logical_or(i == 0, be_ref[i] != be_ref[prev])

        @pl.when(first_of_expert)
        def _():
            wgb[...] = wg_ref[0].astype(BF16)
            wub[...] = wu_ref[0].astype(BF16)
            wdb[...] = wd_ref[0].astype(BF16)

        prev_is_full = nv_prev == rows
        for phase in (phase_load, phase_gate, phase_up, phase_down):
            for cur in range(MOE_DEPTH):
                for full_prev in (True, False):
                    full_cond = prev_is_full if full_prev else jnp.logical_not(prev_is_full)

                    @pl.when(jnp.logical_and(i % MOE_DEPTH == cur, full_cond))
                    def _():
                        phase(cur, full_prev)


def _moe(h2, routing, wg, bg, wu, bu, wd, bd):
    rows = MOE_ROWS
    n_tokens = h2.shape[0] // ROW_TILE
    block_e, block_off, block_nv, n_used, sorted_a = routing
    n_blocks = block_e.shape[0]
    w_spec = pl.BlockSpec((1, D_MODEL, D_MODEL), lambda i, be, *_: (be[i], 0, 0))
    b_spec = pl.BlockSpec((1, 1, D_MODEL), lambda i, be, *_: (be[i], 0, 0))
    grid_spec = pltpu.PrefetchScalarGridSpec(
        num_scalar_prefetch=5,
        grid=(n_blocks,),
        in_specs=[pl.BlockSpec(memory_space=pl.ANY), w_spec, b_spec, w_spec, b_spec, w_spec, b_spec],
        out_specs=pl.BlockSpec(memory_space=pl.ANY),
        scratch_shapes=[
            *[pltpu.VMEM((rows * ROW_TILE, LANES), F32)] * (2 * MOE_DEPTH),
            pltpu.VMEM((D_MODEL, D_MODEL), BF16),
            pltpu.VMEM((D_MODEL, D_MODEL), BF16),
            pltpu.VMEM((D_MODEL, D_MODEL), BF16),
            pltpu.VMEM((rows, D_MODEL), BF16),
            pltpu.VMEM((rows, D_MODEL), F32),
            pltpu.VMEM((rows, D_MODEL), BF16),
            pltpu.SemaphoreType.DMA((MOE_DEPTH,)),
            pltpu.SemaphoreType.DMA((MOE_DEPTH,)),
        ],
    )
    return pl.pallas_call(
        functools.partial(_moe_kernel, n_tokens),
        grid_spec=grid_spec,
        out_shape=jax.ShapeDtypeStruct((n_tokens * TOP_K * ROW_TILE, LANES), F32),
        compiler_params=pltpu.CompilerParams(dimension_semantics=("arbitrary",), vmem_limit_bytes=VMEM_LIMIT),
        name="moe_experts",
    )(block_e, block_off, block_nv, n_used, sorted_a, h2, wg, bg, wu, bu, wd, bd)


ASSIGN_BITS = 17


def _moe_routing(top_i):
    rows = MOE_ROWS
    n_assign = top_i.shape[0] * TOP_K
    assert n_assign <= 1 << ASSIGN_BITS
    n_blocks = -(-n_assign // rows) + N_EXPERTS
    flat_e = top_i.reshape(-1)
    keys = jnp.sort(flat_e * (1 << ASSIGN_BITS) + jnp.arange(n_assign, dtype=jnp.int32))
    smem_len = -(-(n_assign + rows) // 1024) * 1024
    sorted_a = jnp.pad(keys & ((1 << ASSIGN_BITS) - 1), (0, smem_len - n_assign))
    experts = jnp.arange(N_EXPERTS, dtype=jnp.int32)
    counts = jnp.sum((flat_e[:, None] == experts[None, :]).astype(jnp.int32), axis=0)
    nblk = (counts + rows - 1) // rows
    blk_end = jnp.cumsum(nblk)
    blk_start = blk_end - nblk
    row_start = jnp.cumsum(counts) - counts
    blk = jnp.arange(n_blocks, dtype=jnp.int32)
    owner = (blk[:, None] >= blk_start[None, :]) & (blk[:, None] < blk_end[None, :])
    pick = lambda v: jnp.sum(jnp.where(owner, v[None, :], 0), axis=1)
    within = (blk - pick(blk_start)) * rows
    used = blk < blk_end[-1]
    block_e = jnp.where(used, pick(experts), N_EXPERTS - 1).astype(jnp.int32)
    block_off = jnp.where(used, pick(row_start) + within, 0).astype(jnp.int32)
    block_nv = jnp.where(used, jnp.clip(pick(counts) - within, 0, rows), 0).astype(jnp.int32)
    n_used = blk_end[-1].astype(jnp.int32).reshape(1)
    return block_e, block_off, block_nv, n_used, sorted_a.astype(jnp.int32)


def _final_kernel(x1_ref, y_ref, tw_ref, p_ref, gple_ref, wpg_ref, wpp_ref, gfin_ref, out_ref):
    x = x1_ref[...]
    tw = tw_ref[...]
    rows = x.shape[0]
    for j in range(TOP_K):
        x = x + tw[:, j:j + 1] * _load_row_tiles(y_ref, (j,), rows)
    gate = _sigmoid(_dot(_rms(x, gple_ref[...]).astype(BF16), wpg_ref[...]))
    x = x + gate * _dot(p_ref[...].astype(BF16), wpp_ref[...])
    out_ref[...] = _rms(x, gfin_ref[...])


def _final(x1, y4, tw, p, gple, wpg, wpp, gfin, rows, row_offset):
    n = x1.shape[0]
    off = row_offset // rows
    assert off * rows == row_offset and n % rows == 0
    ple = p.shape[-1]
    return pl.pallas_call(
        _final_kernel,
        grid=(n // rows,),
        in_specs=[
            pl.BlockSpec((rows, D_MODEL), lambda i: (i, 0)),
            pl.BlockSpec((TOP_K, rows * ROW_TILE, LANES), lambda i: (0, i + off, 0)),
            pl.BlockSpec((rows, ROUTER_LANES), lambda i: (i, 0)),
            pl.BlockSpec((rows, ple), lambda i: (i, 0)),
            _const_spec((1, D_MODEL)),
            _const_spec(wpg.shape),
            _const_spec(wpp.shape),
            _const_spec((1, D_MODEL)),
        ],
        out_specs=pl.BlockSpec((rows, D_MODEL), lambda i: (i, 0)),
        out_shape=jax.ShapeDtypeStruct((n, D_MODEL), F32),
        compiler_params=pltpu.CompilerParams(dimension_semantics=("arbitrary",), vmem_limit_bytes=VMEM_LIMIT),
        name="final",
    )(x1, y4, tw, p, gple, wpg, wpp, gfin)


def kernel(x_prompt, x_sample, p_prompt, p_sample, state_hgrn, state_pool, g_mix, w_in, hg_lb_logits, hg_norm, w_pool,
           pool_scale, w_out, g_ffn, w_router, b_router, w_gate, b_gate, w_up, b_up, w_down, b_down, g_ple,
           w_ple_gate, w_ple_proj, g_final):
    depth = w_in.shape[0]
    assert depth == 1, "single-layer step"
    b, t, _ = x_prompt.shape
    ns = x_sample.shape[0]
    assert x_sample.shape[1] == 1
    n_prompt = b * t
    n_tok = n_prompt + ns

    row = lambda a: a.reshape(1, -1)
    win = w_in[0].astype(BF16)
    wqft = w_in[0, :, :2 * D_MODEL].T.astype(BF16)
    lbl = hg_lb_logits.astype(F32)
    wpool = w_pool[0].astype(BF16)
    wout = w_out[0].astype(BF16)
    wr = jnp.pad(w_router[0], ((0, 0), (0, ROUTER_LANES - N_EXPERTS)))
    br = jnp.pad(b_router[0], (0, ROUTER_LANES - N_EXPERTS)).reshape(1, -1)
    shared = (row(hg_norm[0]), wpool, row(pool_scale[0]), wout, row(g_ffn[0]), wr, br)

    x1_s, h2_s, ti_s, tw_s, s_s, pool_s = _mixer_sample(
        x_sample.reshape(ns, D_MODEL), row(g_mix[0]), win, wqft, lbl.T, *shared,
        state_hgrn[0], state_pool[0], start_pos=PAST_LEN)
    x1_p, h2, ti_p, tw_p, s_p, pool_p = _mixer_prompt(x_prompt, row(g_mix[0]), win, lbl, *shared, h2_s)

    top_i = jnp.concatenate([ti_p[:, :TOP_K], ti_s[:, :TOP_K]], axis=0)
    routing = _moe_routing(top_i)
    e3 = lambda a: a[0].reshape(N_EXPERTS, 1, D_MODEL)
    y4 = _moe(h2, routing, w_gate[0], e3(b_gate), w_up[0], e3(b_up), w_down[0], e3(b_down))
    y4 = y4.reshape(TOP_K, n_tok * ROW_TILE, LANES)

    wpg = w_ple_gate[0].astype(BF16)
    wpp = w_ple_proj[0].astype(BF16)
    fin = (row(g_ple[0]), wpg, wpp, row(g_final))
    y_p = _final(x1_p.reshape(n_prompt, D_MODEL), y4, tw_p, p_prompt[0].reshape(n_prompt, -1), *fin,
                 rows=FINAL_ROWS, row_offset=0)
    y_s = _final(x1_s, y4, tw_s, p_sample[0].reshape(ns, -1), *fin, rows=ns, row_offset=n_prompt)

    return (y_p.reshape(b, t, D_MODEL), y_s.reshape(ns, 1, D_MODEL), s_p[None], pool_p[None], s_s[None],
            pool_s[None])
```

```python
import functools

import jax
import jax.numpy as jnp
from jax import lax
from jax.experimental import pallas as pl
from jax.experimental.pallas import tpu as pltpu

F32 = jnp.float32
BF16 = jnp.bfloat16

D_MODEL = 1024
HEADS = 8
HEAD_DIM = 128
CHUNK = 32
POOL_WINDOWS = (2, 4, 8, 16)
POOL_GW = D_MODEL // len(POOL_WINDOWS)
POOL_BUF = 15
POOL_CARRY = 16
N_EXPERTS = 32
TOP_K = 4
ROUTER_LANES = 128
SWIGLU_LIMIT = 7.0
SWIGLU_ALPHA = 1.702
EPS = 1e-6
PAST_LEN = 16384

SEG_Q, SEG_F, SEG_I, SEG_G, SEG_U, SEG_A, SEG_B = range(7)

MIXER_ROWS = 512
EW_LANES = 256
MOE_ROWS = 256
MOE_DEPTH = 3
FINAL_ROWS = 512
VMEM_LIMIT = 56 * 1024 * 1024


def _rms(x, g):
    ms = jnp.mean(x * x, axis=-1, keepdims=True)
    return x * lax.rsqrt(ms + EPS) * g


def _sigmoid(x):
    return 1.0 / (1.0 + jnp.exp(-x))


def _dot(a, b):
    return jnp.dot(a, b, preferred_element_type=F32)


def _dot_nt(a, b):
    return lax.dot_general(a, b, (((1,), (1,)), ((), ())), preferred_element_type=F32)


def _dot_tn(a, b):
    return lax.dot_general(a, b, (((0,), (0,)), ((), ())), preferred_element_type=F32)


LANES = 128
ROW_TILE = D_MODEL // LANES


def _store_row_tiles(ref, lead, x, period=ROW_TILE, offset=0):
    rows = x.shape[0]
    for c in range(ROW_TILE):
        ref[lead + (pl.ds(offset + c, rows, stride=period), slice(None))] = x[:, c * LANES:(c + 1) * LANES]


def _load_row_tiles(ref, lead, rows, period=ROW_TILE, offset=0):
    return jnp.concatenate(
        [ref[lead + (pl.ds(offset + c, rows, stride=period), slice(None))] for c in range(ROW_TILE)], axis=-1)


def _split_bf16(x):
    hi = x.astype(BF16)
    lo = (x - hi.astype(F32)).astype(BF16)
    return hi, lo


def _forget_lower_bound(lbl):
    m = jnp.max(lbl, axis=0, keepdims=True)
    e = jnp.exp(lbl - m)
    return e[0:1] / jnp.sum(e, axis=0, keepdims=True)


def _head_norm_gate(o, g_raw, hgn):
    parts = []
    for h in range(HEADS):
        oh = o[:, h * HEAD_DIM:(h + 1) * HEAD_DIM]
        parts.append(_rms(oh, hgn))
    return jnp.concatenate(parts, axis=-1) * (g_raw * _sigmoid(g_raw))


def _route(h_hi, h_lo, wr_ref, br_ref, ti_ref, tw_ref):
    rows = h_hi.shape[0]
    w_hi, w_lo = _split_bf16(wr_ref[...])
    logits = _dot(h_hi, w_hi) + _dot(h_lo, w_hi) + _dot(h_hi, w_lo) + br_ref[...]
    lane = lax.broadcasted_iota(jnp.int32, (rows, ROUTER_LANES), 1)
    neg = jnp.float32(-jnp.inf)
    l = jnp.where(lane < N_EXPERTS, logits, neg)
    ti = jnp.zeros((rows, ROUTER_LANES), jnp.int32)
    tw = jnp.zeros((rows, ROUTER_LANES), F32)
    m0 = None
    denom = None
    es = []
    for j in range(TOP_K):
        m = jnp.max(l, axis=-1, keepdims=True)
        idx = jnp.min(jnp.where(l == m, lane, ROUTER_LANES), axis=-1, keepdims=True)
        l = jnp.where(lane == idx, neg, l)
        if j == 0:
            m0 = m
        e = jnp.exp(m - m0)
        es.append(e)
        denom = e if denom is None else denom + e
        ti = jnp.where(lane == j, idx, ti)
    for j in range(TOP_K):
        tw = jnp.where(lane == j, es[j] / denom, tw)
    ti_ref[...] = ti
    tw_ref[...] = tw


def _mixer_prompt_kernel(nt, x_ref, gmix_ref, win_ref, lbl_ref, hgn_ref, wpool_ref, pscale_ref, wout_ref, gffn_ref,
                         wr_ref, br_ref, h2s_ref, x1_ref, h2_ref, *rest):
    s = pl.program_id(0)
    n_main = pl.num_programs(0) - 1

    @pl.when(s < n_main)
    def _():
        _mixer_prompt_block(s % nt, nt - 1, x_ref, gmix_ref, win_ref, lbl_ref, hgn_ref, wpool_ref, pscale_ref,
                            wout_ref, gffn_ref, wr_ref, br_ref, x1_ref, h2_ref, *rest)

    @pl.when(s == n_main)
    def _():
        h2_ref[0:h2s_ref.shape[0], :] = h2s_ref[...]


def _mixer_prompt_block(t, last_t, x_ref, gmix_ref, win_ref, lbl_ref, hgn_ref, wpool_ref, pscale_ref, wout_ref,
                        gffn_ref, wr_ref, br_ref,
                        x1_ref, h2_ref, ti_ref, tw_ref, sfin_ref, ptail_ref,
                        st_ref, uext_ref, qe_ref, ke_ref, kd_ref, v_ref, o_ref, p_ref, h_ref,
                        za_ref, zb_ref, plast_ref):
    rows = MIXER_ROWS
    n_chunks = rows // CHUNK

    @pl.when(t == 0)
    def _():
        st_ref[...] = jnp.zeros_like(st_ref)
        uext_ref[0:POOL_CARRY, :] = jnp.zeros((POOL_CARRY, D_MODEL), F32)

    h_ref[...] = _rms(x_ref[0], gmix_ref[...]).astype(BF16)

    def proj(seg):
        return _dot(h_ref[...], win_ref[:, seg * D_MODEL:(seg + 1) * D_MODEL])

    lb = _forget_lower_bound(lbl_ref[...])
    o_ref[...] = proj(SEG_Q)
    p_ref[...] = proj(SEG_F)
    v_ref[...] = proj(SEG_I).astype(BF16)
    row_in_chunk = lax.broadcasted_iota(jnp.int32, (CHUNK, EW_LANES), 0)
    for c in range(n_chunks):
        rs = slice(c * CHUNK, (c + 1) * CHUNK)
        for l0 in range(0, D_MODEL, EW_LANES):
            ls = slice(l0, l0 + EW_LANES)
            q_raw = o_ref[rs, ls]
            q = q_raw * _sigmoid(q_raw)
            lbt = lb[:, ls]
            f = lbt + (1.0 - lbt) * _sigmoid(p_ref[rs, ls])
            k = 1.0 - f
            p = f
            s = 1
            while s < CHUNK:
                p = p * jnp.where(row_in_chunk >= s, pltpu.roll(p, s, axis=0), 1.0)
                s *= 2
            plast = p[CHUNK - 1:CHUNK, :]
            qe_ref[rs, ls] = (q * p).astype(BF16)
            ke_ref[rs, ls] = (k / p).astype(BF16)
            kd_ref[rs, ls] = (k * (plast / p)).astype(BF16)
            plast_ref[c:c + 1, ls] = plast

    causal = (lax.broadcasted_iota(jnp.int32, (CHUNK, CHUNK), 0)
              >= lax.broadcasted_iota(jnp.int32, (CHUNK, CHUNK), 1))

    for hd in range(HEADS):
        cols = slice(hd * HEAD_DIM, (hd + 1) * HEAD_DIM)
        st = st_ref[hd]
        for c in range(n_chunks):
            rs = slice(c * CHUNK, (c + 1) * CHUNK)
            qe = qe_ref[rs, cols]
            ke = ke_ref[rs, cols]
            kd = kd_ref[rs, cols]
            vv = v_ref[rs, cols]
            decay = plast_ref[c:c + 1, cols]
            scores = jnp.where(causal, _dot_nt(qe, ke), 0.0)
            o_ref[rs, cols] = _dot_nt(qe, st.astype(BF16)) + _dot(scores.astype(BF16), vv)
            st = st * decay + _dot_tn(vv, kd)
        st_ref[hd] = st

    @pl.when(t == last_t)
    def _():
        for hd in range(HEADS):
            sfin_ref[0, hd] = st_ref[hd].T

    uext_ref[POOL_CARRY:POOL_CARRY + rows, :] = proj(SEG_U)
    pos1 = t * rows + lax.broadcasted_iota(jnp.int32, (rows, 1), 0) + 1
    pooled = []
    for g, w in enumerate(POOL_WINDOWS):
        cols = slice(g * POOL_GW, (g + 1) * POOL_GW)
        sw = uext_ref[:, cols]
        s = 1
        while s < w:
            sw = sw + pltpu.roll(sw, s, axis=0)
            s *= 2
        inv_cnt = 1.0 / jnp.minimum(pos1, w).astype(F32)
        dg = sw[POOL_CARRY:, :] * inv_cnt - uext_ref[POOL_CARRY:POOL_CARRY + rows, cols]
        pooled.append(_dot(dg.astype(BF16), wpool_ref[g]))

    @pl.when(t == last_t)
    def _():
        ptail_ref[0] = uext_ref[rows + POOL_CARRY - POOL_BUF:rows + POOL_CARRY, :]

    uext_ref[0:POOL_CARRY, :] = uext_ref[rows:rows + POOL_CARRY, :]
    uext_ref[POOL_CARRY:POOL_CARRY + rows, :] = jnp.concatenate(pooled, axis=-1) * pscale_ref[...]

    p_ref[...] = proj(SEG_G)
    za_ref[...] = proj(SEG_A)
    zb_ref[...] = proj(SEG_B)
    hgn = hgn_ref[...]
    for c in range(n_chunks):
        rs = slice(c * CHUNK, (c + 1) * CHUNK)
        ys = slice(POOL_CARRY + c * CHUNK, POOL_CARRY + (c + 1) * CHUNK)
        for hd in range(HEADS):
            cols = slice(hd * HEAD_DIM, (hd + 1) * HEAD_DIM)
            g_raw = p_ref[rs, cols]
            o = _rms(o_ref[rs, cols], hgn) * (g_raw * _sigmoid(g_raw))
            merged = _sigmoid(za_ref[rs, cols]) * o + _sigmoid(zb_ref[rs, cols]) * uext_ref[ys, cols]
            h_ref[rs, cols] = merged.astype(BF16)
    x1_ref[0] = x_ref[0] + _dot(h_ref[...], wout_ref[...])
    gffn = gffn_ref[...]
    for c in range(n_chunks):
        rs = slice(c * CHUNK, (c + 1) * CHUNK)
        h2 = _rms(x1_ref[0, rs, :], gffn)
        _store_row_tiles(h2_ref, (), h2, offset=c * CHUNK * ROW_TILE)
        qe_ref[rs, :], ke_ref[rs, :] = _split_bf16(h2)
    _route(qe_ref[...], ke_ref[...], wr_ref, br_ref, ti_ref, tw_ref)


def _const_spec(shape):
    zeros = (0,) * len(shape)
    return pl.BlockSpec(shape, lambda *_: zeros, pipeline_mode=pl.Buffered(1))


def _mixer_prompt(x, gmix, win, lbl, hgn, wpool, pscale, wout, gffn, wr, br, h2_tail):
    b, t, _ = x.shape
    rows = MIXER_ROWS
    nt = t // rows
    n_main = b * nt
    main = lambda s: jnp.minimum(s, n_main - 1)
    tok_spec = pl.BlockSpec((1, rows, D_MODEL), lambda s: (main(s) // nt, main(s) % nt, 0))
    flat_spec = pl.BlockSpec((rows * ROW_TILE, LANES), lambda s: (s, 0))
    lane_spec = pl.BlockSpec((rows, ROUTER_LANES), lambda s: (main(s), 0))
    assert h2_tail.shape[0] <= rows * ROW_TILE
    return pl.pallas_call(
        functools.partial(_mixer_prompt_kernel, nt),
        grid=(n_main + 1,),
        in_specs=[
            tok_spec,
            _const_spec((1, D_MODEL)),
            _const_spec(win.shape),
            _const_spec(lbl.shape),
            _const_spec((1, HEAD_DIM)),
            _const_spec(wpool.shape),
            _const_spec((1, D_MODEL)),
            _const_spec(wout.shape),
            _const_spec((1, D_MODEL)),
            _const_spec(wr.shape),
            _const_spec(br.shape),
            _const_spec(h2_tail.shape),
        ],
        out_specs=[
            tok_spec,
            flat_spec,
            lane_spec,
            lane_spec,
            pl.BlockSpec((1, HEADS, HEAD_DIM, HEAD_DIM), lambda s: (main(s) // nt, 0, 0, 0)),
            pl.BlockSpec((1, POOL_BUF, D_MODEL), lambda s: (main(s) // nt, 0, 0)),
        ],
        out_shape=[
            jax.ShapeDtypeStruct(x.shape, F32),
            jax.ShapeDtypeStruct((b * t * ROW_TILE + h2_tail.shape[0], LANES), F32),
            jax.ShapeDtypeStruct((b * t, ROUTER_LANES), jnp.int32),
            jax.ShapeDtypeStruct((b * t, ROUTER_LANES), F32),
            jax.ShapeDtypeStruct((b, HEADS, HEAD_DIM, HEAD_DIM), F32),
            jax.ShapeDtypeStruct((b, POOL_BUF, D_MODEL), F32),
        ],
        scratch_shapes=[
            pltpu.VMEM((HEADS, HEAD_DIM, HEAD_DIM), F32),
            pltpu.VMEM((rows + POOL_CARRY, D_MODEL), F32),
            pltpu.VMEM((rows, D_MODEL), BF16),
            pltpu.VMEM((rows, D_MODEL), BF16),
            pltpu.VMEM((rows, D_MODEL), BF16),
            pltpu.VMEM((rows, D_MODEL), BF16),
            pltpu.VMEM((rows, D_MODEL), F32),
            pltpu.VMEM((rows, D_MODEL), F32),
            pltpu.VMEM((rows, D_MODEL), BF16),
            pltpu.VMEM((rows, D_MODEL), F32),
            pltpu.VMEM((rows, D_MODEL), F32),
            pltpu.VMEM((rows // CHUNK, D_MODEL), F32),
        ],
        compiler_params=pltpu.CompilerParams(dimension_semantics=("arbitrary",), vmem_limit_bytes=VMEM_LIMIT),
        name="mixer_prompt",
    )(x, gmix, win, lbl, hgn, wpool, pscale, wout, gffn, wr, br, h2_tail)


SAMPLE_GROUP = 8


def _mixer_sample_kernel(pool_cnt, x_ref, gmix_ref, win_ref, wqft_ref, lblt_ref, hgn_ref, wpool_ref, pscale_ref,
                         wout_ref, gffn_ref, wr_ref, br_ref, s_ref, pbuf_ref,
                         x1_ref, h2_ref, ti_ref, tw_ref, snew_ref, pnew_ref,
                         h_ref, znat_ref, o_ref, bsum_ref):
    step = pl.program_id(0)
    last = pl.num_programs(0) - 1
    r0 = pl.multiple_of(step * SAMPLE_GROUP, SAMPLE_GROUP)

    @pl.when(step == 0)
    def _():
        hf = _rms(x_ref[...], gmix_ref[...])
        h_ref[...] = hf
        h = hf.astype(BF16)
        for i, seg in enumerate((SEG_I, SEG_G, SEG_U, SEG_A, SEG_B)):
            znat_ref[:, i * D_MODEL:(i + 1) * D_MODEL] = _dot(h, win_ref[:, seg * D_MODEL:(seg + 1) * D_MODEL])

    hg = h_ref[pl.ds(r0, SAMPLE_GROUP), :].astype(BF16)
    qft = _dot_nt(wqft_ref[...], hg)
    lbl = lblt_ref[...]
    m = jnp.max(lbl, axis=1, keepdims=True)
    e = jnp.exp(lbl - m)
    lb = e[:, 0:1] / jnp.sum(e, axis=1, keepdims=True)
    q_raw = qft[0:D_MODEL]
    qt = q_raw * _sigmoid(q_raw)
    ft = lb + (1.0 - lb) * _sigmoid(qft[D_MODEL:2 * D_MODEL])
    kt = 1.0 - ft

    for j in range(SAMPLE_GROUP):
        v_row = znat_ref[pl.ds(r0 + j, 1), 0:D_MODEL]
        o_parts = []
        for hd in range(HEADS):
            rs = slice(hd * HEAD_DIM, (hd + 1) * HEAD_DIM)
            s_new = ft[rs, j:j + 1] * s_ref[j, hd] + kt[rs, j:j + 1] * v_row[:, rs]
            snew_ref[j, hd] = s_new
            o_parts.append(jnp.sum(qt[rs, j:j + 1] * s_new, axis=0, keepdims=True))
        o_ref[pl.ds(r0 + j, 1), :] = jnp.concatenate(o_parts, axis=-1)

    u_g = znat_ref[pl.ds(r0, SAMPLE_GROUP), 2 * D_MODEL:3 * D_MODEL]
    sums = []
    for g, w in enumerate(POOL_WINDOWS):
        acc = jnp.zeros((SAMPLE_GROUP, POOL_GW), F32)
        for j in range(1, w):
            row = POOL_BUF - j
            acc = acc + pbuf_ref[:, row * D_MODEL + g * POOL_GW:row * D_MODEL + (g + 1) * POOL_GW]
        sums.append(acc)
    bsum_ref[pl.ds(r0, SAMPLE_GROUP), :] = jnp.concatenate(sums, axis=-1)
    pnew_ref[:, 0:(POOL_BUF - 1) * D_MODEL] = pbuf_ref[:, D_MODEL:POOL_BUF * D_MODEL]
    pnew_ref[:, (POOL_BUF - 1) * D_MODEL:POOL_BUF * D_MODEL] = u_g

    @pl.when(step == last)
    def _():
        x = x_ref[...]
        g_raw = znat_ref[:, D_MODEL:2 * D_MODEL]
        u = znat_ref[:, 2 * D_MODEL:3 * D_MODEL]
        o = _head_norm_gate(o_ref[...], g_raw, hgn_ref[...])
        sw = bsum_ref[...] + u
        pooled = []
        for g, w in enumerate(POOL_WINDOWS):
            cols = slice(g * POOL_GW, (g + 1) * POOL_GW)
            dg = sw[:, cols] * (1.0 / pool_cnt[g]) - u[:, cols]
            pooled.append(_dot(dg.astype(BF16), wpool_ref[g]))
        y_pool = jnp.concatenate(pooled, axis=-1) * pscale_ref[...]
        merged = (_sigmoid(znat_ref[:, 3 * D_MODEL:4 * D_MODEL]) * o
                  + _sigmoid(znat_ref[:, 4 * D_MODEL:5 * D_MODEL]) * y_pool)
        x1 = x + _dot(merged.astype(BF16), wout_ref[...])
        x1_ref[...] = x1
        h2 = _rms(x1, gffn_ref[...])
        _store_row_tiles(h2_ref, (), h2)
        _route(*_split_bf16(h2), wr_ref, br_ref, ti_ref, tw_ref)


def _mixer_sample(x, gmix, win, wqft, lblt, hgn, wpool, pscale, wout, gffn, wr, br, state, pbuf, start_pos):
    n = x.shape[0]
    steps = n // SAMPLE_GROUP
    pool_cnt = tuple(float(min(start_pos + 1, w)) for w in POOL_WINDOWS)
    pbuf2 = pbuf.reshape(n, POOL_BUF * D_MODEL)
    full = _const_spec
    out = pl.pallas_call(
        functools.partial(_mixer_sample_kernel, pool_cnt),
        grid=(steps,),
        in_specs=[
            full((n, D_MODEL)),
            full((1, D_MODEL)),
            full(win.shape),
            full(wqft.shape),
            full(lblt.shape),
            full((1, HEAD_DIM)),
            full(wpool.shape),
            full((1, D_MODEL)),
            full(wout.shape),
            full((1, D_MODEL)),
            full(wr.shape),
            full(br.shape),
            pl.BlockSpec((SAMPLE_GROUP, HEADS, HEAD_DIM, HEAD_DIM), lambda i: (i, 0, 0, 0)),
            pl.BlockSpec((SAMPLE_GROUP, POOL_BUF * D_MODEL), lambda i: (i, 0)),
        ],
        out_specs=[
            pl.BlockSpec((n, D_MODEL), lambda i: (0, 0)),
            pl.BlockSpec((n * ROW_TILE, LANES), lambda i: (0, 0)),
            pl.BlockSpec((n, ROUTER_LANES), lambda i: (0, 0)),
            pl.BlockSpec((n, ROUTER_LANES), lambda i: (0, 0)),
            pl.BlockSpec((SAMPLE_GROUP, HEADS, HEAD_DIM, HEAD_DIM), lambda i: (i, 0, 0, 0)),
            pl.BlockSpec((SAMPLE_GROUP, POOL_BUF * D_MODEL), lambda i: (i, 0)),
        ],
        out_shape=[
            jax.ShapeDtypeStruct((n, D_MODEL), F32),
            jax.ShapeDtypeStruct((n * ROW_TILE, LANES), F32),
            jax.ShapeDtypeStruct((n, ROUTER_LANES), jnp.int32),
            jax.ShapeDtypeStruct((n, ROUTER_LANES), F32),
            jax.ShapeDtypeStruct(state.shape, F32),
            jax.ShapeDtypeStruct(pbuf2.shape, F32),
        ],
        scratch_shapes=[
            pltpu.VMEM((n, D_MODEL), F32),
            pltpu.VMEM((n, 5 * D_MODEL), F32),
            pltpu.VMEM((n, D_MODEL), F32),
            pltpu.VMEM((n, D_MODEL), F32),
        ],
        compiler_params=pltpu.CompilerParams(dimension_semantics=("arbitrary",), vmem_limit_bytes=VMEM_LIMIT),
        name="mixer_sample",
    )(x, gmix, win, wqft, lblt, hgn, wpool, pscale, wout, gffn, wr, br, state, pbuf2)
    x1, h2, ti, tw, snew, pnew = out
    return x1, h2, ti, tw, snew, pnew.reshape(n, POOL_BUF, D_MODEL)


def _moe_kernel(n_tokens, be_ref, off_ref, nv_ref, nu_ref, sa_ref, h2_hbm,
                wg_ref, bg_ref, wu_ref, bu_ref, wd_ref, bd_ref, y_hbm,
                xbuf0, xbuf1, xbuf2, ybuf0, ybuf1, ybuf2, wgb, wub, wdb, xb_ref, gate_ref, act_ref, gsem, ssem):
    rows = MOE_ROWS
    i = pl.program_id(0)
    n_used = nu_ref[0]
    n_blocks = pl.num_programs(0)
    nv_back = lambda k: jnp.where(i >= k, nv_ref[jnp.maximum(i - k, 0)], 0)
    prev = jnp.maximum(i - 1, 0)
    nv_prev = nv_back(1)
    xbuf = (xbuf0, xbuf1, xbuf2)
    ybuf = (ybuf0, ybuf1, ybuf2)

    def tile_rows(first, n=1):
        start = first * ROW_TILE
        return pl.ds(start if isinstance(first, int) else pl.multiple_of(start, ROW_TILE), n * ROW_TILE)

    def gather_copy(a, r, s):
        tok = a >> 2
        return pltpu.make_async_copy(h2_hbm.at[tile_rows(tok), :], xbuf[s].at[tile_rows(r), :], gsem.at[s])

    def scatter_copy(a, r, n, s):
        dst = (a & (TOP_K - 1)) * n_tokens + (a >> 2)
        return pltpu.make_async_copy(ybuf[s].at[tile_rows(r, n), :], y_hbm.at[tile_rows(dst, n), :], ssem.at[s])

    def wait_gather(s):
        pltpu.make_async_copy(h2_hbm.at[tile_rows(0, rows), :], xbuf[s], gsem.at[s]).wait()

    def wait_scatter(n, s):
        size = rows
        while size >= 1:
            @pl.when((n & size) != 0)
            def _():
                scatter_copy(0, 0, size, s).wait()
            size //= 2

    def scatter_loop(blk, n, s):
        def body(r, c):
            scatter_copy(sa_ref[off_ref[blk] + r], r, 1, s).start()
            return c
        lax.fori_loop(0, n, body, 0)

    n_phases = 3

    def start_copies(phase, cur, full_prev):
        ahead, behind = (cur + 2) % MOE_DEPTH, (cur + MOE_DEPTH - 1) % MOE_DEPTH
        off_ahead = off_ref[jnp.minimum(i + 2, n_blocks - 1)]
        off_prev = off_ref[prev]
        for r in range(rows * phase // n_phases, rows * (phase + 1) // n_phases):
            gather_copy(sa_ref[off_ahead + r], r, ahead).start(priority=1)
            if full_prev:
                scatter_copy(sa_ref[off_prev + r], r, 1, behind).start(priority=0)

    def phase_load(cur, full_prev):
        wait_gather(cur)
        xb_ref[...] = _load_row_tiles(xbuf[cur], (), rows).astype(BF16)

    def phase_gate(cur, full_prev):
        start_copies(0, cur, full_prev)
        gate_ref[...] = jnp.minimum(_dot(xb_ref[...], wgb[...]) + bg_ref[0], SWIGLU_LIMIT)

    def phase_up(cur, full_prev):
        start_copies(1, cur, full_prev)
        up = jnp.clip(_dot(xb_ref[...], wub[...]) + bu_ref[0], -SWIGLU_LIMIT, SWIGLU_LIMIT)
        gate = gate_ref[...]
        act_ref[...] = ((up + 1.0) * gate * _sigmoid(SWIGLU_ALPHA * gate)).astype(BF16)

    def phase_down(cur, full_prev):
        nxt, behind = (cur + 1) % MOE_DEPTH, (cur + MOE_DEPTH - 1) % MOE_DEPTH
        start_copies(2, cur, full_prev)
        wait_scatter(nv_back(MOE_DEPTH), cur)
        _store_row_tiles(ybuf[cur], (), _dot(act_ref[...], wdb[...]) + bd_ref[0])
        if not full_prev:
            scatter_loop(prev, nv_prev, behind)

        @pl.when(i == n_used - 1)
        def _():
            scatter_loop(i, nv_ref[i], cur)
            wait_scatter(nv_back(2), nxt)
            wait_scatter(nv_prev, behind)
            wait_scatter(nv_ref[i], cur)
            wait_gather(nxt)
            wait_gather((cur + 2) % MOE_DEPTH)

    @pl.when(i == 0)
    def _():
        for blk in range(MOE_DEPTH - 1):
            def body(r, c):
                gather_copy(sa_ref[off_ref[blk] + r], r, blk).start()
                return c
            lax.fori_loop(0, rows, body, 0)

    @pl.when(i < n_used)
    def _():
        first_of_expert = jnp.logical_or(i == 0, be_ref[i] != be_ref[prev])

        @pl.when(first_of_expert)
        def _():
            wgb[...] = wg_ref[0].astype(BF16)
            wub[...] = wu_ref[0].astype(BF16)
            wdb[...] = wd_ref[0].astype(BF16)

        prev_is_full = nv_prev == rows
        for phase in (phase_load, phase_gate, phase_up, phase_down):
            for cur in range(MOE_DEPTH):
                for full_prev in (True, False):
                    full_cond = prev_is_full if full_prev else jnp.logical_not(prev_is_full)

                    @pl.when(jnp.logical_and(i % MOE_DEPTH == cur, full_cond))
                    def _():
                        phase(cur, full_prev)


def _moe(h2, routing, wg, bg, wu, bu, wd, bd):
    rows = MOE_ROWS
    n_tokens = h2.shape[0] // ROW_TILE
    block_e, block_off, block_nv, n_used, sorted_a = routing
    n_blocks = block_e.shape[0]
    w_spec = pl.BlockSpec((1, D_MODEL, D_MODEL), lambda i, be, *_: (be[i], 0, 0))
    b_spec = pl.BlockSpec((1, 1, D_MODEL), lambda i, be, *_: (be[i], 0, 0))
    grid_spec = pltpu.PrefetchScalarGridSpec(
        num_scalar_prefetch=5,
        grid=(n_blocks,),
        in_specs=[pl.BlockSpec(memory_space=pl.ANY), w_spec, b_spec, w_spec, b_spec, w_spec, b_spec],
        out_specs=pl.BlockSpec(memory_space=pl.ANY),
        scratch_shapes=[
            *[pltpu.VMEM((rows * ROW_TILE, LANES), F32)] * (2 * MOE_DEPTH),
            pltpu.VMEM((D_MODEL, D_MODEL), BF16),
            pltpu.VMEM((D_MODEL, D_MODEL), BF16),
            pltpu.VMEM((D_MODEL, D_MODEL), BF16),
            pltpu.VMEM((rows, D_MODEL), BF16),
            pltpu.VMEM((rows, D_MODEL), F32),
            pltpu.VMEM((rows, D_MODEL), BF16),
            pltpu.SemaphoreType.DMA((MOE_DEPTH,)),
            pltpu.SemaphoreType.DMA((MOE_DEPTH,)),
        ],
    )
    return pl.pallas_call(
        functools.partial(_moe_kernel, n_tokens),
        grid_spec=grid_spec,
        out_shape=jax.ShapeDtypeStruct((n_tokens * TOP_K * ROW_TILE, LANES), F32),
        compiler_params=pltpu.CompilerParams(dimension_semantics=("arbitrary",), vmem_limit_bytes=VMEM_LIMIT),
        name="moe_experts",
    )(block_e, block_off, block_nv, n_used, sorted_a, h2, wg, bg, wu, bu, wd, bd)


ASSIGN_BITS = 17


def _moe_routing(top_i):
    rows = MOE_ROWS
    n_assign = top_i.shape[0] * TOP_K
    assert n_assign <= 1 << ASSIGN_BITS
    n_blocks = -(-n_assign // rows) + N_EXPERTS
    flat_e = top_i.reshape(-1)
    keys = jnp.sort(flat_e * (1 << ASSIGN_BITS) + jnp.arange(n_assign, dtype=jnp.int32))
    smem_len = -(-(n_assign + rows) // 1024) * 1024
    sorted_a = jnp.pad(keys & ((1 << ASSIGN_BITS) - 1), (0, smem_len - n_assign))
    experts = jnp.arange(N_EXPERTS, dtype=jnp.int32)
    counts = jnp.sum((flat_e[:, None] == experts[None, :]).astype(jnp.int32), axis=0)
    nblk = (counts + rows - 1) // rows
    blk_end = jnp.cumsum(nblk)
    blk_start = blk_end - nblk
    row_start = jnp.cumsum(counts) - counts
    blk = jnp.arange(n_blocks, dtype=jnp.int32)
    owner = (blk[:, None] >= blk_start[None, :]) & (blk[:, None] < blk_end[None, :])
    pick = lambda v: jnp.sum(jnp.where(owner, v[None, :], 0), axis=1)
    within = (blk - pick(blk_start)) * rows
    used = blk < blk_end[-1]
    block_e = jnp.where(used, pick(experts), N_EXPERTS - 1).astype(jnp.int32)
    block_off = jnp.where(used, pick(row_start) + within, 0).astype(jnp.int32)
    block_nv = jnp.where(used, jnp.clip(pick(counts) - within, 0, rows), 0).astype(jnp.int32)
    n_used = blk_end[-1].astype(jnp.int32).reshape(1)
    return block_e, block_off, block_nv, n_used, sorted_a.astype(jnp.int32)


def _final_kernel(x1_ref, y_ref, tw_ref, p_ref, gple_ref, wpg_ref, wpp_ref, gfin_ref, out_ref):
    x = x1_ref[...]
    tw = tw_ref[...]
    rows = x.shape[0]
    for j in range(TOP_K):
        x = x + tw[:, j:j + 1] * _load_row_tiles(y_ref, (j,), rows)
    gate = _sigmoid(_dot(_rms(x, gple_ref[...]).astype(BF16), wpg_ref[...]))
    x = x + gate * _dot(p_ref[...].astype(BF16), wpp_ref[...])
    out_ref[...] = _rms(x, gfin_ref[...])


def _final(x1, y4, tw, p, gple, wpg, wpp, gfin, rows, row_offset):
    n = x1.shape[0]
    off = row_offset // rows
    assert off * rows == row_offset and n % rows == 0
    ple = p.shape[-1]
    return pl.pallas_call(
        _final_kernel,
        grid=(n // rows,),
        in_specs=[
            pl.BlockSpec((rows, D_MODEL), lambda i: (i, 0)),
            pl.BlockSpec((TOP_K, rows * ROW_TILE, LANES), lambda i: (0, i + off, 0)),
            pl.BlockSpec((rows, ROUTER_LANES), lambda i: (i, 0)),
            pl.BlockSpec((rows, ple), lambda i: (i, 0)),
            _const_spec((1, D_MODEL)),
            _const_spec(wpg.shape),
            _const_spec(wpp.shape),
            _const_spec((1, D_MODEL)),
        ],
        out_specs=pl.BlockSpec((rows, D_MODEL), lambda i: (i, 0)),
        out_shape=jax.ShapeDtypeStruct((n, D_MODEL), F32),
        compiler_params=pltpu.CompilerParams(dimension_semantics=("arbitrary",), vmem_limit_bytes=VMEM_LIMIT),
        name="final",
    )(x1, y4, tw, p, gple, wpg, wpp, gfin)


def kernel(x_prompt, x_sample, p_prompt, p_sample, state_hgrn, state_pool, g_mix, w_in, hg_lb_logits, hg_norm, w_pool,
           pool_scale, w_out, g_ffn, w_router, b_router, w_gate, b_gate, w_up, b_up, w_down, b_down, g_ple,
           w_ple_gate, w_ple_proj, g_final):
    depth = w_in.shape[0]
    assert depth == 1, "single-layer step"
    b, t, _ = x_prompt.shape
    ns = x_sample.shape[0]
    assert x_sample.shape[1] == 1
    n_prompt = b * t
    n_tok = n_prompt + ns

    row = lambda a: a.reshape(1, -1)
    win = w_in[0].astype(BF16)
    wqft = w_in[0, :, :2 * D_MODEL].T.astype(BF16)
    lbl = hg_lb_logits.astype(F32)
    wpool = w_pool[0].astype(BF16)
    wout = w_out[0].astype(BF16)
    wr = jnp.pad(w_router[0], ((0, 0), (0, ROUTER_LANES - N_EXPERTS)))
    br = jnp.pad(b_router[0], (0, ROUTER_LANES - N_EXPERTS)).reshape(1, -1)
    shared = (row(hg_norm[0]), wpool, row(pool_scale[0]), wout, row(g_ffn[0]), wr, br)

    x1_s, h2_s, ti_s, tw_s, s_s, pool_s = _mixer_sample(
        x_sample.reshape(ns, D_MODEL), row(g_mix[0]), win, wqft, lbl.T, *shared,
        state_hgrn[0], state_pool[0], start_pos=PAST_LEN)
    x1_p, h2, ti_p, tw_p, s_p, pool_p = _mixer_prompt(x_prompt, row(g_mix[0]), win, lbl, *shared, h2_s)

    top_i = jnp.concatenate([ti_p[:, :TOP_K], ti_s[:, :TOP_K]], axis=0)
    routing = _moe_routing(top_i)
    e3 = lambda a: a[0].reshape(N_EXPERTS, 1, D_MODEL)
    y4 = _moe(h2, routing, w_gate[0], e3(b_gate), w_up[0], e3(b_up), w_down[0], e3(b_down))
    y4 = y4.reshape(TOP_K, n_tok * ROW_TILE, LANES)

    wpg = w_ple_gate[0].astype(BF16)
    wpp = w_ple_proj[0].astype(BF16)
    fin = (row(g_ple[0]), wpg, wpp, row(g_final))
    y_p = _final(x1_p.reshape(n_prompt, D_MODEL), y4, tw_p, p_prompt[0].reshape(n_prompt, -1), *fin,
                 rows=FINAL_ROWS, row_offset=0)
    y_s = _final(x1_s, y4, tw_s, p_sample[0].reshape(ns, -1), *fin, rows=ns, row_offset=n_prompt)

    return (y_p.reshape(b, t, D_MODEL), y_s.reshape(ns, 1, D_MODEL), s_p[None], pool_p[None], s_s[None],
            pool_s[None])
```

```python
import functools

import jax
import jax.numpy as jnp
from jax import lax
from jax.experimental import pallas as pl
from jax.experimental.pallas import tpu as pltpu

F32 = jnp.float32
BF16 = jnp.bfloat16

D_MODEL = 1024
HEADS = 8
HEAD_DIM = 128
CHUNK = 32
POOL_WINDOWS = (2, 4, 8, 16)
POOL_GW = D_MODEL // len(POOL_WINDOWS)
POOL_BUF = 15
POOL_CARRY = 16
N_EXPERTS = 32
TOP_K = 4
ROUTER_LANES = 128
SWIGLU_LIMIT = 7.0
SWIGLU_ALPHA = 1.702
EPS = 1e-6
PAST_LEN = 16384

SEG_Q, SEG_F, SEG_I, SEG_G, SEG_U, SEG_A, SEG_B = range(7)

MIXER_ROWS = 256
EW_LANES = 256
MOE_ROWS = 256
MOE_DEPTH = 3
VMEM_LIMIT = 56 * 1024 * 1024


def _rms(x, g):
    ms = jnp.mean(x * x, axis=-1, keepdims=True)
    return x * lax.rsqrt(ms + EPS) * g


def _sigmoid(x):
    return 1.0 / (1.0 + jnp.exp(-x))


def _dot(a, b):
    return jnp.dot(a, b, preferred_element_type=F32)


def _dot_nt(a, b):
    return lax.dot_general(a, b, (((1,), (1,)), ((), ())), preferred_element_type=F32)


def _dot_tn(a, b):
    return lax.dot_general(a, b, (((0,), (0,)), ((), ())), preferred_element_type=F32)


LANES = 128
ROW_TILE = D_MODEL // LANES


def _store_row_tiles(ref, lead, x, period=ROW_TILE, offset=0):
    rows = x.shape[0]
    for c in range(ROW_TILE):
        ref[lead + (pl.ds(offset + c, rows, stride=period), slice(None))] = x[:, c * LANES:(c + 1) * LANES]


def _load_row_tiles(ref, lead, rows, period=ROW_TILE, offset=0):
    return jnp.concatenate(
        [ref[lead + (pl.ds(offset + c, rows, stride=period), slice(None))] for c in range(ROW_TILE)], axis=-1)


def _split_bf16(x):
    hi = x.astype(BF16)
    lo = (x - hi.astype(F32)).astype(BF16)
    return hi, lo


def _forget_lower_bound(lbl):
    m = jnp.max(lbl, axis=0, keepdims=True)
    e = jnp.exp(lbl - m)
    return e[0:1] / jnp.sum(e, axis=0, keepdims=True)


def _head_norm_gate(o, g_raw, hgn):
    parts = []
    for h in range(HEADS):
        oh = o[:, h * HEAD_DIM:(h + 1) * HEAD_DIM]
        parts.append(_rms(oh, hgn))
    return jnp.concatenate(parts, axis=-1) * (g_raw * _sigmoid(g_raw))


def _route(h_hi, h_lo, wr_ref, br_ref, ti_ref, tw_ref):
    rows = h_hi.shape[0]
    w_hi, w_lo = _split_bf16(wr_ref[...])
    logits = _dot(h_hi, w_hi) + _dot(h_lo, w_hi) + _dot(h_hi, w_lo) + br_ref[...]
    lane = lax.broadcasted_iota(jnp.int32, (rows, ROUTER_LANES), 1)
    neg = jnp.float32(-jnp.inf)
    l = jnp.where(lane < N_EXPERTS, logits, neg)
    ti = jnp.zeros((rows, ROUTER_LANES), jnp.int32)
    tw = jnp.zeros((rows, ROUTER_LANES), F32)
    m0 = None
    denom = None
    es = []
    for j in range(TOP_K):
        m = jnp.max(l, axis=-1, keepdims=True)
        idx = jnp.min(jnp.where(l == m, lane, ROUTER_LANES), axis=-1, keepdims=True)
        l = jnp.where(lane == idx, neg, l)
        if j == 0:
            m0 = m
        e = jnp.exp(m - m0)
        es.append(e)
        denom = e if denom is None else denom + e
        ti = jnp.where(lane == j, idx, ti)
    for j in range(TOP_K):
        tw = jnp.where(lane == j, es[j] / denom, tw)
    ti_ref[...] = ti
    tw_ref[...] = tw


MIXER_CHAINS = 2


def _mixer_prompt_kernel(nt, x_ref, gmix_ref, win_ref, lbl_ref, hgn_ref, wpool_ref, pscale_ref, wout_ref, gffn_ref,
                         wr_ref, br_ref, h2s_ref, x1_ref, h2_ref, ti_ref, tw_ref, sfin_ref, ptail_ref, *scratch):
    s = pl.program_id(0)
    n_main = pl.num_programs(0) - 1
    rows = MIXER_ROWS
    weights = (gmix_ref, win_ref, lbl_ref, hgn_ref, wpool_ref, pscale_ref, wout_ref, gffn_ref, wr_ref, br_ref)

    @pl.when(s < n_main)
    def _():
        chains = []
        for k in range(MIXER_CHAINS):
            tok = pl.ds(k * rows, rows)
            chains.append(_mixer_chain_phases(
                s % nt, nt - 1, k * rows * ROW_TILE, x_ref.at[k], *weights,
                x1_ref.at[k], h2_ref, ti_ref.at[tok], tw_ref.at[tok], sfin_ref.at[k], ptail_ref.at[k],
                *[r.at[k] for r in scratch]))
        n_phases = len(chains[0])
        for i in range(n_phases + MIXER_CHAINS - 1):
            for k in range(MIXER_CHAINS):
                if 0 <= i - k < n_phases:
                    chains[k][i - k]()

    @pl.when(s == n_main)
    def _():
        h2_ref[0:h2s_ref.shape[0], :] = h2s_ref[...]


def _mixer_chain_phases(t, last_t, h2_off, x_ref, gmix_ref, win_ref, lbl_ref, hgn_ref, wpool_ref, pscale_ref,
                        wout_ref, gffn_ref, wr_ref, br_ref,
                        x1_ref, h2_ref, ti_ref, tw_ref, sfin_ref, ptail_ref,
                        st_ref, uext_ref, qe_ref, ke_ref, kd_ref, v_ref, o_ref, p_ref, h_ref,
                        za_ref, zb_ref, plast_ref):
    rows = MIXER_ROWS
    n_chunks = rows // CHUNK

    def proj(seg):
        return _dot(h_ref[...], win_ref[:, seg * D_MODEL:(seg + 1) * D_MODEL])

    def phase_project():
        @pl.when(t == 0)
        def _():
            st_ref[...] = jnp.zeros_like(st_ref)
            uext_ref[0:POOL_CARRY, :] = jnp.zeros((POOL_CARRY, D_MODEL), F32)

        h_ref[...] = _rms(x_ref[...], gmix_ref[...]).astype(BF16)
        o_ref[...] = proj(SEG_Q)
        p_ref[...] = proj(SEG_F)
        v_ref[...] = proj(SEG_I).astype(BF16)

    def phase_hgrn_inputs():
        lb = _forget_lower_bound(lbl_ref[...])
        row_in_chunk = lax.broadcasted_iota(jnp.int32, (CHUNK, EW_LANES), 0)
        for c in range(n_chunks):
            rs = slice(c * CHUNK, (c + 1) * CHUNK)
            for l0 in range(0, D_MODEL, EW_LANES):
                ls = slice(l0, l0 + EW_LANES)
                q_raw = o_ref[rs, ls]
                q = q_raw * _sigmoid(q_raw)
                lbt = lb[:, ls]
                f = lbt + (1.0 - lbt) * _sigmoid(p_ref[rs, ls])
                k = 1.0 - f
                p = f
                s = 1
                while s < CHUNK:
                    p = p * jnp.where(row_in_chunk >= s, pltpu.roll(p, s, axis=0), 1.0)
                    s *= 2
                plast = p[CHUNK - 1:CHUNK, :]
                qe_ref[rs, ls] = (q * p).astype(BF16)
                ke_ref[rs, ls] = (k / p).astype(BF16)
                kd_ref[rs, ls] = (k * (plast / p)).astype(BF16)
                plast_ref[c:c + 1, ls] = plast

    def phase_hgrn():
        causal = (lax.broadcasted_iota(jnp.int32, (CHUNK, CHUNK), 0)
                  >= lax.broadcasted_iota(jnp.int32, (CHUNK, CHUNK), 1))
        for hd in range(HEADS):
            cols = slice(hd * HEAD_DIM, (hd + 1) * HEAD_DIM)
            st = st_ref[hd]
            for c in range(n_chunks):
                rs = slice(c * CHUNK, (c + 1) * CHUNK)
                qe = qe_ref[rs, cols]
                ke = ke_ref[rs, cols]
                kd = kd_ref[rs, cols]
                vv = v_ref[rs, cols]
                decay = plast_ref[c:c + 1, cols]
                scores = jnp.where(causal, _dot_nt(qe, ke), 0.0)
                o_ref[rs, cols] = _dot_nt(qe, st.astype(BF16)) + _dot(scores.astype(BF16), vv)
                st = st * decay + _dot_tn(vv, kd)
            st_ref[hd] = st

        @pl.when(t == last_t)
        def _():
            for hd in range(HEADS):
                sfin_ref[hd] = st_ref[hd].T

    def phase_pool_and_gates():
        uext_ref[POOL_CARRY:POOL_CARRY + rows, :] = proj(SEG_U)
        pos1 = t * rows + lax.broadcasted_iota(jnp.int32, (rows, 1), 0) + 1
        pooled = []
        for g, w in enumerate(POOL_WINDOWS):
            cols = slice(g * POOL_GW, (g + 1) * POOL_GW)
            sw = uext_ref[:, cols]
            s = 1
            while s < w:
                sw = sw + pltpu.roll(sw, s, axis=0)
                s *= 2
            inv_cnt = 1.0 / jnp.minimum(pos1, w).astype(F32)
            dg = sw[POOL_CARRY:, :] * inv_cnt - uext_ref[POOL_CARRY:POOL_CARRY + rows, cols]
            pooled.append(_dot(dg.astype(BF16), wpool_ref[g]))

        @pl.when(t == last_t)
        def _():
            ptail_ref[...] = uext_ref[rows + POOL_CARRY - POOL_BUF:rows + POOL_CARRY, :]

        uext_ref[0:POOL_CARRY, :] = uext_ref[rows:rows + POOL_CARRY, :]
        uext_ref[POOL_CARRY:POOL_CARRY + rows, :] = jnp.concatenate(pooled, axis=-1) * pscale_ref[...]
        p_ref[...] = proj(SEG_G)
        za_ref[...] = proj(SEG_A)
        zb_ref[...] = proj(SEG_B)

    def phase_merge():
        hgn = hgn_ref[...]
        for c in range(n_chunks):
            rs = slice(c * CHUNK, (c + 1) * CHUNK)
            ys = slice(POOL_CARRY + c * CHUNK, POOL_CARRY + (c + 1) * CHUNK)
            for hd in range(HEADS):
                cols = slice(hd * HEAD_DIM, (hd + 1) * HEAD_DIM)
                g_raw = p_ref[rs, cols]
                o = _rms(o_ref[rs, cols], hgn) * (g_raw * _sigmoid(g_raw))
                merged = _sigmoid(za_ref[rs, cols]) * o + _sigmoid(zb_ref[rs, cols]) * uext_ref[ys, cols]
                h_ref[rs, cols] = merged.astype(BF16)

    def phase_out_and_route():
        x1_ref[...] = x_ref[...] + _dot(h_ref[...], wout_ref[...])
        gffn = gffn_ref[...]
        for c in range(n_chunks):
            rs = slice(c * CHUNK, (c + 1) * CHUNK)
            h2 = _rms(x1_ref[rs, :], gffn)
            _store_row_tiles(h2_ref, (), h2, offset=h2_off + c * CHUNK * ROW_TILE)
            qe_ref[rs, :], ke_ref[rs, :] = _split_bf16(h2)
        _route(qe_ref[...], ke_ref[...], wr_ref, br_ref, ti_ref, tw_ref)

    return [phase_project, phase_hgrn_inputs, phase_hgrn, phase_pool_and_gates, phase_merge, phase_out_and_route]


def _const_spec(shape):
    zeros = (0,) * len(shape)
    return pl.BlockSpec(shape, lambda *_: zeros, pipeline_mode=pl.Buffered(1))


def _mixer_prompt(x, gmix, win, lbl, hgn, wpool, pscale, wout, gffn, wr, br, h2_tail):
    b, t, _ = x.shape
    rows, nc = MIXER_ROWS, MIXER_CHAINS
    nt = t // rows
    n_main = (b // nc) * nt
    assert b % nc == 0 and t % rows == 0 and h2_tail.shape[0] <= nc * rows * ROW_TILE
    main = lambda s: jnp.minimum(s, n_main - 1)
    tok_spec = pl.BlockSpec((nc, rows, D_MODEL), lambda s: (main(s) // nt, main(s) % nt, 0))
    flat_spec = pl.BlockSpec((nc * rows * ROW_TILE, LANES), lambda s: (s, 0))
    lane_spec = pl.BlockSpec((nc * rows, ROUTER_LANES), lambda s: (main(s), 0))
    chain = lambda shape, dtype: pltpu.VMEM((nc,) + shape, dtype)
    return pl.pallas_call(
        functools.partial(_mixer_prompt_kernel, nt),
        grid=(n_main + 1,),
        in_specs=[
            tok_spec,
            _const_spec((1, D_MODEL)),
            _const_spec(win.shape),
            _const_spec(lbl.shape),
            _const_spec((1, HEAD_DIM)),
            _const_spec(wpool.shape),
            _const_spec((1, D_MODEL)),
            _const_spec(wout.shape),
            _const_spec((1, D_MODEL)),
            _const_spec(wr.shape),
            _const_spec(br.shape),
            _const_spec(h2_tail.shape),
        ],
        out_specs=[
            tok_spec,
            flat_spec,
            lane_spec,
            lane_spec,
            pl.BlockSpec((nc, HEADS, HEAD_DIM, HEAD_DIM), lambda s: (main(s) // nt, 0, 0, 0)),
            pl.BlockSpec((nc, POOL_BUF, D_MODEL), lambda s: (main(s) // nt, 0, 0)),
        ],
        out_shape=[
            jax.ShapeDtypeStruct(x.shape, F32),
            jax.ShapeDtypeStruct((b * t * ROW_TILE + h2_tail.shape[0], LANES), F32),
            jax.ShapeDtypeStruct((b * t, ROUTER_LANES), jnp.int32),
            jax.ShapeDtypeStruct((b * t, ROUTER_LANES), F32),
            jax.ShapeDtypeStruct((b, HEADS, HEAD_DIM, HEAD_DIM), F32),
            jax.ShapeDtypeStruct((b, POOL_BUF, D_MODEL), F32),
        ],
        scratch_shapes=[
            chain((HEADS, HEAD_DIM, HEAD_DIM), F32),
            chain((rows + POOL_CARRY, D_MODEL), F32),
            chain((rows, D_MODEL), BF16),
            chain((rows, D_MODEL), BF16),
            chain((rows, D_MODEL), BF16),
            chain((rows, D_MODEL), BF16),
            chain((rows, D_MODEL), F32),
            chain((rows, D_MODEL), F32),
            chain((rows, D_MODEL), BF16),
            chain((rows, D_MODEL), F32),
            chain((rows, D_MODEL), F32),
            chain((rows // CHUNK, D_MODEL), F32),
        ],
        compiler_params=pltpu.CompilerParams(dimension_semantics=("arbitrary",), vmem_limit_bytes=VMEM_LIMIT),
        name="mixer_prompt",
    )(x, gmix, win, lbl, hgn, wpool, pscale, wout, gffn, wr, br, h2_tail)


SAMPLE_GROUP = 8


def _mixer_sample_kernel(pool_cnt, x_ref, gmix_ref, win_ref, wqft_ref, lblt_ref, hgn_ref, wpool_ref, pscale_ref,
                         wout_ref, gffn_ref, wr_ref, br_ref, s_ref, pbuf_ref,
                         x1_ref, h2_ref, ti_ref, tw_ref, snew_ref, pnew_ref,
                         h_ref, znat_ref, o_ref, bsum_ref):
    step = pl.program_id(0)
    last = pl.num_programs(0) - 1
    r0 = pl.multiple_of(step * SAMPLE_GROUP, SAMPLE_GROUP)

    @pl.when(step == 0)
    def _():
        hf = _rms(x_ref[...], gmix_ref[...])
        h_ref[...] = hf
        h = hf.astype(BF16)
        for i, seg in enumerate((SEG_I, SEG_G, SEG_U, SEG_A, SEG_B)):
            znat_ref[:, i * D_MODEL:(i + 1) * D_MODEL] = _dot(h, win_ref[:, seg * D_MODEL:(seg + 1) * D_MODEL])

    hg = h_ref[pl.ds(r0, SAMPLE_GROUP), :].astype(BF16)
    qft = _dot_nt(wqft_ref[...], hg)
    lbl = lblt_ref[...]
    m = jnp.max(lbl, axis=1, keepdims=True)
    e = jnp.exp(lbl - m)
    lb = e[:, 0:1] / jnp.sum(e, axis=1, keepdims=True)
    q_raw = qft[0:D_MODEL]
    qt = q_raw * _sigmoid(q_raw)
    ft = lb + (1.0 - lb) * _sigmoid(qft[D_MODEL:2 * D_MODEL])
    kt = 1.0 - ft

    for j in range(SAMPLE_GROUP):
        v_row = znat_ref[pl.ds(r0 + j, 1), 0:D_MODEL]
        o_parts = []
        for hd in range(HEADS):
            rs = slice(hd * HEAD_DIM, (hd + 1) * HEAD_DIM)
            s_new = ft[rs, j:j + 1] * s_ref[j, hd] + kt[rs, j:j + 1] * v_row[:, rs]
            snew_ref[j, hd] = s_new
            o_parts.append(jnp.sum(qt[rs, j:j + 1] * s_new, axis=0, keepdims=True))
        o_ref[pl.ds(r0 + j, 1), :] = jnp.concatenate(o_parts, axis=-1)

    u_g = znat_ref[pl.ds(r0, SAMPLE_GROUP), 2 * D_MODEL:3 * D_MODEL]
    sums = []
    for g, w in enumerate(POOL_WINDOWS):
        acc = jnp.zeros((SAMPLE_GROUP, POOL_GW), F32)
        for j in range(1, w):
            row = POOL_BUF - j
            acc = acc + pbuf_ref[:, row * D_MODEL + g * POOL_GW:row * D_MODEL + (g + 1) * POOL_GW]
        sums.append(acc)
    bsum_ref[pl.ds(r0, SAMPLE_GROUP), :] = jnp.concatenate(sums, axis=-1)
    pnew_ref[:, 0:(POOL_BUF - 1) * D_MODEL] = pbuf_ref[:, D_MODEL:POOL_BUF * D_MODEL]
    pnew_ref[:, (POOL_BUF - 1) * D_MODEL:POOL_BUF * D_MODEL] = u_g

    @pl.when(step == last)
    def _():
        x = x_ref[...]
        g_raw = znat_ref[:, D_MODEL:2 * D_MODEL]
        u = znat_ref[:, 2 * D_MODEL:3 * D_MODEL]
        o = _head_norm_gate(o_ref[...], g_raw, hgn_ref[...])
        sw = bsum_ref[...] + u
        pooled = []
        for g, w in enumerate(POOL_WINDOWS):
            cols = slice(g * POOL_GW, (g + 1) * POOL_GW)
            dg = sw[:, cols] * (1.0 / pool_cnt[g]) - u[:, cols]
            pooled.append(_dot(dg.astype(BF16), wpool_ref[g]))
        y_pool = jnp.concatenate(pooled, axis=-1) * pscale_ref[...]
        merged = (_sigmoid(znat_ref[:, 3 * D_MODEL:4 * D_MODEL]) * o
                  + _sigmoid(znat_ref[:, 4 * D_MODEL:5 * D_MODEL]) * y_pool)
        x1 = x + _dot(merged.astype(BF16), wout_ref[...])
        x1_ref[...] = x1
        h2 = _rms(x1, gffn_ref[...])
        _store_row_tiles(h2_ref, (), h2)
        _route(*_split_bf16(h2), wr_ref, br_ref, ti_ref, tw_ref)


def _mixer_sample(x, gmix, win, wqft, lblt, hgn, wpool, pscale, wout, gffn, wr, br, state, pbuf, start_pos):
    n = x.shape[0]
    steps = n // SAMPLE_GROUP
    pool_cnt = tuple(float(min(start_pos + 1, w)) for w in POOL_WINDOWS)
    pbuf2 = pbuf.reshape(n, POOL_BUF * D_MODEL)
    full = _const_spec
    out = pl.pallas_call(
        functools.partial(_mixer_sample_kernel, pool_cnt),
        grid=(steps,),
        in_specs=[
            full((n, D_MODEL)),
            full((1, D_MODEL)),
            full(win.shape),
            full(wqft.shape),
            full(lblt.shape),
            full((1, HEAD_DIM)),
            full(wpool.shape),
            full((1, D_MODEL)),
            full(wout.shape),
            full((1, D_MODEL)),
            full(wr.shape),
            full(br.shape),
            pl.BlockSpec((SAMPLE_GROUP, HEADS, HEAD_DIM, HEAD_DIM), lambda i: (i, 0, 0, 0)),
            pl.BlockSpec((SAMPLE_GROUP, POOL_BUF * D_MODEL), lambda i: (i, 0)),
        ],
        out_specs=[
            pl.BlockSpec((n, D_MODEL), lambda i: (0, 0)),
            pl.BlockSpec((n * ROW_TILE, LANES), lambda i: (0, 0)),
            pl.BlockSpec((n, ROUTER_LANES), lambda i: (0, 0)),
            pl.BlockSpec((n, ROUTER_LANES), lambda i: (0, 0)),
            pl.BlockSpec((SAMPLE_GROUP, HEADS, HEAD_DIM, HEAD_DIM), lambda i: (i, 0, 0, 0)),
            pl.BlockSpec((SAMPLE_GROUP, POOL_BUF * D_MODEL), lambda i: (i, 0)),
        ],
        out_shape=[
            jax.ShapeDtypeStruct((n, D_MODEL), F32),
            jax.ShapeDtypeStruct((n * ROW_TILE, LANES), F32),
            jax.ShapeDtypeStruct((n, ROUTER_LANES), jnp.int32),
            jax.ShapeDtypeStruct((n, ROUTER_LANES), F32),
            jax.ShapeDtypeStruct(state.shape, F32),
            jax.ShapeDtypeStruct(pbuf2.shape, F32),
        ],
        scratch_shapes=[
            pltpu.VMEM((n, D_MODEL), F32),
            pltpu.VMEM((n, 5 * D_MODEL), F32),
            pltpu.VMEM((n, D_MODEL), F32),
            pltpu.VMEM((n, D_MODEL), F32),
        ],
        compiler_params=pltpu.CompilerParams(dimension_semantics=("arbitrary",), vmem_limit_bytes=VMEM_LIMIT),
        name="mixer_sample",
    )(x, gmix, win, wqft, lblt, hgn, wpool, pscale, wout, gffn, wr, br, state, pbuf2)
    x1, h2, ti, tw, snew, pnew = out
    return x1, h2, ti, tw, snew, pnew.reshape(n, POOL_BUF, D_MODEL)


def _moe_kernel(n_tokens, be_ref, off_ref, nv_ref, nu_ref, sa_ref, h2_hbm,
                wg_ref, bg_ref, wu_ref, bu_ref, wd_ref, bd_ref, y_hbm,
                xbuf0, xbuf1, xbuf2, ybuf0, ybuf1, ybuf2, wgb, wub, wdb, xb_ref, gate_ref, act_ref, gsem, ssem):
    rows = MOE_ROWS
    i = pl.program_id(0)
    n_used = nu_ref[0]
    n_blocks = pl.num_programs(0)
    nv_back = lambda k: jnp.where(i >= k, nv_ref[jnp.maximum(i - k, 0)], 0)
    prev = jnp.maximum(i - 1, 0)
    nv_prev = nv_back(1)
    xbuf = (xbuf0, xbuf1, xbuf2)
    ybuf = (ybuf0, ybuf1, ybuf2)

    def tile_rows(first, n=1):
        start = first * ROW_TILE
        return pl.ds(start if isinstance(first, int) else pl.multiple_of(start, ROW_TILE), n * ROW_TILE)

    def gather_copy(a, r, s):
        tok = a >> 2
        return pltpu.make_async_copy(h2_hbm.at[tile_rows(tok), :], xbuf[s].at[tile_rows(r), :], gsem.at[s])

    def scatter_copy(a, r, n, s):
        dst = (a & (TOP_K - 1)) * n_tokens + (a >> 2)
        return pltpu.make_async_copy(ybuf[s].at[tile_rows(r, n), :], y_hbm.at[tile_rows(dst, n), :], ssem.at[s])

    def wait_gather(s):
        pltpu.make_async_copy(h2_hbm.at[tile_rows(0, rows), :], xbuf[s], gsem.at[s]).wait()

    def wait_scatter(n, s):
        size = rows
        while size >= 1:
            @pl.when((n & size) != 0)
            def _():
                scatter_copy(0, 0, size, s).wait()
            size //= 2

    def scatter_loop(blk, n, s):
        def body(r, c):
            scatter_copy(sa_ref[off_ref[blk] + r], r, 1, s).start()
            return c
        lax.fori_loop(0, n, body, 0)

    n_phases = 3

    def start_copies(phase, cur, full_prev):
        ahead, behind = (cur + 2) % MOE_DEPTH, (cur + MOE_DEPTH - 1) % MOE_DEPTH
        off_ahead = off_ref[jnp.minimum(i + 2, n_blocks - 1)]
        off_prev = off_ref[prev]
        for r in range(rows * phase // n_phases, rows * (phase + 1) // n_phases):
            gather_copy(sa_ref[off_ahead + r], r, ahead).start(priority=1)
            if full_prev:
                scatter_copy(sa_ref[off_prev + r], r, 1, behind).start(priority=0)

    def phase_load(cur, full_prev):
        wait_gather(cur)
        xb_ref[...] = _load_row_tiles(xbuf[cur], (), rows).astype(BF16)

    def phase_gate(cur, full_prev):
        start_copies(0, cur, full_prev)
        gate_ref[...] = jnp.minimum(_dot(xb_ref[...], wgb[...]) + bg_ref[0], SWIGLU_LIMIT)

    def phase_up(cur, full_prev):
        start_copies(1, cur, full_prev)
        up = jnp.clip(_dot(xb_ref[...], wub[...]) + bu_ref[0], -SWIGLU_LIMIT, SWIGLU_LIMIT)
        gate = gate_ref[...]
        act_ref[...] = ((up + 1.0) * gate * _sigmoid(SWIGLU_ALPHA * gate)).astype(BF16)

    def phase_down(cur, full_prev):
        nxt, behind = (cur + 1) % MOE_DEPTH, (cur + MOE_DEPTH - 1) % MOE_DEPTH
        start_copies(2, cur, full_prev)
        wait_scatter(nv_back(MOE_DEPTH), cur)
        _store_row_tiles(ybuf[cur], (), _dot(act_ref[...], wdb[...]) + bd_ref[0])
        if not full_prev:
            scatter_loop(prev, nv_prev, behind)

        @pl.when(i == n_used - 1)
        def _():
            scatter_loop(i, nv_ref[i], cur)
            wait_scatter(nv_back(2), nxt)
            wait_scatter(nv_prev, behind)
            wait_scatter(nv_ref[i], cur)
            wait_gather(nxt)
            wait_gather((cur + 2) % MOE_DEPTH)

    @pl.when(i == 0)
    def _():
        for blk in range(MOE_DEPTH - 1):
            def body(r, c):
                gather_copy(sa_ref[off_ref[blk] + r], r, blk).start()
                return c
            lax.fori_loop(0, rows, body, 0)

    @pl.when(i < n_used)
    def _():
        first_of_expert = jnp.logical_or(i == 0, be_ref[i] != be_ref[prev])

        @pl.when(first_of_expert)
        def _():
            wgb[...] = wg_ref[0].astype(BF16)
            wub[...] = wu_ref[0].astype(BF16)
            wdb[...] = wd_ref[0].astype(BF16)

        prev_is_full = nv_prev == rows
        for phase in (phase_load, phase_gate, phase_up, phase_down):
            for cur in range(MOE_DEPTH):
                for full_prev in (True, False):
                    full_cond = prev_is_full if full_prev else jnp.logical_not(prev_is_full)

                    @pl.when(jnp.logical_and(i % MOE_DEPTH == cur, full_cond))
                    def _():
                        phase(cur, full_prev)


def _moe(h2, routing, wg, bg, wu, bu, wd, bd):
    rows = MOE_ROWS
    n_tokens = h2.shape[0] // ROW_TILE
    block_e, block_off, block_nv, n_used, sorted_a = routing
    n_blocks = block_e.shape[0]
    w_spec = pl.BlockSpec((1, D_MODEL, D_MODEL), lambda i, be, *_: (be[i], 0, 0))
    b_spec = pl.BlockSpec((1, 1, D_MODEL), lambda i, be, *_: (be[i], 0, 0))
    grid_spec = pltpu.PrefetchScalarGridSpec(
        num_scalar_prefetch=5,
        grid=(n_blocks,),
        in_specs=[pl.BlockSpec(memory_space=pl.ANY), w_spec, b_spec, w_spec, b_spec, w_spec, b_spec],
        out_specs=pl.BlockSpec(memory_space=pl.ANY),
        scratch_shapes=[
            *[pltpu.VMEM((rows * ROW_TILE, LANES), F32)] * (2 * MOE_DEPTH),
            pltpu.VMEM((D_MODEL, D_MODEL), BF16),
            pltpu.VMEM((D_MODEL, D_MODEL), BF16),
            pltpu.VMEM((D_MODEL, D_MODEL), BF16),
            pltpu.VMEM((rows, D_MODEL), BF16),
            pltpu.VMEM((rows, D_MODEL), F32),
            pltpu.VMEM((rows, D_MODEL), BF16),
            pltpu.SemaphoreType.DMA((MOE_DEPTH,)),
            pltpu.SemaphoreType.DMA((MOE_DEPTH,)),
        ],
    )
    return pl.pallas_call(
        functools.partial(_moe_kernel, n_tokens),
        grid_spec=grid_spec,
        out_shape=jax.ShapeDtypeStruct((n_tokens * TOP_K * ROW_TILE, LANES), F32),
        compiler_params=pltpu.CompilerParams(dimension_semantics=("arbitrary",), vmem_limit_bytes=VMEM_LIMIT),
        name="moe_experts",
    )(block_e, block_off, block_nv, n_used, sorted_a, h2, wg, bg, wu, bu, wd, bd)


ASSIGN_BITS = 17


def _moe_routing(top_i):
    rows = MOE_ROWS
    n_assign = top_i.shape[0] * TOP_K
    assert n_assign <= 1 << ASSIGN_BITS
    n_blocks = -(-n_assign // rows) + N_EXPERTS
    flat_e = top_i.reshape(-1)
    keys = jnp.sort(flat_e * (1 << ASSIGN_BITS) + jnp.arange(n_assign, dtype=jnp.int32))
    smem_len = -(-(n_assign + rows) // 1024) * 1024
    sorted_a = jnp.pad(keys & ((1 << ASSIGN_BITS) - 1), (0, smem_len - n_assign))
    experts = jnp.arange(N_EXPERTS, dtype=jnp.int32)
    counts = jnp.sum((flat_e[:, None] == experts[None, :]).astype(jnp.int32), axis=0)
    nblk = (counts + rows - 1) // rows
    blk_end = jnp.cumsum(nblk)
    blk_start = blk_end - nblk
    row_start = jnp.cumsum(counts) - counts
    blk = jnp.arange(n_blocks, dtype=jnp.int32)
    owner = (blk[:, None] >= blk_start[None, :]) & (blk[:, None] < blk_end[None, :])
    pick = lambda v: jnp.sum(jnp.where(owner, v[None, :], 0), axis=1)
    within = (blk - pick(blk_start)) * rows
    used = blk < blk_end[-1]
    block_e = jnp.where(used, pick(experts), N_EXPERTS - 1).astype(jnp.int32)
    block_off = jnp.where(used, pick(row_start) + within, 0).astype(jnp.int32)
    block_nv = jnp.where(used, jnp.clip(pick(counts) - within, 0, rows), 0).astype(jnp.int32)
    n_used = blk_end[-1].astype(jnp.int32).reshape(1)
    return block_e, block_off, block_nv, n_used, sorted_a.astype(jnp.int32)


def _final_kernel(x1_ref, y_ref, tw_ref, p_ref, gple_ref, wpg_ref, wpp_ref, gfin_ref, out_ref):
    tw = tw_ref[...]
    rows = tw.shape[0]
    x = x1_ref[...].reshape(rows, D_MODEL)
    for j in range(TOP_K):
        x = x + tw[:, j:j + 1] * _load_row_tiles(y_ref, (j,), rows)
    gate = _sigmoid(_dot(_rms(x, gple_ref[...]).astype(BF16), wpg_ref[...]))
    x = x + gate * _dot(p_ref[...].reshape(rows, -1).astype(BF16), wpp_ref[...])
    out_ref[...] = _rms(x, gfin_ref[...]).reshape(out_ref.shape)


def _final(x1, y4, tw, p, gple, wpg, wpp, gfin, chains, rows, tok_offset):
    g, t, _ = x1.shape
    nt = t // rows
    flat = chains * rows
    off = tok_offset // flat
    assert off * flat == tok_offset and g % chains == 0 and t % rows == 0
    ple = p.shape[-1]
    tok_map = lambda i: (i // nt, i % nt, 0)
    return pl.pallas_call(
        _final_kernel,
        grid=((g // chains) * nt,),
        in_specs=[
            pl.BlockSpec((chains, rows, D_MODEL), tok_map),
            pl.BlockSpec((TOP_K, flat * ROW_TILE, LANES), lambda i: (0, i + off, 0)),
            pl.BlockSpec((flat, ROUTER_LANES), lambda i: (i, 0)),
            pl.BlockSpec((chains, rows, ple), tok_map),
            _const_spec((1, D_MODEL)),
            _const_spec(wpg.shape),
            _const_spec(wpp.shape),
            _const_spec((1, D_MODEL)),
        ],
        out_specs=pl.BlockSpec((chains, rows, D_MODEL), tok_map),
        out_shape=jax.ShapeDtypeStruct(x1.shape, F32),
        compiler_params=pltpu.CompilerParams(dimension_semantics=("arbitrary",), vmem_limit_bytes=VMEM_LIMIT),
        name="final",
    )(x1, y4, tw, p, gple, wpg, wpp, gfin)


def kernel(x_prompt, x_sample, p_prompt, p_sample, state_hgrn, state_pool, g_mix, w_in, hg_lb_logits, hg_norm, w_pool,
           pool_scale, w_out, g_ffn, w_router, b_router, w_gate, b_gate, w_up, b_up, w_down, b_down, g_ple,
           w_ple_gate, w_ple_proj, g_final):
    depth = w_in.shape[0]
    assert depth == 1, "single-layer step"
    b, t, _ = x_prompt.shape
    ns = x_sample.shape[0]
    assert x_sample.shape[1] == 1
    n_prompt = b * t
    n_tok = n_prompt + ns

    row = lambda a: a.reshape(1, -1)
    win = w_in[0].astype(BF16)
    wqft = w_in[0, :, :2 * D_MODEL].T.astype(BF16)
    lbl = hg_lb_logits.astype(F32)
    wpool = w_pool[0].astype(BF16)
    wout = w_out[0].astype(BF16)
    wr = jnp.pad(w_router[0], ((0, 0), (0, ROUTER_LANES - N_EXPERTS)))
    br = jnp.pad(b_router[0], (0, ROUTER_LANES - N_EXPERTS)).reshape(1, -1)
    shared = (row(hg_norm[0]), wpool, row(pool_scale[0]), wout, row(g_ffn[0]), wr, br)

    x1_s, h2_s, ti_s, tw_s, s_s, pool_s = _mixer_sample(
        x_sample.reshape(ns, D_MODEL), row(g_mix[0]), win, wqft, lbl.T, *shared,
        state_hgrn[0], state_pool[0], start_pos=PAST_LEN)
    x1_p, h2, ti_p, tw_p, s_p, pool_p = _mixer_prompt(x_prompt, row(g_mix[0]), win, lbl, *shared, h2_s)

    top_i = jnp.concatenate([ti_p[:, :TOP_K], ti_s[:, :TOP_K]], axis=0)
    routing = _moe_routing(top_i)
    e3 = lambda a: a[0].reshape(N_EXPERTS, 1, D_MODEL)
    y4 = _moe(h2, routing, w_gate[0], e3(b_gate), w_up[0], e3(b_up), w_down[0], e3(b_down))
    y4 = y4.reshape(TOP_K, n_tok * ROW_TILE, LANES)

    wpg = w_ple_gate[0].astype(BF16)
    wpp = w_ple_proj[0].astype(BF16)
    fin = (row(g_ple[0]), wpg, wpp, row(g_final))
    y_p = _final(x1_p, y4, tw_p, p_prompt[0], *fin, chains=MIXER_CHAINS, rows=MIXER_ROWS, tok_offset=0)
    y_s = _final(x1_s[None], y4, tw_s, p_sample[0].reshape(1, ns, -1), *fin, chains=1, rows=ns, tok_offset=n_prompt)

    return (y_p, y_s.reshape(ns, 1, D_MODEL), s_p[None], pool_p[None], s_s[None], pool_s[None])
```

```python
import functools

import jax
import jax.numpy as jnp
from jax import lax
from jax.experimental import pallas as pl
from jax.experimental.pallas import tpu as pltpu

F32 = jnp.float32
BF16 = jnp.bfloat16

D_MODEL = 1024
HEADS = 8
HEAD_DIM = 128
CHUNK = 32
POOL_WINDOWS = (2, 4, 8, 16)
POOL_GW = D_MODEL // len(POOL_WINDOWS)
POOL_BUF = 15
POOL_CARRY = 16
N_EXPERTS = 32
TOP_K = 4
ROUTER_LANES = 128
SWIGLU_LIMIT = 7.0
SWIGLU_ALPHA = 1.702
EPS = 1e-6
PAST_LEN = 16384

SEG_Q, SEG_F, SEG_I, SEG_G, SEG_U, SEG_A, SEG_B = range(7)

MIXER_ROWS = 512
EW_LANES = 256
MOE_ROWS = 512
MOE_DEPTH = 3
VMEM_LIMIT = 56 * 1024 * 1024


def _rms(x, g):
    ms = jnp.mean(x * x, axis=-1, keepdims=True)
    return x * lax.rsqrt(ms + EPS) * g


def _sigmoid(x):
    return 1.0 / (1.0 + jnp.exp(-x))


def _dot(a, b):
    return jnp.dot(a, b, preferred_element_type=F32)


def _dot_nt(a, b):
    return lax.dot_general(a, b, (((1,), (1,)), ((), ())), preferred_element_type=F32)


def _dot_tn(a, b):
    return lax.dot_general(a, b, (((0,), (0,)), ((), ())), preferred_element_type=F32)


LANES = 128
ROW_TILE = D_MODEL // LANES


def _store_row_tiles(ref, lead, x, period=ROW_TILE, offset=0):
    rows = x.shape[0]
    for c in range(ROW_TILE):
        ref[lead + (pl.ds(offset + c, rows, stride=period), slice(None))] = x[:, c * LANES:(c + 1) * LANES]


def _load_row_tiles(ref, lead, rows, period=ROW_TILE, offset=0):
    return jnp.concatenate(
        [ref[lead + (pl.ds(offset + c, rows, stride=period), slice(None))] for c in range(ROW_TILE)], axis=-1)


def _split_bf16(x):
    hi = x.astype(BF16)
    lo = (x - hi.astype(F32)).astype(BF16)
    return hi, lo


def _forget_lower_bound(lbl):
    m = jnp.max(lbl, axis=0, keepdims=True)
    e = jnp.exp(lbl - m)
    return e[0:1] / jnp.sum(e, axis=0, keepdims=True)


def _head_norm_gate(o, g_raw, hgn):
    parts = []
    for h in range(HEADS):
        oh = o[:, h * HEAD_DIM:(h + 1) * HEAD_DIM]
        parts.append(_rms(oh, hgn))
    return jnp.concatenate(parts, axis=-1) * (g_raw * _sigmoid(g_raw))


def _route(h_hi, h_lo, wr_ref, br_ref, ti_ref, tw_ref):
    rows = h_hi.shape[0]
    w_hi, w_lo = _split_bf16(wr_ref[...])
    logits = _dot(h_hi, w_hi) + _dot(h_lo, w_hi) + _dot(h_hi, w_lo) + br_ref[...]
    lane = lax.broadcasted_iota(jnp.int32, (rows, ROUTER_LANES), 1)
    neg = jnp.float32(-jnp.inf)
    l = jnp.where(lane < N_EXPERTS, logits, neg)
    ti = jnp.zeros((rows, ROUTER_LANES), jnp.int32)
    tw = jnp.zeros((rows, ROUTER_LANES), F32)
    m0 = None
    denom = None
    es = []
    for j in range(TOP_K):
        m = jnp.max(l, axis=-1, keepdims=True)
        idx = jnp.min(jnp.where(l == m, lane, ROUTER_LANES), axis=-1, keepdims=True)
        l = jnp.where(lane == idx, neg, l)
        if j == 0:
            m0 = m
        e = jnp.exp(m - m0)
        es.append(e)
        denom = e if denom is None else denom + e
        ti = jnp.where(lane == j, idx, ti)
    for j in range(TOP_K):
        tw = jnp.where(lane == j, es[j] / denom, tw)
    ti_ref[...] = ti
    tw_ref[...] = tw


MIXER_CHAINS = 1


def _mixer_prompt_kernel(nt, x_ref, gmix_ref, win_ref, lbl_ref, hgn_ref, wpool_ref, pscale_ref, wout_ref, gffn_ref,
                         wr_ref, br_ref, h2s_ref, x1_ref, h2_ref, ti_ref, tw_ref, sfin_ref, ptail_ref, *scratch):
    s = pl.program_id(0)
    n_main = pl.num_programs(0) - 1
    rows = MIXER_ROWS
    weights = (gmix_ref, win_ref, lbl_ref, hgn_ref, wpool_ref, pscale_ref, wout_ref, gffn_ref, wr_ref, br_ref)

    @pl.when(s < n_main)
    def _():
        chains = []
        for k in range(MIXER_CHAINS):
            tok = pl.ds(k * rows, rows)
            chains.append(_mixer_chain_phases(
                s % nt, nt - 1, k * rows * ROW_TILE, x_ref.at[k], *weights,
                x1_ref.at[k], h2_ref, ti_ref.at[tok], tw_ref.at[tok], sfin_ref.at[k], ptail_ref.at[k],
                *[r.at[k] for r in scratch]))
        n_phases = len(chains[0])
        for i in range(n_phases + MIXER_CHAINS - 1):
            for k in range(MIXER_CHAINS):
                if 0 <= i - k < n_phases:
                    chains[k][i - k]()

    @pl.when(s == n_main)
    def _():
        h2_ref[0:h2s_ref.shape[0], :] = h2s_ref[...]


def _mixer_chain_phases(t, last_t, h2_off, x_ref, gmix_ref, win_ref, lbl_ref, hgn_ref, wpool_ref, pscale_ref,
                        wout_ref, gffn_ref, wr_ref, br_ref,
                        x1_ref, h2_ref, ti_ref, tw_ref, sfin_ref, ptail_ref,
                        st_ref, uext_ref, qe_ref, ke_ref, kd_ref, v_ref, o_ref, p_ref, h_ref,
                        za_ref, zb_ref, plast_ref):
    rows = MIXER_ROWS
    n_chunks = rows // CHUNK

    def proj(seg):
        return _dot(h_ref[...], win_ref[:, seg * D_MODEL:(seg + 1) * D_MODEL])

    def phase_project():
        @pl.when(t == 0)
        def _():
            st_ref[...] = jnp.zeros_like(st_ref)
            uext_ref[0:POOL_CARRY, :] = jnp.zeros((POOL_CARRY, D_MODEL), F32)

        h_ref[...] = _rms(x_ref[...], gmix_ref[...]).astype(BF16)
        o_ref[...] = proj(SEG_Q)
        p_ref[...] = proj(SEG_F)
        v_ref[...] = proj(SEG_I).astype(BF16)

    def phase_hgrn_inputs():
        lb = _forget_lower_bound(lbl_ref[...])
        row_in_chunk = lax.broadcasted_iota(jnp.int32, (CHUNK, EW_LANES), 0)
        for c in range(n_chunks):
            rs = slice(c * CHUNK, (c + 1) * CHUNK)
            for l0 in range(0, D_MODEL, EW_LANES):
                ls = slice(l0, l0 + EW_LANES)
                q_raw = o_ref[rs, ls]
                q = q_raw * _sigmoid(q_raw)
                lbt = lb[:, ls]
                f = lbt + (1.0 - lbt) * _sigmoid(p_ref[rs, ls])
                k = 1.0 - f
                p = f
                s = 1
                while s < CHUNK:
                    p = p * jnp.where(row_in_chunk >= s, pltpu.roll(p, s, axis=0), 1.0)
                    s *= 2
                plast = p[CHUNK - 1:CHUNK, :]
                qe_ref[rs, ls] = (q * p).astype(BF16)
                ke_ref[rs, ls] = (k / p).astype(BF16)
                kd_ref[rs, ls] = (k * (plast / p)).astype(BF16)
                plast_ref[c:c + 1, ls] = plast

    def phase_hgrn():
        causal = (lax.broadcasted_iota(jnp.int32, (CHUNK, CHUNK), 0)
                  >= lax.broadcasted_iota(jnp.int32, (CHUNK, CHUNK), 1))
        for hd in range(HEADS):
            cols = slice(hd * HEAD_DIM, (hd + 1) * HEAD_DIM)
            st = st_ref[hd]
            for c in range(n_chunks):
                rs = slice(c * CHUNK, (c + 1) * CHUNK)
                qe = qe_ref[rs, cols]
                ke = ke_ref[rs, cols]
                kd = kd_ref[rs, cols]
                vv = v_ref[rs, cols]
                decay = plast_ref[c:c + 1, cols]
                scores = jnp.where(causal, _dot_nt(qe, ke), 0.0)
                o_ref[rs, cols] = _dot_nt(qe, st.astype(BF16)) + _dot(scores.astype(BF16), vv)
                st = st * decay + _dot_tn(vv, kd)
            st_ref[hd] = st

        @pl.when(t == last_t)
        def _():
            for hd in range(HEADS):
                sfin_ref[hd] = st_ref[hd].T

    def phase_pool_and_gates():
        uext_ref[POOL_CARRY:POOL_CARRY + rows, :] = proj(SEG_U)
        pos1 = t * rows + lax.broadcasted_iota(jnp.int32, (rows, 1), 0) + 1
        pooled = []
        for g, w in enumerate(POOL_WINDOWS):
            cols = slice(g * POOL_GW, (g + 1) * POOL_GW)
            sw = uext_ref[:, cols]
            s = 1
            while s < w:
                sw = sw + pltpu.roll(sw, s, axis=0)
                s *= 2
            inv_cnt = 1.0 / jnp.minimum(pos1, w).astype(F32)
            dg = sw[POOL_CARRY:, :] * inv_cnt - uext_ref[POOL_CARRY:POOL_CARRY + rows, cols]
            pooled.append(_dot(dg.astype(BF16), wpool_ref[g]))

        @pl.when(t == last_t)
        def _():
            ptail_ref[...] = uext_ref[rows + POOL_CARRY - POOL_BUF:rows + POOL_CARRY, :]

        uext_ref[0:POOL_CARRY, :] = uext_ref[rows:rows + POOL_CARRY, :]
        uext_ref[POOL_CARRY:POOL_CARRY + rows, :] = jnp.concatenate(pooled, axis=-1) * pscale_ref[...]
        p_ref[...] = proj(SEG_G)
        za_ref[...] = proj(SEG_A)
        zb_ref[...] = proj(SEG_B)

    def phase_merge():
        hgn = hgn_ref[...]
        for c in range(n_chunks):
            rs = slice(c * CHUNK, (c + 1) * CHUNK)
            ys = slice(POOL_CARRY + c * CHUNK, POOL_CARRY + (c + 1) * CHUNK)
            for hd in range(HEADS):
                cols = slice(hd * HEAD_DIM, (hd + 1) * HEAD_DIM)
                g_raw = p_ref[rs, cols]
                o = _rms(o_ref[rs, cols], hgn) * (g_raw * _sigmoid(g_raw))
                merged = _sigmoid(za_ref[rs, cols]) * o + _sigmoid(zb_ref[rs, cols]) * uext_ref[ys, cols]
                h_ref[rs, cols] = merged.astype(BF16)

    def phase_out_and_route():
        x1_ref[...] = x_ref[...] + _dot(h_ref[...], wout_ref[...])
        gffn = gffn_ref[...]
        for c in range(n_chunks):
            rs = slice(c * CHUNK, (c + 1) * CHUNK)
            h2 = _rms(x1_ref[rs, :], gffn)
            _store_row_tiles(h2_ref, (), h2, offset=h2_off + c * CHUNK * ROW_TILE)
            qe_ref[rs, :], ke_ref[rs, :] = _split_bf16(h2)
        _route(qe_ref[...], ke_ref[...], wr_ref, br_ref, ti_ref, tw_ref)

    return [phase_project, phase_hgrn_inputs, phase_hgrn, phase_pool_and_gates, phase_merge, phase_out_and_route]


def _const_spec(shape):
    zeros = (0,) * len(shape)
    return pl.BlockSpec(shape, lambda *_: zeros, pipeline_mode=pl.Buffered(1))


def _mixer_prompt(x, gmix, win, lbl, hgn, wpool, pscale, wout, gffn, wr, br, h2_tail):
    b, t, _ = x.shape
    rows, nc = MIXER_ROWS, MIXER_CHAINS
    nt = t // rows
    n_main = (b // nc) * nt
    assert b % nc == 0 and t % rows == 0 and h2_tail.shape[0] <= nc * rows * ROW_TILE
    main = lambda s: jnp.minimum(s, n_main - 1)
    tok_spec = pl.BlockSpec((nc, rows, D_MODEL), lambda s: (main(s) // nt, main(s) % nt, 0))
    flat_spec = pl.BlockSpec((nc * rows * ROW_TILE, LANES), lambda s: (s, 0))
    lane_spec = pl.BlockSpec((nc * rows, ROUTER_LANES), lambda s: (main(s), 0))
    chain = lambda shape, dtype: pltpu.VMEM((nc,) + shape, dtype)
    return pl.pallas_call(
        functools.partial(_mixer_prompt_kernel, nt),
        grid=(n_main + 1,),
        in_specs=[
            tok_spec,
            _const_spec((1, D_MODEL)),
            _const_spec(win.shape),
            _const_spec(lbl.shape),
            _const_spec((1, HEAD_DIM)),
            _const_spec(wpool.shape),
            _const_spec((1, D_MODEL)),
            _const_spec(wout.shape),
            _const_spec((1, D_MODEL)),
            _const_spec(wr.shape),
            _const_spec(br.shape),
            _const_spec(h2_tail.shape),
        ],
        out_specs=[
            tok_spec,
            flat_spec,
            lane_spec,
            lane_spec,
            pl.BlockSpec((nc, HEADS, HEAD_DIM, HEAD_DIM), lambda s: (main(s) // nt, 0, 0, 0)),
            pl.BlockSpec((nc, POOL_BUF, D_MODEL), lambda s: (main(s) // nt, 0, 0)),
        ],
        out_shape=[
            jax.ShapeDtypeStruct(x.shape, F32),
            jax.ShapeDtypeStruct((b * t * ROW_TILE + h2_tail.shape[0], LANES), F32),
            jax.ShapeDtypeStruct((b * t, ROUTER_LANES), jnp.int32),
            jax.ShapeDtypeStruct((b * t, ROUTER_LANES), F32),
            jax.ShapeDtypeStruct((b, HEADS, HEAD_DIM, HEAD_DIM), F32),
            jax.ShapeDtypeStruct((b, POOL_BUF, D_MODEL), F32),
        ],
        scratch_shapes=[
            chain((HEADS, HEAD_DIM, HEAD_DIM), F32),
            chain((rows + POOL_CARRY, D_MODEL), F32),
            chain((rows, D_MODEL), BF16),
            chain((rows, D_MODEL), BF16),
            chain((rows, D_MODEL), BF16),
            chain((rows, D_MODEL), BF16),
            chain((rows, D_MODEL), F32),
            chain((rows, D_MODEL), F32),
            chain((rows, D_MODEL), BF16),
            chain((rows, D_MODEL), F32),
            chain((rows, D_MODEL), F32),
            chain((rows // CHUNK, D_MODEL), F32),
        ],
        compiler_params=pltpu.CompilerParams(dimension_semantics=("arbitrary",), vmem_limit_bytes=VMEM_LIMIT),
        name="mixer_prompt",
    )(x, gmix, win, lbl, hgn, wpool, pscale, wout, gffn, wr, br, h2_tail)


SAMPLE_GROUP = 8


def _mixer_sample_kernel(pool_cnt, x_ref, gmix_ref, win_ref, wqft_ref, lblt_ref, hgn_ref, wpool_ref, pscale_ref,
                         wout_ref, gffn_ref, wr_ref, br_ref, s_ref, pbuf_ref,
                         x1_ref, h2_ref, ti_ref, tw_ref, snew_ref, pnew_ref,
                         h_ref, znat_ref, o_ref, bsum_ref):
    step = pl.program_id(0)
    last = pl.num_programs(0) - 1
    r0 = pl.multiple_of(step * SAMPLE_GROUP, SAMPLE_GROUP)

    @pl.when(step == 0)
    def _():
        hf = _rms(x_ref[...], gmix_ref[...])
        h_ref[...] = hf
        h = hf.astype(BF16)
        for i, seg in enumerate((SEG_I, SEG_G, SEG_U, SEG_A, SEG_B)):
            znat_ref[:, i * D_MODEL:(i + 1) * D_MODEL] = _dot(h, win_ref[:, seg * D_MODEL:(seg + 1) * D_MODEL])

    hg = h_ref[pl.ds(r0, SAMPLE_GROUP), :].astype(BF16)
    qft = _dot_nt(wqft_ref[...], hg)
    lbl = lblt_ref[...]
    m = jnp.max(lbl, axis=1, keepdims=True)
    e = jnp.exp(lbl - m)
    lb = e[:, 0:1] / jnp.sum(e, axis=1, keepdims=True)
    q_raw = qft[0:D_MODEL]
    qt = q_raw * _sigmoid(q_raw)
    ft = lb + (1.0 - lb) * _sigmoid(qft[D_MODEL:2 * D_MODEL])
    kt = 1.0 - ft

    for j in range(SAMPLE_GROUP):
        v_row = znat_ref[pl.ds(r0 + j, 1), 0:D_MODEL]
        o_parts = []
        for hd in range(HEADS):
            rs = slice(hd * HEAD_DIM, (hd + 1) * HEAD_DIM)
            s_new = ft[rs, j:j + 1] * s_ref[j, hd] + kt[rs, j:j + 1] * v_row[:, rs]
            snew_ref[j, hd] = s_new
            o_parts.append(jnp.sum(qt[rs, j:j + 1] * s_new, axis=0, keepdims=True))
        o_ref[pl.ds(r0 + j, 1), :] = jnp.concatenate(o_parts, axis=-1)

    u_g = znat_ref[pl.ds(r0, SAMPLE_GROUP), 2 * D_MODEL:3 * D_MODEL]
    sums = []
    for g, w in enumerate(POOL_WINDOWS):
        acc = jnp.zeros((SAMPLE_GROUP, POOL_GW), F32)
        for j in range(1, w):
            row = POOL_BUF - j
            acc = acc + pbuf_ref[:, row * D_MODEL + g * POOL_GW:row * D_MODEL + (g + 1) * POOL_GW]
        sums.append(acc)
    bsum_ref[pl.ds(r0, SAMPLE_GROUP), :] = jnp.concatenate(sums, axis=-1)
    pnew_ref[:, 0:(POOL_BUF - 1) * D_MODEL] = pbuf_ref[:, D_MODEL:POOL_BUF * D_MODEL]
    pnew_ref[:, (POOL_BUF - 1) * D_MODEL:POOL_BUF * D_MODEL] = u_g

    @pl.when(step == last)
    def _():
        x = x_ref[...]
        g_raw = znat_ref[:, D_MODEL:2 * D_MODEL]
        u = znat_ref[:, 2 * D_MODEL:3 * D_MODEL]
        o = _head_norm_gate(o_ref[...], g_raw, hgn_ref[...])
        sw = bsum_ref[...] + u
        pooled = []
        for g, w in enumerate(POOL_WINDOWS):
            cols = slice(g * POOL_GW, (g + 1) * POOL_GW)
            dg = sw[:, cols] * (1.0 / pool_cnt[g]) - u[:, cols]
            pooled.append(_dot(dg.astype(BF16), wpool_ref[g]))
        y_pool = jnp.concatenate(pooled, axis=-1) * pscale_ref[...]
        merged = (_sigmoid(znat_ref[:, 3 * D_MODEL:4 * D_MODEL]) * o
                  + _sigmoid(znat_ref[:, 4 * D_MODEL:5 * D_MODEL]) * y_pool)
        x1 = x + _dot(merged.astype(BF16), wout_ref[...])
        x1_ref[...] = x1
        h2 = _rms(x1, gffn_ref[...])
        _store_row_tiles(h2_ref, (), h2)
        _route(*_split_bf16(h2), wr_ref, br_ref, ti_ref, tw_ref)


def _mixer_sample(x, gmix, win, wqft, lblt, hgn, wpool, pscale, wout, gffn, wr, br, state, pbuf, start_pos):
    n = x.shape[0]
    steps = n // SAMPLE_GROUP
    pool_cnt = tuple(float(min(start_pos + 1, w)) for w in POOL_WINDOWS)
    pbuf2 = pbuf.reshape(n, POOL_BUF * D_MODEL)
    full = _const_spec
    out = pl.pallas_call(
        functools.partial(_mixer_sample_kernel, pool_cnt),
        grid=(steps,),
        in_specs=[
            full((n, D_MODEL)),
            full((1, D_MODEL)),
            full(win.shape),
            full(wqft.shape),
            full(lblt.shape),
            full((1, HEAD_DIM)),
            full(wpool.shape),
            full((1, D_MODEL)),
            full(wout.shape),
            full((1, D_MODEL)),
            full(wr.shape),
            full(br.shape),
            pl.BlockSpec((SAMPLE_GROUP, HEADS, HEAD_DIM, HEAD_DIM), lambda i: (i, 0, 0, 0)),
            pl.BlockSpec((SAMPLE_GROUP, POOL_BUF * D_MODEL), lambda i: (i, 0)),
        ],
        out_specs=[
            pl.BlockSpec((n, D_MODEL), lambda i: (0, 0)),
            pl.BlockSpec((n * ROW_TILE, LANES), lambda i: (0, 0)),
            pl.BlockSpec((n, ROUTER_LANES), lambda i: (0, 0)),
            pl.BlockSpec((n, ROUTER_LANES), lambda i: (0, 0)),
            pl.BlockSpec((SAMPLE_GROUP, HEADS, HEAD_DIM, HEAD_DIM), lambda i: (i, 0, 0, 0)),
            pl.BlockSpec((SAMPLE_GROUP, POOL_BUF * D_MODEL), lambda i: (i, 0)),
        ],
        out_shape=[
            jax.ShapeDtypeStruct((n, D_MODEL), F32),
            jax.ShapeDtypeStruct((n * ROW_TILE, LANES), F32),
            jax.ShapeDtypeStruct((n, ROUTER_LANES), jnp.int32),
            jax.ShapeDtypeStruct((n, ROUTER_LANES), F32),
            jax.ShapeDtypeStruct(state.shape, F32),
            jax.ShapeDtypeStruct(pbuf2.shape, F32),
        ],
        scratch_shapes=[
            pltpu.VMEM((n, D_MODEL), F32),
            pltpu.VMEM((n, 5 * D_MODEL), F32),
            pltpu.VMEM((n, D_MODEL), F32),
            pltpu.VMEM((n, D_MODEL), F32),
        ],
        compiler_params=pltpu.CompilerParams(dimension_semantics=("arbitrary",), vmem_limit_bytes=VMEM_LIMIT),
        name="mixer_sample",
    )(x, gmix, win, wqft, lblt, hgn, wpool, pscale, wout, gffn, wr, br, state, pbuf2)
    x1, h2, ti, tw, snew, pnew = out
    return x1, h2, ti, tw, snew, pnew.reshape(n, POOL_BUF, D_MODEL)


def _moe_kernel(n_tokens, be_ref, off_ref, nv_ref, nu_ref, sa_ref, h2_hbm,
                wg_ref, bg_ref, wu_ref, bu_ref, wd_ref, bd_ref, y_hbm,
                xbuf0, xbuf1, xbuf2, ybuf0, ybuf1, ybuf2, wgb, wub, wdb, xb_ref, gate_ref, act_ref, gsem, ssem):
    rows = MOE_ROWS
    i = pl.program_id(0)
    n_used = nu_ref[0]
    n_blocks = pl.num_programs(0)
    nv_back = lambda k: jnp.where(i >= k, nv_ref[jnp.maximum(i - k, 0)], 0)
    prev = jnp.maximum(i - 1, 0)
    nv_prev = nv_back(1)
    xbuf = (xbuf0, xbuf1, xbuf2)
    ybuf = (ybuf0, ybuf1, ybuf2)

    def tile_rows(first, n=1):
        start = first * ROW_TILE
        return pl.ds(start if isinstance(first, int) else pl.multiple_of(start, ROW_TILE), n * ROW_TILE)

    def gather_copy(a, r, s):
        tok = a >> 2
        return pltpu.make_async_copy(h2_hbm.at[tile_rows(tok), :], xbuf[s].at[tile_rows(r), :], gsem.at[s])

    def scatter_copy(a, r, n, s):
        dst = (a & (TOP_K - 1)) * n_tokens + (a >> 2)
        return pltpu.make_async_copy(ybuf[s].at[tile_rows(r, n), :], y_hbm.at[tile_rows(dst, n), :], ssem.at[s])

    def wait_gather(s):
        pltpu.make_async_copy(h2_hbm.at[tile_rows(0, rows), :], xbuf[s], gsem.at[s]).wait()

    def wait_scatter(n, s):
        size = rows
        while size >= 1:
            @pl.when((n & size) != 0)
            def _():
                scatter_copy(0, 0, size, s).wait()
            size //= 2

    def scatter_loop(blk, n, s):
        def body(r, c):
            scatter_copy(sa_ref[off_ref[blk] + r], r, 1, s).start()
            return c
        lax.fori_loop(0, n, body, 0)

    n_phases = 3

    def start_copies(phase, cur, full_prev):
        ahead, behind = (cur + 2) % MOE_DEPTH, (cur + MOE_DEPTH - 1) % MOE_DEPTH
        off_ahead = off_ref[jnp.minimum(i + 2, n_blocks - 1)]
        off_prev = off_ref[prev]
        for r in range(rows * phase // n_phases, rows * (phase + 1) // n_phases):
            gather_copy(sa_ref[off_ahead + r], r, ahead).start(priority=1)
            if full_prev:
                scatter_copy(sa_ref[off_prev + r], r, 1, behind).start(priority=0)

    def phase_load(cur, full_prev):
        wait_gather(cur)
        xb_ref[...] = _load_row_tiles(xbuf[cur], (), rows).astype(BF16)

    def phase_gate(cur, full_prev):
        start_copies(0, cur, full_prev)
        gate_ref[...] = jnp.minimum(_dot(xb_ref[...], wgb[...]) + bg_ref[0], SWIGLU_LIMIT)

    def phase_up(cur, full_prev):
        start_copies(1, cur, full_prev)
        up = jnp.clip(_dot(xb_ref[...], wub[...]) + bu_ref[0], -SWIGLU_LIMIT, SWIGLU_LIMIT)
        gate = gate_ref[...]
        act_ref[...] = ((up + 1.0) * gate * _sigmoid(SWIGLU_ALPHA * gate)).astype(BF16)

    def phase_down(cur, full_prev):
        nxt, behind = (cur + 1) % MOE_DEPTH, (cur + MOE_DEPTH - 1) % MOE_DEPTH
        start_copies(2, cur, full_prev)
        wait_scatter(nv_back(MOE_DEPTH), cur)
        _store_row_tiles(ybuf[cur], (), _dot(act_ref[...], wdb[...]) + bd_ref[0])
        if not full_prev:
            scatter_loop(prev, nv_prev, behind)

        @pl.when(i == n_used - 1)
        def _():
            scatter_loop(i, nv_ref[i], cur)
            wait_scatter(nv_back(2), nxt)
            wait_scatter(nv_prev, behind)
            wait_scatter(nv_ref[i], cur)
            wait_gather(nxt)
            wait_gather((cur + 2) % MOE_DEPTH)

    @pl.when(i == 0)
    def _():
        for blk in range(MOE_DEPTH - 1):
            def body(r, c):
                gather_copy(sa_ref[off_ref[blk] + r], r, blk).start()
                return c
            lax.fori_loop(0, rows, body, 0)

    @pl.when(i < n_used)
    def _():
        first_of_expert = jnp.logical_or(i == 0, be_ref[i] != be_ref[prev])

        @pl.when(first_of_expert)
        def _():
            wgb[...] = wg_ref[0].astype(BF16)
            wub[...] = wu_ref[0].astype(BF16)
            wdb[...] = wd_ref[0].astype(BF16)

        prev_is_full = nv_prev == rows
        for phase in (phase_load, phase_gate, phase_up, phase_down):
            for cur in range(MOE_DEPTH):
                for full_prev in (True, False):
                    full_cond = prev_is_full if full_prev else jnp.logical_not(prev_is_full)

                    @pl.when(jnp.logical_and(i % MOE_DEPTH == cur, full_cond))
                    def _():
                        phase(cur, full_prev)


def _moe(h2, routing, wg, bg, wu, bu, wd, bd):
    rows = MOE_ROWS
    n_tokens = h2.shape[0] // ROW_TILE
    block_e, block_off, block_nv, n_used, sorted_a = routing
    n_blocks = block_e.shape[0]
    w_spec = pl.BlockSpec((1, D_MODEL, D_MODEL), lambda i, be, *_: (be[i], 0, 0))
    b_spec = pl.BlockSpec((1, 1, D_MODEL), lambda i, be, *_: (be[i], 0, 0))
    grid_spec = pltpu.PrefetchScalarGridSpec(
        num_scalar_prefetch=5,
        grid=(n_blocks,),
        in_specs=[pl.BlockSpec(memory_space=pl.ANY), w_spec, b_spec, w_spec, b_spec, w_spec, b_spec],
        out_specs=pl.BlockSpec(memory_space=pl.ANY),
        scratch_shapes=[
            *[pltpu.VMEM((rows * ROW_TILE, LANES), F32)] * (2 * MOE_DEPTH),
            pltpu.VMEM((D_MODEL, D_MODEL), BF16),
            pltpu.VMEM((D_MODEL, D_MODEL), BF16),
            pltpu.VMEM((D_MODEL, D_MODEL), BF16),
            pltpu.VMEM((rows, D_MODEL), BF16),
            pltpu.VMEM((rows, D_MODEL), F32),
            pltpu.VMEM((rows, D_MODEL), BF16),
            pltpu.SemaphoreType.DMA((MOE_DEPTH,)),
            pltpu.SemaphoreType.DMA((MOE_DEPTH,)),
        ],
    )
    return pl.pallas_call(
        functools.partial(_moe_kernel, n_tokens),
        grid_spec=grid_spec,
        out_shape=jax.ShapeDtypeStruct((n_tokens * TOP_K * ROW_TILE, LANES), F32),
        compiler_params=pltpu.CompilerParams(dimension_semantics=("arbitrary",), vmem_limit_bytes=VMEM_LIMIT),
        name="moe_experts",
    )(block_e, block_off, block_nv, n_used, sorted_a, h2, wg, bg, wu, bu, wd, bd)


ASSIGN_BITS = 17


def _moe_routing(top_i):
    rows = MOE_ROWS
    n_assign = top_i.shape[0] * TOP_K
    assert n_assign <= 1 << ASSIGN_BITS
    n_blocks = -(-n_assign // rows) + N_EXPERTS
    flat_e = top_i.reshape(-1)
    keys = jnp.sort(flat_e * (1 << ASSIGN_BITS) + jnp.arange(n_assign, dtype=jnp.int32))
    smem_len = -(-(n_assign + rows) // 1024) * 1024
    sorted_a = jnp.pad(keys & ((1 << ASSIGN_BITS) - 1), (0, smem_len - n_assign))
    experts = jnp.arange(N_EXPERTS, dtype=jnp.int32)
    counts = jnp.sum((flat_e[:, None] == experts[None, :]).astype(jnp.int32), axis=0)
    nblk = (counts + rows - 1) // rows
    blk_end = jnp.cumsum(nblk)
    blk_start = blk_end - nblk
    row_start = jnp.cumsum(counts) - counts
    blk = jnp.arange(n_blocks, dtype=jnp.int32)
    owner = (blk[:, None] >= blk_start[None, :]) & (blk[:, None] < blk_end[None, :])
    pick = lambda v: jnp.sum(jnp.where(owner, v[None, :], 0), axis=1)
    within = (blk - pick(blk_start)) * rows
    used = blk < blk_end[-1]
    block_e = jnp.where(used, pick(experts), N_EXPERTS - 1).astype(jnp.int32)
    block_off = jnp.where(used, pick(row_start) + within, 0).astype(jnp.int32)
    block_nv = jnp.where(used, jnp.clip(pick(counts) - within, 0, rows), 0).astype(jnp.int32)
    n_used = blk_end[-1].astype(jnp.int32).reshape(1)
    return block_e, block_off, block_nv, n_used, sorted_a.astype(jnp.int32)


def _final_kernel(x1_ref, y_ref, tw_ref, p_ref, gple_ref, wpg_ref, wpp_ref, gfin_ref, out_ref):
    tw = tw_ref[...]
    rows = tw.shape[0]
    x = x1_ref[...].reshape(rows, D_MODEL)
    for j in range(TOP_K):
        x = x + tw[:, j:j + 1] * _load_row_tiles(y_ref, (j,), rows)
    gate = _sigmoid(_dot(_rms(x, gple_ref[...]).astype(BF16), wpg_ref[...]))
    x = x + gate * _dot(p_ref[...].reshape(rows, -1).astype(BF16), wpp_ref[...])
    out_ref[...] = _rms(x, gfin_ref[...]).reshape(out_ref.shape)


def _final(x1, y4, tw, p, gple, wpg, wpp, gfin, chains, rows, tok_offset):
    g, t, _ = x1.shape
    nt = t // rows
    flat = chains * rows
    off = tok_offset // flat
    assert off * flat == tok_offset and g % chains == 0 and t % rows == 0
    ple = p.shape[-1]
    tok_map = lambda i: (i // nt, i % nt, 0)
    return pl.pallas_call(
        _final_kernel,
        grid=((g // chains) * nt,),
        in_specs=[
            pl.BlockSpec((chains, rows, D_MODEL), tok_map),
            pl.BlockSpec((TOP_K, flat * ROW_TILE, LANES), lambda i: (0, i + off, 0)),
            pl.BlockSpec((flat, ROUTER_LANES), lambda i: (i, 0)),
            pl.BlockSpec((chains, rows, ple), tok_map),
            _const_spec((1, D_MODEL)),
            _const_spec(wpg.shape),
            _const_spec(wpp.shape),
            _const_spec((1, D_MODEL)),
        ],
        out_specs=pl.BlockSpec((chains, rows, D_MODEL), tok_map),
        out_shape=jax.ShapeDtypeStruct(x1.shape, F32),
        compiler_params=pltpu.CompilerParams(dimension_semantics=("arbitrary",), vmem_limit_bytes=VMEM_LIMIT),
        name="final",
    )(x1, y4, tw, p, gple, wpg, wpp, gfin)


def kernel(x_prompt, x_sample, p_prompt, p_sample, state_hgrn, state_pool, g_mix, w_in, hg_lb_logits, hg_norm, w_pool,
           pool_scale, w_out, g_ffn, w_router, b_router, w_gate, b_gate, w_up, b_up, w_down, b_down, g_ple,
           w_ple_gate, w_ple_proj, g_final):
    depth = w_in.shape[0]
    assert depth == 1, "single-layer step"
    b, t, _ = x_prompt.shape
    ns = x_sample.shape[0]
    assert x_sample.shape[1] == 1
    n_prompt = b * t
    n_tok = n_prompt + ns

    row = lambda a: a.reshape(1, -1)
    win = w_in[0].astype(BF16)
    wqft = w_in[0, :, :2 * D_MODEL].T.astype(BF16)
    lbl = hg_lb_logits.astype(F32)
    wpool = w_pool[0].astype(BF16)
    wout = w_out[0].astype(BF16)
    wr = jnp.pad(w_router[0], ((0, 0), (0, ROUTER_LANES - N_EXPERTS)))
    br = jnp.pad(b_router[0], (0, ROUTER_LANES - N_EXPERTS)).reshape(1, -1)
    shared = (row(hg_norm[0]), wpool, row(pool_scale[0]), wout, row(g_ffn[0]), wr, br)

    x1_s, h2_s, ti_s, tw_s, s_s, pool_s = _mixer_sample(
        x_sample.reshape(ns, D_MODEL), row(g_mix[0]), win, wqft, lbl.T, *shared,
        state_hgrn[0], state_pool[0], start_pos=PAST_LEN)
    x1_p, h2, ti_p, tw_p, s_p, pool_p = _mixer_prompt(x_prompt, row(g_mix[0]), win, lbl, *shared, h2_s)

    top_i = jnp.concatenate([ti_p[:, :TOP_K], ti_s[:, :TOP_K]], axis=0)
    routing = _moe_routing(top_i)
    e3 = lambda a: a[0].reshape(N_EXPERTS, 1, D_MODEL)
    y4 = _moe(h2, routing, w_gate[0], e3(b_gate), w_up[0], e3(b_up), w_down[0], e3(b_down))
    y4 = y4.reshape(TOP_K, n_tok * ROW_TILE, LANES)

    wpg = w_ple_gate[0].astype(BF16)
    wpp = w_ple_proj[0].astype(BF16)
    fin = (row(g_ple[0]), wpg, wpp, row(g_final))
    y_p = _final(x1_p, y4, tw_p, p_prompt[0], *fin, chains=MIXER_CHAINS, rows=MIXER_ROWS, tok_offset=0)
    y_s = _final(x1_s[None], y4, tw_s, p_sample[0].reshape(1, ns, -1), *fin, chains=1, rows=ns, tok_offset=n_prompt)

    return (y_p, y_s.reshape(ns, 1, D_MODEL), s_p[None], pool_p[None], s_s[None], pool_s[None])
```

```python
import functools

import jax
import jax.numpy as jnp
from jax import lax
from jax.experimental import pallas as pl
from jax.experimental.pallas import tpu as pltpu

F32 = jnp.float32
BF16 = jnp.bfloat16

D_MODEL = 1024
HEADS = 8
HEAD_DIM = 128
CHUNK = 32
POOL_WINDOWS = (2, 4, 8, 16)
POOL_GW = D_MODEL // len(POOL_WINDOWS)
POOL_BUF = 15
POOL_CARRY = 16
N_EXPERTS = 32
TOP_K = 4
ROUTER_LANES = 128
SWIGLU_LIMIT = 7.0
SWIGLU_ALPHA = 1.702
EPS = 1e-6
PAST_LEN = 16384

SEG_Q, SEG_F, SEG_I, SEG_G, SEG_U, SEG_A, SEG_B = range(7)

MIXER_ROWS = 512
EW_LANES = 256
MOE_ROWS = 512
MOE_DEPTH = 3
VMEM_LIMIT = 56 * 1024 * 1024


def _rms(x, g):
    ms = jnp.mean(x * x, axis=-1, keepdims=True)
    return x * lax.rsqrt(ms + EPS) * g


def _sigmoid(x):
    return 1.0 / (1.0 + jnp.exp(-x))


def _dot(a, b):
    return jnp.dot(a, b, preferred_element_type=F32)


def _dot_nt(a, b):
    return lax.dot_general(a, b, (((1,), (1,)), ((), ())), preferred_element_type=F32)


def _dot_tn(a, b):
    return lax.dot_general(a, b, (((0,), (0,)), ((), ())), preferred_element_type=F32)


LANES = 128
ROW_TILE = D_MODEL // LANES


def _store_row_tiles(ref, lead, x, period=ROW_TILE, offset=0):
    rows = x.shape[0]
    for c in range(ROW_TILE):
        ref[lead + (pl.ds(offset + c, rows, stride=period), slice(None))] = x[:, c * LANES:(c + 1) * LANES]


def _load_row_tiles(ref, lead, rows, period=ROW_TILE, offset=0):
    return jnp.concatenate(
        [ref[lead + (pl.ds(offset + c, rows, stride=period), slice(None))] for c in range(ROW_TILE)], axis=-1)


def _split_bf16(x):
    hi = x.astype(BF16)
    lo = (x - hi.astype(F32)).astype(BF16)
    return hi, lo


def _forget_lower_bound(lbl):
    m = jnp.max(lbl, axis=0, keepdims=True)
    e = jnp.exp(lbl - m)
    return e[0:1] / jnp.sum(e, axis=0, keepdims=True)


def _head_norm_gate(o, g_raw, hgn):
    parts = []
    for h in range(HEADS):
        oh = o[:, h * HEAD_DIM:(h + 1) * HEAD_DIM]
        parts.append(_rms(oh, hgn))
    return jnp.concatenate(parts, axis=-1) * (g_raw * _sigmoid(g_raw))


def _route(h_hi, h_lo, wr_ref, br_ref, ti_ref, tw_ref):
    rows = h_hi.shape[0]
    w_hi, w_lo = _split_bf16(wr_ref[...])
    logits = _dot(h_hi, w_hi) + _dot(h_lo, w_hi) + _dot(h_hi, w_lo) + br_ref[...]
    lane = lax.broadcasted_iota(jnp.int32, (rows, ROUTER_LANES), 1)
    neg = jnp.float32(-jnp.inf)
    l = jnp.where(lane < N_EXPERTS, logits, neg)
    ti = jnp.zeros((rows, ROUTER_LANES), jnp.int32)
    tw = jnp.zeros((rows, ROUTER_LANES), F32)
    m0 = None
    denom = None
    es = []
    for j in range(TOP_K):
        m = jnp.max(l, axis=-1, keepdims=True)
        idx = jnp.min(jnp.where(l == m, lane, ROUTER_LANES), axis=-1, keepdims=True)
        l = jnp.where(lane == idx, neg, l)
        if j == 0:
            m0 = m
        e = jnp.exp(m - m0)
        es.append(e)
        denom = e if denom is None else denom + e
        ti = jnp.where(lane == j, idx, ti)
    for j in range(TOP_K):
        tw = jnp.where(lane == j, es[j] / denom, tw)
    ti_ref[...] = ti
    tw_ref[...] = tw


MIXER_CHAINS = 1


def _mixer_prompt_kernel(nt, x_ref, gmix_ref, win_ref, lbl_ref, hgn_ref, wpool_ref, pscale_ref, wout_ref, gffn_ref,
                         wr_ref, br_ref, h2s_ref, x1_ref, h2_ref, ti_ref, tw_ref, sfin_ref, ptail_ref, *scratch):
    s = pl.program_id(0)
    n_main = pl.num_programs(0) - 1
    rows = MIXER_ROWS
    weights = (gmix_ref, win_ref, lbl_ref, hgn_ref, wpool_ref, pscale_ref, wout_ref, gffn_ref, wr_ref, br_ref)

    @pl.when(s < n_main)
    def _():
        chains = []
        for k in range(MIXER_CHAINS):
            tok = pl.ds(k * rows, rows)
            chains.append(_mixer_chain_phases(
                s % nt, nt - 1, k * rows * ROW_TILE, x_ref.at[k], *weights,
                x1_ref.at[k], h2_ref, ti_ref.at[tok], tw_ref.at[tok], sfin_ref.at[k], ptail_ref.at[k],
                *[r.at[k] for r in scratch]))
        n_phases = len(chains[0])
        for i in range(n_phases + MIXER_CHAINS - 1):
            for k in range(MIXER_CHAINS):
                if 0 <= i - k < n_phases:
                    chains[k][i - k]()

    @pl.when(s == n_main)
    def _():
        h2_ref[0:h2s_ref.shape[0], :] = h2s_ref[...]


def _mixer_chain_phases(t, last_t, h2_off, x_ref, gmix_ref, win_ref, lbl_ref, hgn_ref, wpool_ref, pscale_ref,
                        wout_ref, gffn_ref, wr_ref, br_ref,
                        x1_ref, h2_ref, ti_ref, tw_ref, sfin_ref, ptail_ref,
                        st_ref, uext_ref, qe_ref, ke_ref, kd_ref, v_ref, o_ref, p_ref, h_ref,
                        za_ref, zb_ref, plast_ref, zg_ref):
    rows = MIXER_ROWS
    n_chunks = rows // CHUNK

    def proj(seg):
        return _dot(h_ref[...], win_ref[:, seg * D_MODEL:(seg + 1) * D_MODEL])

    def phase_project():
        @pl.when(t == 0)
        def _():
            st_ref[...] = jnp.zeros_like(st_ref)
            uext_ref[0:POOL_CARRY, :] = jnp.zeros((POOL_CARRY, D_MODEL), F32)

        h_ref[...] = _rms(x_ref[...], gmix_ref[...]).astype(BF16)
        o_ref[...] = proj(SEG_Q)
        p_ref[...] = proj(SEG_F)

    def phase_hgrn_inputs():
        block_rows = slice(POOL_CARRY, POOL_CARRY + rows)
        targets = ((SEG_I, v_ref, slice(None)), (SEG_U, uext_ref, block_rows), (SEG_G, zg_ref, slice(None)),
                   (SEG_A, za_ref, slice(None)), (SEG_B, zb_ref, slice(None)))
        jobs = [(seg, ref, rsl, n0) for seg, ref, rsl in targets for n0 in range(0, D_MODEL, EW_LANES)]
        n_tiles = n_chunks * (D_MODEL // EW_LANES)
        n_jobs = len(jobs)

        def run_jobs(tile_index):
            while jobs and (n_jobs - len(jobs)) * n_tiles <= tile_index * n_jobs:
                seg, ref, rsl, n0 = jobs.pop(0)
                w = win_ref[:, seg * D_MODEL + n0:seg * D_MODEL + n0 + EW_LANES]
                ref[rsl, n0:n0 + EW_LANES] = _dot(h_ref[...], w).astype(ref.dtype)

        lb = _forget_lower_bound(lbl_ref[...])
        row_in_chunk = lax.broadcasted_iota(jnp.int32, (CHUNK, EW_LANES), 0)
        for c in range(n_chunks):
            rs = slice(c * CHUNK, (c + 1) * CHUNK)
            for l0 in range(0, D_MODEL, EW_LANES):
                run_jobs(c * (D_MODEL // EW_LANES) + l0 // EW_LANES)
                ls = slice(l0, l0 + EW_LANES)
                q_raw = o_ref[rs, ls]
                q = q_raw * _sigmoid(q_raw)
                lbt = lb[:, ls]
                f = lbt + (1.0 - lbt) * _sigmoid(p_ref[rs, ls])
                k = 1.0 - f
                p = f
                s = 1
                while s < CHUNK:
                    p = p * jnp.where(row_in_chunk >= s, pltpu.roll(p, s, axis=0), 1.0)
                    s *= 2
                plast = p[CHUNK - 1:CHUNK, :]
                qe_ref[rs, ls] = (q * p).astype(BF16)
                ke_ref[rs, ls] = (k / p).astype(BF16)
                kd_ref[rs, ls] = (k * (plast / p)).astype(BF16)
                plast_ref[c:c + 1, ls] = plast
        assert not jobs

    def phase_hgrn():
        causal = (lax.broadcasted_iota(jnp.int32, (CHUNK, CHUNK), 0)
                  >= lax.broadcasted_iota(jnp.int32, (CHUNK, CHUNK), 1))
        for hd in range(HEADS):
            cols = slice(hd * HEAD_DIM, (hd + 1) * HEAD_DIM)
            st = st_ref[hd]
            for c in range(n_chunks):
                rs = slice(c * CHUNK, (c + 1) * CHUNK)
                qe = qe_ref[rs, cols]
                ke = ke_ref[rs, cols]
                kd = kd_ref[rs, cols]
                vv = v_ref[rs, cols]
                decay = plast_ref[c:c + 1, cols]
                scores = jnp.where(causal, _dot_nt(qe, ke), 0.0)
                o_ref[rs, cols] = _dot_nt(qe, st.astype(BF16)) + _dot(scores.astype(BF16), vv)
                st = st * decay + _dot_tn(vv, kd)
            st_ref[hd] = st

        @pl.when(t == last_t)
        def _():
            for hd in range(HEADS):
                sfin_ref[hd] = st_ref[hd].T

    def phase_pool():
        pos1 = t * rows + lax.broadcasted_iota(jnp.int32, (rows, 1), 0) + 1
        pooled = []
        for g, w in enumerate(POOL_WINDOWS):
            cols = slice(g * POOL_GW, (g + 1) * POOL_GW)
            sw = uext_ref[:, cols]
            s = 1
            while s < w:
                sw = sw + pltpu.roll(sw, s, axis=0)
                s *= 2
            inv_cnt = 1.0 / jnp.minimum(pos1, w).astype(F32)
            dg = sw[POOL_CARRY:, :] * inv_cnt - uext_ref[POOL_CARRY:POOL_CARRY + rows, cols]
            pooled.append(_dot(dg.astype(BF16), wpool_ref[g]))

        @pl.when(t == last_t)
        def _():
            ptail_ref[...] = uext_ref[rows + POOL_CARRY - POOL_BUF:rows + POOL_CARRY, :]

        uext_ref[0:POOL_CARRY, :] = uext_ref[rows:rows + POOL_CARRY, :]
        uext_ref[POOL_CARRY:POOL_CARRY + rows, :] = jnp.concatenate(pooled, axis=-1) * pscale_ref[...]

    def phase_merge():
        hgn = hgn_ref[...]
        for c in range(n_chunks):
            rs = slice(c * CHUNK, (c + 1) * CHUNK)
            ys = slice(POOL_CARRY + c * CHUNK, POOL_CARRY + (c + 1) * CHUNK)
            for hd in range(HEADS):
                cols = slice(hd * HEAD_DIM, (hd + 1) * HEAD_DIM)
                g_raw = zg_ref[rs, cols]
                o = _rms(o_ref[rs, cols], hgn) * (g_raw * _sigmoid(g_raw))
                merged = _sigmoid(za_ref[rs, cols]) * o + _sigmoid(zb_ref[rs, cols]) * uext_ref[ys, cols]
                h_ref[rs, cols] = merged.astype(BF16)

    def phase_out_and_route():
        x1_ref[...] = x_ref[...] + _dot(h_ref[...], wout_ref[...])
        gffn = gffn_ref[...]
        for c in range(n_chunks):
            rs = slice(c * CHUNK, (c + 1) * CHUNK)
            h2 = _rms(x1_ref[rs, :], gffn)
            _store_row_tiles(h2_ref, (), h2, offset=h2_off + c * CHUNK * ROW_TILE)
            qe_ref[rs, :], ke_ref[rs, :] = _split_bf16(h2)
        _route(qe_ref[...], ke_ref[...], wr_ref, br_ref, ti_ref, tw_ref)

    return [phase_project, phase_hgrn_inputs, phase_hgrn, phase_pool, phase_merge, phase_out_and_route]


def _const_spec(shape):
    zeros = (0,) * len(shape)
    return pl.BlockSpec(shape, lambda *_: zeros, pipeline_mode=pl.Buffered(1))


def _mixer_prompt(x, gmix, win, lbl, hgn, wpool, pscale, wout, gffn, wr, br, h2_tail):
    b, t, _ = x.shape
    rows, nc = MIXER_ROWS, MIXER_CHAINS
    nt = t // rows
    n_main = (b // nc) * nt
    assert b % nc == 0 and t % rows == 0 and h2_tail.shape[0] <= nc * rows * ROW_TILE
    main = lambda s: jnp.minimum(s, n_main - 1)
    tok_spec = pl.BlockSpec((nc, rows, D_MODEL), lambda s: (main(s) // nt, main(s) % nt, 0))
    flat_spec = pl.BlockSpec((nc * rows * ROW_TILE, LANES), lambda s: (s, 0))
    lane_spec = pl.BlockSpec((nc * rows, ROUTER_LANES), lambda s: (main(s), 0))
    chain = lambda shape, dtype: pltpu.VMEM((nc,) + shape, dtype)
    return pl.pallas_call(
        functools.partial(_mixer_prompt_kernel, nt),
        grid=(n_main + 1,),
        in_specs=[
            tok_spec,
            _const_spec((1, D_MODEL)),
            _const_spec(win.shape),
            _const_spec(lbl.shape),
            _const_spec((1, HEAD_DIM)),
            _const_spec(wpool.shape),
            _const_spec((1, D_MODEL)),
            _const_spec(wout.shape),
            _const_spec((1, D_MODEL)),
            _const_spec(wr.shape),
            _const_spec(br.shape),
            _const_spec(h2_tail.shape),
        ],
        out_specs=[
            tok_spec,
            flat_spec,
            lane_spec,
            lane_spec,
            pl.BlockSpec((nc, HEADS, HEAD_DIM, HEAD_DIM), lambda s: (main(s) // nt, 0, 0, 0)),
            pl.BlockSpec((nc, POOL_BUF, D_MODEL), lambda s: (main(s) // nt, 0, 0)),
        ],
        out_shape=[
            jax.ShapeDtypeStruct(x.shape, F32),
            jax.ShapeDtypeStruct((b * t * ROW_TILE + h2_tail.shape[0], LANES), F32),
            jax.ShapeDtypeStruct((b * t, ROUTER_LANES), jnp.int32),
            jax.ShapeDtypeStruct((b * t, ROUTER_LANES), F32),
            jax.ShapeDtypeStruct((b, HEADS, HEAD_DIM, HEAD_DIM), F32),
            jax.ShapeDtypeStruct((b, POOL_BUF, D_MODEL), F32),
        ],
        scratch_shapes=[
            chain((HEADS, HEAD_DIM, HEAD_DIM), F32),
            chain((rows + POOL_CARRY, D_MODEL), F32),
            chain((rows, D_MODEL), BF16),
            chain((rows, D_MODEL), BF16),
            chain((rows, D_MODEL), BF16),
            chain((rows, D_MODEL), BF16),
            chain((rows, D_MODEL), F32),
            chain((rows, D_MODEL), F32),
            chain((rows, D_MODEL), BF16),
            chain((rows, D_MODEL), F32),
            chain((rows, D_MODEL), F32),
            chain((rows // CHUNK, D_MODEL), F32),
            chain((rows, D_MODEL), F32),
        ],
        compiler_params=pltpu.CompilerParams(dimension_semantics=("arbitrary",), vmem_limit_bytes=VMEM_LIMIT),
        name="mixer_prompt",
    )(x, gmix, win, lbl, hgn, wpool, pscale, wout, gffn, wr, br, h2_tail)


SAMPLE_GROUP = 8


def _mixer_sample_kernel(pool_cnt, x_ref, gmix_ref, win_ref, wqft_ref, lblt_ref, hgn_ref, wpool_ref, pscale_ref,
                         wout_ref, gffn_ref, wr_ref, br_ref, s_ref, pbuf_ref,
                         x1_ref, h2_ref, ti_ref, tw_ref, snew_ref, pnew_ref,
                         h_ref, znat_ref, o_ref, bsum_ref):
    step = pl.program_id(0)
    last = pl.num_programs(0) - 1
    r0 = pl.multiple_of(step * SAMPLE_GROUP, SAMPLE_GROUP)

    @pl.when(step == 0)
    def _():
        hf = _rms(x_ref[...], gmix_ref[...])
        h_ref[...] = hf
        h = hf.astype(BF16)
        for i, seg in enumerate((SEG_I, SEG_G, SEG_U, SEG_A, SEG_B)):
            znat_ref[:, i * D_MODEL:(i + 1) * D_MODEL] = _dot(h, win_ref[:, seg * D_MODEL:(seg + 1) * D_MODEL])

    hg = h_ref[pl.ds(r0, SAMPLE_GROUP), :].astype(BF16)
    qft = _dot_nt(wqft_ref[...], hg)
    lbl = lblt_ref[...]
    m = jnp.max(lbl, axis=1, keepdims=True)
    e = jnp.exp(lbl - m)
    lb = e[:, 0:1] / jnp.sum(e, axis=1, keepdims=True)
    q_raw = qft[0:D_MODEL]
    qt = q_raw * _sigmoid(q_raw)
    ft = lb + (1.0 - lb) * _sigmoid(qft[D_MODEL:2 * D_MODEL])
    kt = 1.0 - ft

    for j in range(SAMPLE_GROUP):
        v_row = znat_ref[pl.ds(r0 + j, 1), 0:D_MODEL]
        o_parts = []
        for hd in range(HEADS):
            rs = slice(hd * HEAD_DIM, (hd + 1) * HEAD_DIM)
            s_new = ft[rs, j:j + 1] * s_ref[j, hd] + kt[rs, j:j + 1] * v_row[:, rs]
            snew_ref[j, hd] = s_new
            o_parts.append(jnp.sum(qt[rs, j:j + 1] * s_new, axis=0, keepdims=True))
        o_ref[pl.ds(r0 + j, 1), :] = jnp.concatenate(o_parts, axis=-1)

    u_g = znat_ref[pl.ds(r0, SAMPLE_GROUP), 2 * D_MODEL:3 * D_MODEL]
    sums = []
    for g, w in enumerate(POOL_WINDOWS):
        acc = jnp.zeros((SAMPLE_GROUP, POOL_GW), F32)
        for j in range(1, w):
            row = POOL_BUF - j
            acc = acc + pbuf_ref[:, row * D_MODEL + g * POOL_GW:row * D_MODEL + (g + 1) * POOL_GW]
        sums.append(acc)
    bsum_ref[pl.ds(r0, SAMPLE_GROUP), :] = jnp.concatenate(sums, axis=-1)
    pnew_ref[:, 0:(POOL_BUF - 1) * D_MODEL] = pbuf_ref[:, D_MODEL:POOL_BUF * D_MODEL]
    pnew_ref[:, (POOL_BUF - 1) * D_MODEL:POOL_BUF * D_MODEL] = u_g

    @pl.when(step == last)
    def _():
        x = x_ref[...]
        g_raw = znat_ref[:, D_MODEL:2 * D_MODEL]
        u = znat_ref[:, 2 * D_MODEL:3 * D_MODEL]
        o = _head_norm_gate(o_ref[...], g_raw, hgn_ref[...])
        sw = bsum_ref[...] + u
        pooled = []
        for g, w in enumerate(POOL_WINDOWS):
            cols = slice(g * POOL_GW, (g + 1) * POOL_GW)
            dg = sw[:, cols] * (1.0 / pool_cnt[g]) - u[:, cols]
            pooled.append(_dot(dg.astype(BF16), wpool_ref[g]))
        y_pool = jnp.concatenate(pooled, axis=-1) * pscale_ref[...]
        merged = (_sigmoid(znat_ref[:, 3 * D_MODEL:4 * D_MODEL]) * o
                  + _sigmoid(znat_ref[:, 4 * D_MODEL:5 * D_MODEL]) * y_pool)
        x1 = x + _dot(merged.astype(BF16), wout_ref[...])
        x1_ref[...] = x1
        h2 = _rms(x1, gffn_ref[...])
        _store_row_tiles(h2_ref, (), h2)
        _route(*_split_bf16(h2), wr_ref, br_ref, ti_ref, tw_ref)


def _mixer_sample(x, gmix, win, wqft, lblt, hgn, wpool, pscale, wout, gffn, wr, br, state, pbuf, start_pos):
    n = x.shape[0]
    steps = n // SAMPLE_GROUP
    pool_cnt = tuple(float(min(start_pos + 1, w)) for w in POOL_WINDOWS)
    pbuf2 = pbuf.reshape(n, POOL_BUF * D_MODEL)
    full = _const_spec
    out = pl.pallas_call(
        functools.partial(_mixer_sample_kernel, pool_cnt),
        grid=(steps,),
        in_specs=[
            full((n, D_MODEL)),
            full((1, D_MODEL)),
            full(win.shape),
            full(wqft.shape),
            full(lblt.shape),
            full((1, HEAD_DIM)),
            full(wpool.shape),
            full((1, D_MODEL)),
            full(wout.shape),
            full((1, D_MODEL)),
            full(wr.shape),
            full(br.shape),
            pl.BlockSpec((SAMPLE_GROUP, HEADS, HEAD_DIM, HEAD_DIM), lambda i: (i, 0, 0, 0)),
            pl.BlockSpec((SAMPLE_GROUP, POOL_BUF * D_MODEL), lambda i: (i, 0)),
        ],
        out_specs=[
            pl.BlockSpec((n, D_MODEL), lambda i: (0, 0)),
            pl.BlockSpec((n * ROW_TILE, LANES), lambda i: (0, 0)),
            pl.BlockSpec((n, ROUTER_LANES), lambda i: (0, 0)),
            pl.BlockSpec((n, ROUTER_LANES), lambda i: (0, 0)),
            pl.BlockSpec((SAMPLE_GROUP, HEADS, HEAD_DIM, HEAD_DIM), lambda i: (i, 0, 0, 0)),
            pl.BlockSpec((SAMPLE_GROUP, POOL_BUF * D_MODEL), lambda i: (i, 0)),
        ],
        out_shape=[
            jax.ShapeDtypeStruct((n, D_MODEL), F32),
            jax.ShapeDtypeStruct((n * ROW_TILE, LANES), F32),
            jax.ShapeDtypeStruct((n, ROUTER_LANES), jnp.int32),
            jax.ShapeDtypeStruct((n, ROUTER_LANES), F32),
            jax.ShapeDtypeStruct(state.shape, F32),
            jax.ShapeDtypeStruct(pbuf2.shape, F32),
        ],
        scratch_shapes=[
            pltpu.VMEM((n, D_MODEL), F32),
            pltpu.VMEM((n, 5 * D_MODEL), F32),
            pltpu.VMEM((n, D_MODEL), F32),
            pltpu.VMEM((n, D_MODEL), F32),
        ],
        compiler_params=pltpu.CompilerParams(dimension_semantics=("arbitrary",), vmem_limit_bytes=VMEM_LIMIT),
        name="mixer_sample",
    )(x, gmix, win, wqft, lblt, hgn, wpool, pscale, wout, gffn, wr, br, state, pbuf2)
    x1, h2, ti, tw, snew, pnew = out
    return x1, h2, ti, tw, snew, pnew.reshape(n, POOL_BUF, D_MODEL)


def _moe_kernel(n_tokens, be_ref, off_ref, nv_ref, nu_ref, sa_ref, h2_hbm,
                wg_ref, bg_ref, wu_ref, bu_ref, wd_ref, bd_ref, y_hbm,
                xbuf0, xbuf1, xbuf2, ybuf0, ybuf1, ybuf2, wgb, wub, wdb, xb_ref, gate_ref, act_ref, gsem, ssem):
    rows = MOE_ROWS
    i = pl.program_id(0)
    n_used = nu_ref[0]
    n_blocks = pl.num_programs(0)
    nv_back = lambda k: jnp.where(i >= k, nv_ref[jnp.maximum(i - k, 0)], 0)
    prev = jnp.maximum(i - 1, 0)
    nv_prev = nv_back(1)
    xbuf = (xbuf0, xbuf1, xbuf2)
    ybuf = (ybuf0, ybuf1, ybuf2)

    def tile_rows(first, n=1):
        start = first * ROW_TILE
        return pl.ds(start if isinstance(first, int) else pl.multiple_of(start, ROW_TILE), n * ROW_TILE)

    def gather_copy(a, r, s):
        tok = a >> 2
        return pltpu.make_async_copy(h2_hbm.at[tile_rows(tok), :], xbuf[s].at[tile_rows(r), :], gsem.at[s])

    def scatter_copy(a, r, n, s):
        dst = (a & (TOP_K - 1)) * n_tokens + (a >> 2)
        return pltpu.make_async_copy(ybuf[s].at[tile_rows(r, n), :], y_hbm.at[tile_rows(dst, n), :], ssem.at[s])

    def wait_gather(s):
        pltpu.make_async_copy(h2_hbm.at[tile_rows(0, rows), :], xbuf[s], gsem.at[s]).wait()

    def wait_scatter(n, s):
        size = rows
        while size >= 1:
            @pl.when((n & size) != 0)
            def _():
                scatter_copy(0, 0, size, s).wait()
            size //= 2

    def scatter_loop(blk, n, s):
        def body(r, c):
            scatter_copy(sa_ref[off_ref[blk] + r], r, 1, s).start()
            return c
        lax.fori_loop(0, n, body, 0)

    n_phases = 3

    def start_copies(phase, cur, full_prev):
        ahead, behind = (cur + 2) % MOE_DEPTH, (cur + MOE_DEPTH - 1) % MOE_DEPTH
        off_ahead = off_ref[jnp.minimum(i + 2, n_blocks - 1)]
        off_prev = off_ref[prev]
        for r in range(rows * phase // n_phases, rows * (phase + 1) // n_phases):
            gather_copy(sa_ref[off_ahead + r], r, ahead).start(priority=1)
            if full_prev:
                scatter_copy(sa_ref[off_prev + r], r, 1, behind).start(priority=0)

    def phase_load(cur, full_prev):
        wait_gather(cur)
        xb_ref[...] = _load_row_tiles(xbuf[cur], (), rows).astype(BF16)

    def phase_gate(cur, full_prev):
        start_copies(0, cur, full_prev)
        gate_ref[...] = jnp.minimum(_dot(xb_ref[...], wgb[...]) + bg_ref[0], SWIGLU_LIMIT)

    def phase_up(cur, full_prev):
        start_copies(1, cur, full_prev)
        up = jnp.clip(_dot(xb_ref[...], wub[...]) + bu_ref[0], -SWIGLU_LIMIT, SWIGLU_LIMIT)
        gate = gate_ref[...]
        act_ref[...] = ((up + 1.0) * gate * _sigmoid(SWIGLU_ALPHA * gate)).astype(BF16)

    def phase_down(cur, full_prev):
        nxt, behind = (cur + 1) % MOE_DEPTH, (cur + MOE_DEPTH - 1) % MOE_DEPTH
        start_copies(2, cur, full_prev)
        wait_scatter(nv_back(MOE_DEPTH), cur)
        _store_row_tiles(ybuf[cur], (), _dot(act_ref[...], wdb[...]) + bd_ref[0])
        if not full_prev:
            scatter_loop(prev, nv_prev, behind)

        @pl.when(i == n_used - 1)
        def _():
            scatter_loop(i, nv_ref[i], cur)
            wait_scatter(nv_back(2), nxt)
            wait_scatter(nv_prev, behind)
            wait_scatter(nv_ref[i], cur)
            wait_gather(nxt)
            wait_gather((cur + 2) % MOE_DEPTH)

    @pl.when(i == 0)
    def _():
        for blk in range(MOE_DEPTH - 1):
            def body(r, c):
                gather_copy(sa_ref[off_ref[blk] + r], r, blk).start()
                return c
            lax.fori_loop(0, rows, body, 0)

    @pl.when(i < n_used)
    def _():
        first_of_expert = jnp.logical_or(i == 0, be_ref[i] != be_ref[prev])

        @pl.when(first_of_expert)
        def _():
            wgb[...] = wg_ref[0].astype(BF16)
            wub[...] = wu_ref[0].astype(BF16)
            wdb[...] = wd_ref[0].astype(BF16)

        prev_is_full = nv_prev == rows
        for phase in (phase_load, phase_gate, phase_up, phase_down):
            for cur in range(MOE_DEPTH):
                for full_prev in (True, False):
                    full_cond = prev_is_full if full_prev else jnp.logical_not(prev_is_full)

                    @pl.when(jnp.logical_and(i % MOE_DEPTH == cur, full_cond))
                    def _():
                        phase(cur, full_prev)


def _moe(h2, routing, wg, bg, wu, bu, wd, bd):
    rows = MOE_ROWS
    n_tokens = h2.shape[0] // ROW_TILE
    block_e, block_off, block_nv, n_used, sorted_a = routing
    n_blocks = block_e.shape[0]
    w_spec = pl.BlockSpec((1, D_MODEL, D_MODEL), lambda i, be, *_: (be[i], 0, 0))
    b_spec = pl.BlockSpec((1, 1, D_MODEL), lambda i, be, *_: (be[i], 0, 0))
    grid_spec = pltpu.PrefetchScalarGridSpec(
        num_scalar_prefetch=5,
        grid=(n_blocks,),
        in_specs=[pl.BlockSpec(memory_space=pl.ANY), w_spec, b_spec, w_spec, b_spec, w_spec, b_spec],
        out_specs=pl.BlockSpec(memory_space=pl.ANY),
        scratch_shapes=[
            *[pltpu.VMEM((rows * ROW_TILE, LANES), F32)] * (2 * MOE_DEPTH),
            pltpu.VMEM((D_MODEL, D_MODEL), BF16),
            pltpu.VMEM((D_MODEL, D_MODEL), BF16),
            pltpu.VMEM((D_MODEL, D_MODEL), BF16),
            pltpu.VMEM((rows, D_MODEL), BF16),
            pltpu.VMEM((rows, D_MODEL), F32),
            pltpu.VMEM((rows, D_MODEL), BF16),
            pltpu.SemaphoreType.DMA((MOE_DEPTH,)),
            pltpu.SemaphoreType.DMA((MOE_DEPTH,)),
        ],
    )
    return pl.pallas_call(
        functools.partial(_moe_kernel, n_tokens),
        grid_spec=grid_spec,
        out_shape=jax.ShapeDtypeStruct((n_tokens * TOP_K * ROW_TILE, LANES), F32),
        compiler_params=pltpu.CompilerParams(dimension_semantics=("arbitrary",), vmem_limit_bytes=VMEM_LIMIT),
        name="moe_experts",
    )(block_e, block_off, block_nv, n_used, sorted_a, h2, wg, bg, wu, bu, wd, bd)


ASSIGN_BITS = 17


def _moe_routing(top_i):
    rows = MOE_ROWS
    n_assign = top_i.shape[0] * TOP_K
    assert n_assign <= 1 << ASSIGN_BITS
    n_blocks = -(-n_assign // rows) + N_EXPERTS
    flat_e = top_i.reshape(-1)
    keys = jnp.sort(flat_e * (1 << ASSIGN_BITS) + jnp.arange(n_assign, dtype=jnp.int32))
    smem_len = -(-(n_assign + rows) // 1024) * 1024
    sorted_a = jnp.pad(keys & ((1 << ASSIGN_BITS) - 1), (0, smem_len - n_assign))
    experts = jnp.arange(N_EXPERTS, dtype=jnp.int32)
    counts = jnp.sum((flat_e[:, None] == experts[None, :]).astype(jnp.int32), axis=0)
    nblk = (counts + rows - 1) // rows
    blk_end = jnp.cumsum(nblk)
    blk_start = blk_end - nblk
    row_start = jnp.cumsum(counts) - counts
    blk = jnp.arange(n_blocks, dtype=jnp.int32)
    owner = (blk[:, None] >= blk_start[None, :]) & (blk[:, None] < blk_end[None, :])
    pick = lambda v: jnp.sum(jnp.where(owner, v[None, :], 0), axis=1)
    within = (blk - pick(blk_start)) * rows
    used = blk < blk_end[-1]
    block_e = jnp.where(used, pick(experts), N_EXPERTS - 1).astype(jnp.int32)
    block_off = jnp.where(used, pick(row_start) + within, 0).astype(jnp.int32)
    block_nv = jnp.where(used, jnp.clip(pick(counts) - within, 0, rows), 0).astype(jnp.int32)
    n_used = blk_end[-1].astype(jnp.int32).reshape(1)
    return block_e, block_off, block_nv, n_used, sorted_a.astype(jnp.int32)


def _final_kernel(x1_ref, y_ref, tw_ref, p_ref, gple_ref, wpg_ref, wpp_ref, gfin_ref, out_ref):
    tw = tw_ref[...]
    rows = tw.shape[0]
    x = x1_ref[...].reshape(rows, D_MODEL)
    for j in range(TOP_K):
        x = x + tw[:, j:j + 1] * _load_row_tiles(y_ref, (j,), rows)
    gate = _sigmoid(_dot(_rms(x, gple_ref[...]).astype(BF16), wpg_ref[...]))
    x = x + gate * _dot(p_ref[...].reshape(rows, -1).astype(BF16), wpp_ref[...])
    out_ref[...] = _rms(x, gfin_ref[...]).reshape(out_ref.shape)


def _final(x1, y4, tw, p, gple, wpg, wpp, gfin, chains, rows, tok_offset):
    g, t, _ = x1.shape
    nt = t // rows
    flat = chains * rows
    off = tok_offset // flat
    assert off * flat == tok_offset and g % chains == 0 and t % rows == 0
    ple = p.shape[-1]
    tok_map = lambda i: (i // nt, i % nt, 0)
    return pl.pallas_call(
        _final_kernel,
        grid=((g // chains) * nt,),
        in_specs=[
            pl.BlockSpec((chains, rows, D_MODEL), tok_map),
            pl.BlockSpec((TOP_K, flat * ROW_TILE, LANES), lambda i: (0, i + off, 0)),
            pl.BlockSpec((flat, ROUTER_LANES), lambda i: (i, 0)),
            pl.BlockSpec((chains, rows, ple), tok_map),
            _const_spec((1, D_MODEL)),
            _const_spec(wpg.shape),
            _const_spec(wpp.shape),
            _const_spec((1, D_MODEL)),
        ],
        out_specs=pl.BlockSpec((chains, rows, D_MODEL), tok_map),
        out_shape=jax.ShapeDtypeStruct(x1.shape, F32),
        compiler_params=pltpu.CompilerParams(dimension_semantics=("arbitrary",), vmem_limit_bytes=VMEM_LIMIT),
        name="final",
    )(x1, y4, tw, p, gple, wpg, wpp, gfin)


def kernel(x_prompt, x_sample, p_prompt, p_sample, state_hgrn, state_pool, g_mix, w_in, hg_lb_logits, hg_norm, w_pool,
           pool_scale, w_out, g_ffn, w_router, b_router, w_gate, b_gate, w_up, b_up, w_down, b_down, g_ple,
           w_ple_gate, w_ple_proj, g_final):
    depth = w_in.shape[0]
    assert depth == 1, "single-layer step"
    b, t, _ = x_prompt.shape
    ns = x_sample.shape[0]
    assert x_sample.shape[1] == 1
    n_prompt = b * t
    n_tok = n_prompt + ns

    row = lambda a: a.reshape(1, -1)
    win = w_in[0].astype(BF16)
    wqft = w_in[0, :, :2 * D_MODEL].T.astype(BF16)
    lbl = hg_lb_logits.astype(F32)
    wpool = w_pool[0].astype(BF16)
    wout = w_out[0].astype(BF16)
    wr = jnp.pad(w_router[0], ((0, 0), (0, ROUTER_LANES - N_EXPERTS)))
    br = jnp.pad(b_router[0], (0, ROUTER_LANES - N_EXPERTS)).reshape(1, -1)
    shared = (row(hg_norm[0]), wpool, row(pool_scale[0]), wout, row(g_ffn[0]), wr, br)

    x1_s, h2_s, ti_s, tw_s, s_s, pool_s = _mixer_sample(
        x_sample.reshape(ns, D_MODEL), row(g_mix[0]), win, wqft, lbl.T, *shared,
        state_hgrn[0], state_pool[0], start_pos=PAST_LEN)
    x1_p, h2, ti_p, tw_p, s_p, pool_p = _mixer_prompt(x_prompt, row(g_mix[0]), win, lbl, *shared, h2_s)

    top_i = jnp.concatenate([ti_p[:, :TOP_K], ti_s[:, :TOP_K]], axis=0)
    routing = _moe_routing(top_i)
    e3 = lambda a: a[0].reshape(N_EXPERTS, 1, D_MODEL)
    y4 = _moe(h2, routing, w_gate[0], e3(b_gate), w_up[0], e3(b_up), w_down[0], e3(b_down))
    y4 = y4.reshape(TOP_K, n_tok * ROW_TILE, LANES)

    wpg = w_ple_gate[0].astype(BF16)
    wpp = w_ple_proj[0].astype(BF16)
    fin = (row(g_ple[0]), wpg, wpp, row(g_final))
    y_p = _final(x1_p, y4, tw_p, p_prompt[0], *fin, chains=MIXER_CHAINS, rows=MIXER_ROWS, tok_offset=0)
    y_s = _final(x1_s[None], y4, tw_s, p_sample[0].reshape(1, ns, -1), *fin, chains=1, rows=ns, tok_offset=n_prompt)

    return (y_p, y_s.reshape(ns, 1, D_MODEL), s_p[None], pool_p[None], s_s[None], pool_s[None])
```

```python
import functools

import jax
import jax.numpy as jnp
from jax import lax
from jax.experimental import pallas as pl
from jax.experimental.pallas import tpu as pltpu

F32 = jnp.float32
BF16 = jnp.bfloat16

D_MODEL = 1024
HEADS = 8
HEAD_DIM = 128
CHUNK = 32
POOL_WINDOWS = (2, 4, 8, 16)
POOL_GW = D_MODEL // len(POOL_WINDOWS)
POOL_BUF = 15
POOL_CARRY = 16
N_EXPERTS = 32
TOP_K = 4
ROUTER_LANES = 128
SWIGLU_LIMIT = 7.0
SWIGLU_ALPHA = 1.702
EPS = 1e-6
PAST_LEN = 16384

SEG_Q, SEG_F, SEG_I, SEG_G, SEG_U, SEG_A, SEG_B = range(7)

MIXER_ROWS = 512
EW_LANES = 256
MOE_ROWS = 512
MOE_DEPTH = 3
FINAL_ROWS = 512
VMEM_LIMIT = 56 * 1024 * 1024


def _rms(x, g):
    ms = jnp.mean(x * x, axis=-1, keepdims=True)
    return x * lax.rsqrt(ms + EPS) * g


def _sigmoid(x):
    return 1.0 / (1.0 + jnp.exp(-x))


def _dot(a, b):
    return jnp.dot(a, b, preferred_element_type=F32)


def _dot_nt(a, b):
    return lax.dot_general(a, b, (((1,), (1,)), ((), ())), preferred_element_type=F32)


def _dot_tn(a, b):
    return lax.dot_general(a, b, (((0,), (0,)), ((), ())), preferred_element_type=F32)


LANES = 128
ROW_TILE = D_MODEL // LANES


def _store_row_tiles(ref, lead, x, period=ROW_TILE, offset=0):
    rows = x.shape[0]
    for c in range(ROW_TILE):
        ref[lead + (pl.ds(offset + c, rows, stride=period), slice(None))] = x[:, c * LANES:(c + 1) * LANES]


def _load_row_tiles(ref, lead, rows, period=ROW_TILE, offset=0):
    return jnp.concatenate(
        [ref[lead + (pl.ds(offset + c, rows, stride=period), slice(None))] for c in range(ROW_TILE)], axis=-1)


def _split_bf16(x):
    hi = x.astype(BF16)
    lo = (x - hi.astype(F32)).astype(BF16)
    return hi, lo


def _forget_lower_bound(lbl):
    m = jnp.max(lbl, axis=0, keepdims=True)
    e = jnp.exp(lbl - m)
    return e[0:1] / jnp.sum(e, axis=0, keepdims=True)


def _head_norm_gate(o, g_raw, hgn):
    parts = []
    for h in range(HEADS):
        oh = o[:, h * HEAD_DIM:(h + 1) * HEAD_DIM]
        parts.append(_rms(oh, hgn))
    return jnp.concatenate(parts, axis=-1) * (g_raw * _sigmoid(g_raw))


def _route(h_hi, h_lo, wr_ref, br_ref, ti_ref, tw_ref):
    rows = h_hi.shape[0]
    w_hi, w_lo = _split_bf16(wr_ref[...])
    logits = _dot(h_hi, w_hi) + _dot(h_lo, w_hi) + _dot(h_hi, w_lo) + br_ref[...]
    lane = lax.broadcasted_iota(jnp.int32, (rows, ROUTER_LANES), 1)
    neg = jnp.float32(-jnp.inf)
    l = jnp.where(lane < N_EXPERTS, logits, neg)
    ti = jnp.zeros((rows, ROUTER_LANES), jnp.int32)
    tw = jnp.zeros((rows, ROUTER_LANES), F32)
    m0 = None
    denom = None
    es = []
    for j in range(TOP_K):
        m = jnp.max(l, axis=-1, keepdims=True)
        idx = jnp.min(jnp.where(l == m, lane, ROUTER_LANES), axis=-1, keepdims=True)
        l = jnp.where(lane == idx, neg, l)
        if j == 0:
            m0 = m
        e = jnp.exp(m - m0)
        es.append(e)
        denom = e if denom is None else denom + e
        ti = jnp.where(lane == j, idx, ti)
    for j in range(TOP_K):
        tw = jnp.where(lane == j, es[j] / denom, tw)
    ti_ref[...] = ti
    tw_ref[...] = tw


def _mixer_prompt_kernel(nt, x_ref, gmix_ref, win_ref, lbl_ref, hgn_ref, wpool_ref, pscale_ref, wout_ref, gffn_ref,
                         wr_ref, br_ref, h2s_ref, x1_ref, h2_ref, ti_ref, tw_ref, sfin_ref, ptail_ref, *scratch):
    s = pl.program_id(0)
    n_main = pl.num_programs(0) - 1

    @pl.when(s < n_main)
    def _():
        _mixer_prompt_block(s % nt, nt - 1, x_ref.at[0], gmix_ref, win_ref, lbl_ref, hgn_ref, wpool_ref, pscale_ref,
                            wout_ref, gffn_ref, wr_ref, br_ref,
                            x1_ref.at[0], h2_ref, ti_ref, tw_ref, sfin_ref.at[0], ptail_ref.at[0], *scratch)

    @pl.when(s == n_main)
    def _():
        h2_ref[0:h2s_ref.shape[0], :] = h2s_ref[...]


def _mixer_prompt_block(t, last_t, x_ref, gmix_ref, win_ref, lbl_ref, hgn_ref, wpool_ref, pscale_ref,
                        wout_ref, gffn_ref, wr_ref, br_ref,
                        x1_ref, h2_ref, ti_ref, tw_ref, sfin_ref, ptail_ref,
                        st_ref, uext_ref, qe_ref, ke_ref, kd_ref, v_ref, o_ref, p_ref, h_ref,
                        za_ref, zb_ref, plast_ref):
    rows = MIXER_ROWS
    n_chunks = rows // CHUNK

    def proj(seg):
        return _dot(h_ref[...], win_ref[:, seg * D_MODEL:(seg + 1) * D_MODEL])

    def phase_project():
        @pl.when(t == 0)
        def _():
            st_ref[...] = jnp.zeros_like(st_ref)
            uext_ref[0:POOL_CARRY, :] = jnp.zeros((POOL_CARRY, D_MODEL), F32)

        h_ref[...] = _rms(x_ref[...], gmix_ref[...]).astype(BF16)
        o_ref[...] = proj(SEG_Q)
        p_ref[...] = proj(SEG_F)
        v_ref[...] = proj(SEG_I).astype(BF16)

    def phase_hgrn_inputs():
        lb = _forget_lower_bound(lbl_ref[...])
        row_in_chunk = lax.broadcasted_iota(jnp.int32, (CHUNK, EW_LANES), 0)
        for c in range(n_chunks):
            rs = slice(c * CHUNK, (c + 1) * CHUNK)
            for l0 in range(0, D_MODEL, EW_LANES):
                ls = slice(l0, l0 + EW_LANES)
                q_raw = o_ref[rs, ls]
                q = q_raw * _sigmoid(q_raw)
                lbt = lb[:, ls]
                f = lbt + (1.0 - lbt) * _sigmoid(p_ref[rs, ls])
                k = 1.0 - f
                p = f
                s = 1
                while s < CHUNK:
                    p = p * jnp.where(row_in_chunk >= s, pltpu.roll(p, s, axis=0), 1.0)
                    s *= 2
                plast = p[CHUNK - 1:CHUNK, :]
                qe_ref[rs, ls] = (q * p).astype(BF16)
                ke_ref[rs, ls] = (k / p).astype(BF16)
                kd_ref[rs, ls] = (k * (plast / p)).astype(BF16)
                plast_ref[c:c + 1, ls] = plast

    def phase_hgrn():
        causal = (lax.broadcasted_iota(jnp.int32, (CHUNK, CHUNK), 0)
                  >= lax.broadcasted_iota(jnp.int32, (CHUNK, CHUNK), 1))
        for hd in range(HEADS):
            cols = slice(hd * HEAD_DIM, (hd + 1) * HEAD_DIM)
            st = st_ref[hd]
            for c in range(n_chunks):
                rs = slice(c * CHUNK, (c + 1) * CHUNK)
                qe = qe_ref[rs, cols]
                ke = ke_ref[rs, cols]
                kd = kd_ref[rs, cols]
                vv = v_ref[rs, cols]
                decay = plast_ref[c:c + 1, cols]
                scores = jnp.where(causal, _dot_nt(qe, ke), 0.0)
                o_ref[rs, cols] = _dot_nt(qe, st.astype(BF16)) + _dot(scores.astype(BF16), vv)
                st = st * decay + _dot_tn(vv, kd)
            st_ref[hd] = st

        @pl.when(t == last_t)
        def _():
            for hd in range(HEADS):
                sfin_ref[hd] = st_ref[hd].T

    def phase_pool_and_gates():
        uext_ref[POOL_CARRY:POOL_CARRY + rows, :] = proj(SEG_U)
        pos1 = t * rows + lax.broadcasted_iota(jnp.int32, (rows, 1), 0) + 1
        pooled = []
        for g, w in enumerate(POOL_WINDOWS):
            cols = slice(g * POOL_GW, (g + 1) * POOL_GW)
            sw = uext_ref[:, cols]
            s = 1
            while s < w:
                sw = sw + pltpu.roll(sw, s, axis=0)
                s *= 2
            inv_cnt = 1.0 / jnp.minimum(pos1, w).astype(F32)
            dg = sw[POOL_CARRY:, :] * inv_cnt - uext_ref[POOL_CARRY:POOL_CARRY + rows, cols]
            pooled.append(_dot(dg.astype(BF16), wpool_ref[g]))

        @pl.when(t == last_t)
        def _():
            ptail_ref[...] = uext_ref[rows + POOL_CARRY - POOL_BUF:rows + POOL_CARRY, :]

        uext_ref[0:POOL_CARRY, :] = uext_ref[rows:rows + POOL_CARRY, :]
        uext_ref[POOL_CARRY:POOL_CARRY + rows, :] = jnp.concatenate(pooled, axis=-1) * pscale_ref[...]
        p_ref[...] = proj(SEG_G)
        za_ref[...] = proj(SEG_A)
        zb_ref[...] = proj(SEG_B)

    def phase_merge():
        hgn = hgn_ref[...]
        for c in range(n_chunks):
            rs = slice(c * CHUNK, (c + 1) * CHUNK)
            ys = slice(POOL_CARRY + c * CHUNK, POOL_CARRY + (c + 1) * CHUNK)
            for hd in range(HEADS):
                cols = slice(hd * HEAD_DIM, (hd + 1) * HEAD_DIM)
                g_raw = p_ref[rs, cols]
                o = _rms(o_ref[rs, cols], hgn) * (g_raw * _sigmoid(g_raw))
                merged = _sigmoid(za_ref[rs, cols]) * o + _sigmoid(zb_ref[rs, cols]) * uext_ref[ys, cols]
                h_ref[rs, cols] = merged.astype(BF16)

    def phase_out_and_route():
        x1_ref[...] = x_ref[...] + _dot(h_ref[...], wout_ref[...])
        gffn = gffn_ref[...]
        for c in range(n_chunks):
            rs = slice(c * CHUNK, (c + 1) * CHUNK)
            h2 = _rms(x1_ref[rs, :], gffn)
            _store_row_tiles(h2_ref, (), h2, offset=c * CHUNK * ROW_TILE)
            qe_ref[rs, :], ke_ref[rs, :] = _split_bf16(h2)
        _route(qe_ref[...], ke_ref[...], wr_ref, br_ref, ti_ref, tw_ref)

    for phase in (phase_project, phase_hgrn_inputs, phase_hgrn, phase_pool_and_gates, phase_merge,
                  phase_out_and_route):
        phase()


def _const_spec(shape):
    zeros = (0,) * len(shape)
    return pl.BlockSpec(shape, lambda *_: zeros, pipeline_mode=pl.Buffered(1))


def _mixer_prompt(x, gmix, win, lbl, hgn, wpool, pscale, wout, gffn, wr, br, h2_tail):
    b, t, _ = x.shape
    rows = MIXER_ROWS
    nt = t // rows
    n_main = b * nt
    assert t % rows == 0 and h2_tail.shape[0] <= rows * ROW_TILE
    main = lambda s: jnp.minimum(s, n_main - 1)
    tok_spec = pl.BlockSpec((1, rows, D_MODEL), lambda s: (main(s) // nt, main(s) % nt, 0))
    flat_spec = pl.BlockSpec((rows * ROW_TILE, LANES), lambda s: (s, 0))
    lane_spec = pl.BlockSpec((rows, ROUTER_LANES), lambda s: (main(s), 0))
    return pl.pallas_call(
        functools.partial(_mixer_prompt_kernel, nt),
        grid=(n_main + 1,),
        in_specs=[
            tok_spec,
            _const_spec((1, D_MODEL)),
            _const_spec(win.shape),
            _const_spec(lbl.shape),
            _const_spec((1, HEAD_DIM)),
            _const_spec(wpool.shape),
            _const_spec((1, D_MODEL)),
            _const_spec(wout.shape),
            _const_spec((1, D_MODEL)),
            _const_spec(wr.shape),
            _const_spec(br.shape),
            _const_spec(h2_tail.shape),
        ],
        out_specs=[
            tok_spec,
            flat_spec,
            lane_spec,
            lane_spec,
            pl.BlockSpec((1, HEADS, HEAD_DIM, HEAD_DIM), lambda s: (main(s) // nt, 0, 0, 0)),
            pl.BlockSpec((1, POOL_BUF, D_MODEL), lambda s: (main(s) // nt, 0, 0)),
        ],
        out_shape=[
            jax.ShapeDtypeStruct(x.shape, F32),
            jax.ShapeDtypeStruct((b * t * ROW_TILE + h2_tail.shape[0], LANES), F32),
            jax.ShapeDtypeStruct((b * t, ROUTER_LANES), jnp.int32),
            jax.ShapeDtypeStruct((b * t, ROUTER_LANES), F32),
            jax.ShapeDtypeStruct((b, HEADS, HEAD_DIM, HEAD_DIM), F32),
            jax.ShapeDtypeStruct((b, POOL_BUF, D_MODEL), F32),
        ],
        scratch_shapes=[
            pltpu.VMEM((HEADS, HEAD_DIM, HEAD_DIM), F32),
            pltpu.VMEM((rows + POOL_CARRY, D_MODEL), F32),
            pltpu.VMEM((rows, D_MODEL), BF16),
            pltpu.VMEM((rows, D_MODEL), BF16),
            pltpu.VMEM((rows, D_MODEL), BF16),
            pltpu.VMEM((rows, D_MODEL), BF16),
            pltpu.VMEM((rows, D_MODEL), F32),
            pltpu.VMEM((rows, D_MODEL), F32),
            pltpu.VMEM((rows, D_MODEL), BF16),
            pltpu.VMEM((rows, D_MODEL), F32),
            pltpu.VMEM((rows, D_MODEL), F32),
            pltpu.VMEM((rows // CHUNK, D_MODEL), F32),
        ],
        compiler_params=pltpu.CompilerParams(dimension_semantics=("arbitrary",), vmem_limit_bytes=VMEM_LIMIT),
        name="mixer_prompt",
    )(x, gmix, win, lbl, hgn, wpool, pscale, wout, gffn, wr, br, h2_tail)


SAMPLE_GROUP = 8


def _mixer_sample_kernel(pool_cnt, x_ref, gmix_ref, win_ref, wqft_ref, lblt_ref, hgn_ref, wpool_ref, pscale_ref,
                         wout_ref, gffn_ref, wr_ref, br_ref, s_ref, pbuf_ref,
                         x1_ref, h2_ref, ti_ref, tw_ref, snew_ref, pnew_ref,
                         h_ref, znat_ref, o_ref, bsum_ref):
    step = pl.program_id(0)
    last = pl.num_programs(0) - 1
    r0 = pl.multiple_of(step * SAMPLE_GROUP, SAMPLE_GROUP)

    @pl.when(step == 0)
    def _():
        hf = _rms(x_ref[...], gmix_ref[...])
        h_ref[...] = hf
        h = hf.astype(BF16)
        for i, seg in enumerate((SEG_I, SEG_G, SEG_U, SEG_A, SEG_B)):
            znat_ref[:, i * D_MODEL:(i + 1) * D_MODEL] = _dot(h, win_ref[:, seg * D_MODEL:(seg + 1) * D_MODEL])

    hg = h_ref[pl.ds(r0, SAMPLE_GROUP), :].astype(BF16)
    qft = _dot_nt(wqft_ref[...], hg)
    lbl = lblt_ref[...]
    m = jnp.max(lbl, axis=1, keepdims=True)
    e = jnp.exp(lbl - m)
    lb = e[:, 0:1] / jnp.sum(e, axis=1, keepdims=True)
    q_raw = qft[0:D_MODEL]
    qt = q_raw * _sigmoid(q_raw)
    ft = lb + (1.0 - lb) * _sigmoid(qft[D_MODEL:2 * D_MODEL])
    kt = 1.0 - ft

    for j in range(SAMPLE_GROUP):
        v_row = znat_ref[pl.ds(r0 + j, 1), 0:D_MODEL]
        o_parts = []
        for hd in range(HEADS):
            rs = slice(hd * HEAD_DIM, (hd + 1) * HEAD_DIM)
            s_new = ft[rs, j:j + 1] * s_ref[j, hd] + kt[rs, j:j + 1] * v_row[:, rs]
            snew_ref[j, hd] = s_new
            o_parts.append(jnp.sum(qt[rs, j:j + 1] * s_new, axis=0, keepdims=True))
        o_ref[pl.ds(r0 + j, 1), :] = jnp.concatenate(o_parts, axis=-1)

    u_g = znat_ref[pl.ds(r0, SAMPLE_GROUP), 2 * D_MODEL:3 * D_MODEL]
    sums = []
    for g, w in enumerate(POOL_WINDOWS):
        acc = jnp.zeros((SAMPLE_GROUP, POOL_GW), F32)
        for j in range(1, w):
            row = POOL_BUF - j
            acc = acc + pbuf_ref[:, row * D_MODEL + g * POOL_GW:row * D_MODEL + (g + 1) * POOL_GW]
        sums.append(acc)
    bsum_ref[pl.ds(r0, SAMPLE_GROUP), :] = jnp.concatenate(sums, axis=-1)
    pnew_ref[:, 0:(POOL_BUF - 1) * D_MODEL] = pbuf_ref[:, D_MODEL:POOL_BUF * D_MODEL]
    pnew_ref[:, (POOL_BUF - 1) * D_MODEL:POOL_BUF * D_MODEL] = u_g

    @pl.when(step == last)
    def _():
        x = x_ref[...]
        g_raw = znat_ref[:, D_MODEL:2 * D_MODEL]
        u = znat_ref[:, 2 * D_MODEL:3 * D_MODEL]
        o = _head_norm_gate(o_ref[...], g_raw, hgn_ref[...])
        sw = bsum_ref[...] + u
        pooled = []
        for g, w in enumerate(POOL_WINDOWS):
            cols = slice(g * POOL_GW, (g + 1) * POOL_GW)
            dg = sw[:, cols] * (1.0 / pool_cnt[g]) - u[:, cols]
            pooled.append(_dot(dg.astype(BF16), wpool_ref[g]))
        y_pool = jnp.concatenate(pooled, axis=-1) * pscale_ref[...]
        merged = (_sigmoid(znat_ref[:, 3 * D_MODEL:4 * D_MODEL]) * o
                  + _sigmoid(znat_ref[:, 4 * D_MODEL:5 * D_MODEL]) * y_pool)
        x1 = x + _dot(merged.astype(BF16), wout_ref[...])
        x1_ref[...] = x1
        h2 = _rms(x1, gffn_ref[...])
        _store_row_tiles(h2_ref, (), h2)
        _route(*_split_bf16(h2), wr_ref, br_ref, ti_ref, tw_ref)


def _mixer_sample(x, gmix, win, wqft, lblt, hgn, wpool, pscale, wout, gffn, wr, br, state, pbuf, start_pos):
    n = x.shape[0]
    steps = n // SAMPLE_GROUP
    pool_cnt = tuple(float(min(start_pos + 1, w)) for w in POOL_WINDOWS)
    pbuf2 = pbuf.reshape(n, POOL_BUF * D_MODEL)
    full = _const_spec
    out = pl.pallas_call(
        functools.partial(_mixer_sample_kernel, pool_cnt),
        grid=(steps,),
        in_specs=[
            full((n, D_MODEL)),
            full((1, D_MODEL)),
            full(win.shape),
            full(wqft.shape),
            full(lblt.shape),
            full((1, HEAD_DIM)),
            full(wpool.shape),
            full((1, D_MODEL)),
            full(wout.shape),
            full((1, D_MODEL)),
            full(wr.shape),
            full(br.shape),
            pl.BlockSpec((SAMPLE_GROUP, HEADS, HEAD_DIM, HEAD_DIM), lambda i: (i, 0, 0, 0)),
            pl.BlockSpec((SAMPLE_GROUP, POOL_BUF * D_MODEL), lambda i: (i, 0)),
        ],
        out_specs=[
            pl.BlockSpec((n, D_MODEL), lambda i: (0, 0)),
            pl.BlockSpec((n * ROW_TILE, LANES), lambda i: (0, 0)),
            pl.BlockSpec((n, ROUTER_LANES), lambda i: (0, 0)),
            pl.BlockSpec((n, ROUTER_LANES), lambda i: (0, 0)),
            pl.BlockSpec((SAMPLE_GROUP, HEADS, HEAD_DIM, HEAD_DIM), lambda i: (i, 0, 0, 0)),
            pl.BlockSpec((SAMPLE_GROUP, POOL_BUF * D_MODEL), lambda i: (i, 0)),
        ],
        out_shape=[
            jax.ShapeDtypeStruct((n, D_MODEL), F32),
            jax.ShapeDtypeStruct((n * ROW_TILE, LANES), F32),
            jax.ShapeDtypeStruct((n, ROUTER_LANES), jnp.int32),
            jax.ShapeDtypeStruct((n, ROUTER_LANES), F32),
            jax.ShapeDtypeStruct(state.shape, F32),
            jax.ShapeDtypeStruct(pbuf2.shape, F32),
        ],
        scratch_shapes=[
            pltpu.VMEM((n, D_MODEL), F32),
            pltpu.VMEM((n, 5 * D_MODEL), F32),
            pltpu.VMEM((n, D_MODEL), F32),
            pltpu.VMEM((n, D_MODEL), F32),
        ],
        compiler_params=pltpu.CompilerParams(dimension_semantics=("arbitrary",), vmem_limit_bytes=VMEM_LIMIT),
        name="mixer_sample",
    )(x, gmix, win, wqft, lblt, hgn, wpool, pscale, wout, gffn, wr, br, state, pbuf2)
    x1, h2, ti, tw, snew, pnew = out
    return x1, h2, ti, tw, snew, pnew.reshape(n, POOL_BUF, D_MODEL)


def _moe_kernel(n_tokens, be_ref, off_ref, nv_ref, nu_ref, sa_ref, h2_hbm,
                wg_ref, bg_ref, wu_ref, bu_ref, wd_ref, bd_ref, y_hbm,
                xbuf0, xbuf1, xbuf2, ybuf0, ybuf1, ybuf2, wgb, wub, wdb, xb_ref, gate_ref, act_ref, gsem, ssem):
    rows = MOE_ROWS
    i = pl.program_id(0)
    n_used = nu_ref[0]
    n_blocks = pl.num_programs(0)
    nv_back = lambda k: jnp.where(i >= k, nv_ref[jnp.maximum(i - k, 0)], 0)
    prev = jnp.maximum(i - 1, 0)
    nv_prev = nv_back(1)
    xbuf = (xbuf0, xbuf1, xbuf2)
    ybuf = (ybuf0, ybuf1, ybuf2)

    def tile_rows(first, n=1):
        start = first * ROW_TILE
        return pl.ds(start if isinstance(first, int) else pl.multiple_of(start, ROW_TILE), n * ROW_TILE)

    def gather_copy(a, r, s):
        tok = a >> 2
        return pltpu.make_async_copy(h2_hbm.at[tile_rows(tok), :], xbuf[s].at[tile_rows(r), :], gsem.at[s])

    def scatter_copy(a, r, n, s):
        dst = (a & (TOP_K - 1)) * n_tokens + (a >> 2)
        return pltpu.make_async_copy(ybuf[s].at[tile_rows(r, n), :], y_hbm.at[tile_rows(dst, n), :], ssem.at[s])

    def wait_gather(s):
        pltpu.make_async_copy(h2_hbm.at[tile_rows(0, rows), :], xbuf[s], gsem.at[s]).wait()

    def wait_scatter(n, s):
        size = rows
        while size >= 1:
            @pl.when((n & size) != 0)
            def _():
                scatter_copy(0, 0, size, s).wait()
            size //= 2

    def scatter_loop(blk, n, s):
        def body(r, c):
            scatter_copy(sa_ref[off_ref[blk] + r], r, 1, s).start()
            return c
        lax.fori_loop(0, n, body, 0)

    n_phases = 3

    def start_copies(phase, cur, full_prev):
        ahead, behind = (cur + 2) % MOE_DEPTH, (cur + MOE_DEPTH - 1) % MOE_DEPTH
        off_ahead = off_ref[jnp.minimum(i + 2, n_blocks - 1)]
        off_prev = off_ref[prev]
        for r in range(rows * phase // n_phases, rows * (phase + 1) // n_phases):
            gather_copy(sa_ref[off_ahead + r], r, ahead).start()
            if full_prev:
                scatter_copy(sa_ref[off_prev + r], r, 1, behind).start()

    def phase_load(cur, full_prev):
        wait_gather(cur)
        xb_ref[...] = _load_row_tiles(xbuf[cur], (), rows).astype(BF16)

    def phase_gate(cur, full_prev):
        start_copies(0, cur, full_prev)
        gate_ref[...] = jnp.minimum(_dot(xb_ref[...], wgb[...]) + bg_ref[0], SWIGLU_LIMIT)

    def phase_up(cur, full_prev):
        start_copies(1, cur, full_prev)
        up = jnp.clip(_dot(xb_ref[...], wub[...]) + bu_ref[0], -SWIGLU_LIMIT, SWIGLU_LIMIT)
        gate = gate_ref[...]
        act_ref[...] = ((up + 1.0) * gate * _sigmoid(SWIGLU_ALPHA * gate)).astype(BF16)

    def phase_down(cur, full_prev):
        nxt, behind = (cur + 1) % MOE_DEPTH, (cur + MOE_DEPTH - 1) % MOE_DEPTH
        start_copies(2, cur, full_prev)
        wait_scatter(nv_back(MOE_DEPTH), cur)
        _store_row_tiles(ybuf[cur], (), _dot(act_ref[...], wdb[...]) + bd_ref[0])
        if not full_prev:
            scatter_loop(prev, nv_prev, behind)

        @pl.when(i == n_used - 1)
        def _():
            scatter_loop(i, nv_ref[i], cur)
            wait_scatter(nv_back(2), nxt)
            wait_scatter(nv_prev, behind)
            wait_scatter(nv_ref[i], cur)
            wait_gather(nxt)
            wait_gather((cur + 2) % MOE_DEPTH)

    @pl.when(i == 0)
    def _():
        for blk in range(MOE_DEPTH - 1):
            def body(r, c):
                gather_copy(sa_ref[off_ref[blk] + r], r, blk).start()
                return c
            lax.fori_loop(0, rows, body, 0)

    @pl.when(i < n_used)
    def _():
        first_of_expert = jnp.logical_or(i == 0, be_ref[i] != be_ref[prev])

        @pl.when(first_of_expert)
        def _():
            wgb[...] = wg_ref[0].astype(BF16)
            wub[...] = wu_ref[0].astype(BF16)
            wdb[...] = wd_ref[0].astype(BF16)

        prev_is_full = nv_prev == rows
        for phase in (phase_load, phase_gate, phase_up, phase_down):
            for cur in range(MOE_DEPTH):
                for full_prev in (True, False):
                    full_cond = prev_is_full if full_prev else jnp.logical_not(prev_is_full)

                    @pl.when(jnp.logical_and(i % MOE_DEPTH == cur, full_cond))
                    def _():
                        phase(cur, full_prev)


def _moe(h2, routing, wg, bg, wu, bu, wd, bd):
    rows = MOE_ROWS
    n_tokens = h2.shape[0] // ROW_TILE
    block_e, block_off, block_nv, n_used, sorted_a = routing
    n_blocks = block_e.shape[0]
    w_spec = pl.BlockSpec((1, D_MODEL, D_MODEL), lambda i, be, *_: (be[i], 0, 0))
    b_spec = pl.BlockSpec((1, 1, D_MODEL), lambda i, be, *_: (be[i], 0, 0))
    grid_spec = pltpu.PrefetchScalarGridSpec(
        num_scalar_prefetch=5,
        grid=(n_blocks,),
        in_specs=[pl.BlockSpec(memory_space=pl.ANY), w_spec, b_spec, w_spec, b_spec, w_spec, b_spec],
        out_specs=pl.BlockSpec(memory_space=pl.ANY),
        scratch_shapes=[
            *[pltpu.VMEM((rows * ROW_TILE, LANES), F32)] * (2 * MOE_DEPTH),
            pltpu.VMEM((D_MODEL, D_MODEL), BF16),
            pltpu.VMEM((D_MODEL, D_MODEL), BF16),
            pltpu.VMEM((D_MODEL, D_MODEL), BF16),
            pltpu.VMEM((rows, D_MODEL), BF16),
            pltpu.VMEM((rows, D_MODEL), F32),
            pltpu.VMEM((rows, D_MODEL), BF16),
            pltpu.SemaphoreType.DMA((MOE_DEPTH,)),
            pltpu.SemaphoreType.DMA((MOE_DEPTH,)),
        ],
    )
    return pl.pallas_call(
        functools.partial(_moe_kernel, n_tokens),
        grid_spec=grid_spec,
        out_shape=jax.ShapeDtypeStruct((n_tokens * TOP_K * ROW_TILE, LANES), F32),
        compiler_params=pltpu.CompilerParams(dimension_semantics=("arbitrary",), vmem_limit_bytes=VMEM_LIMIT),
        name="moe_experts",
    )(block_e, block_off, block_nv, n_used, sorted_a, h2, wg, bg, wu, bu, wd, bd)


ASSIGN_BITS = 17


def _moe_routing(top_i):
    rows = MOE_ROWS
    n_assign = top_i.shape[0] * TOP_K
    assert n_assign <= 1 << ASSIGN_BITS
    n_blocks = -(-n_assign // rows) + N_EXPERTS
    flat_e = top_i.reshape(-1)
    keys = jnp.sort(flat_e * (1 << ASSIGN_BITS) + jnp.arange(n_assign, dtype=jnp.int32))
    smem_len = -(-(n_assign + rows) // 1024) * 1024
    sorted_a = jnp.pad(keys & ((1 << ASSIGN_BITS) - 1), (0, smem_len - n_assign))
    experts = jnp.arange(N_EXPERTS, dtype=jnp.int32)
    counts = jnp.sum((flat_e[:, None] == experts[None, :]).astype(jnp.int32), axis=0)
    nblk = (counts + rows - 1) // rows
    blk_end = jnp.cumsum(nblk)
    blk_start = blk_end - nblk
    row_start = jnp.cumsum(counts) - counts
    blk = jnp.arange(n_blocks, dtype=jnp.int32)
    owner = (blk[:, None] >= blk_start[None, :]) & (blk[:, None] < blk_end[None, :])
    pick = lambda v: jnp.sum(jnp.where(owner, v[None, :], 0), axis=1)
    within = (blk - pick(blk_start)) * rows
    used = blk < blk_end[-1]
    block_e = jnp.where(used, pick(experts), N_EXPERTS - 1).astype(jnp.int32)
    block_off = jnp.where(used, pick(row_start) + within, 0).astype(jnp.int32)
    block_nv = jnp.where(used, jnp.clip(pick(counts) - within, 0, rows), 0).astype(jnp.int32)
    n_used = blk_end[-1].astype(jnp.int32).reshape(1)
    return block_e, block_off, block_nv, n_used, sorted_a.astype(jnp.int32)


def _final_kernel(x1_ref, y_ref, tw_ref, p_ref, gple_ref, wpg_ref, wpp_ref, gfin_ref, out_ref):
    tw = tw_ref[...]
    rows = tw.shape[0]
    x = x1_ref[...].reshape(rows, D_MODEL)
    for j in range(TOP_K):
        x = x + tw[:, j:j + 1] * _load_row_tiles(y_ref, (j,), rows)
    gate = _sigmoid(_dot(_rms(x, gple_ref[...]).astype(BF16), wpg_ref[...]))
    x = x + gate * _dot(p_ref[...].reshape(rows, -1).astype(BF16), wpp_ref[...])
    out_ref[...] = _rms(x, gfin_ref[...]).reshape(out_ref.shape)


def _final(x1, y4, tw, p, gple, wpg, wpp, gfin, rows, tok_offset):
    g, t, _ = x1.shape
    nt = t // rows
    off = tok_offset // rows
    assert off * rows == tok_offset and t % rows == 0
    ple = p.shape[-1]
    tok_map = lambda i: (i // nt, i % nt, 0)
    return pl.pallas_call(
        _final_kernel,
        grid=(g * nt,),
        in_specs=[
            pl.BlockSpec((1, rows, D_MODEL), tok_map),
            pl.BlockSpec((TOP_K, rows * ROW_TILE, LANES), lambda i: (0, i + off, 0)),
            pl.BlockSpec((rows, ROUTER_LANES), lambda i: (i, 0)),
            pl.BlockSpec((1, rows, ple), tok_map),
            _const_spec((1, D_MODEL)),
            _const_spec(wpg.shape),
            _const_spec(wpp.shape),
            _const_spec((1, D_MODEL)),
        ],
        out_specs=pl.BlockSpec((1, rows, D_MODEL), tok_map),
        out_shape=jax.ShapeDtypeStruct(x1.shape, F32),
        compiler_params=pltpu.CompilerParams(dimension_semantics=("arbitrary",), vmem_limit_bytes=VMEM_LIMIT),
        name="final",
    )(x1, y4, tw, p, gple, wpg, wpp, gfin)


def kernel(x_prompt, x_sample, p_prompt, p_sample, state_hgrn, state_pool, g_mix, w_in, hg_lb_logits, hg_norm, w_pool,
           pool_scale, w_out, g_ffn, w_router, b_router, w_gate, b_gate, w_up, b_up, w_down, b_down, g_ple,
           w_ple_gate, w_ple_proj, g_final):
    depth = w_in.shape[0]
    assert depth == 1, "single-layer step"
    b, t, _ = x_prompt.shape
    ns = x_sample.shape[0]
    assert x_sample.shape[1] == 1
    n_prompt = b * t
    n_tok = n_prompt + ns

    row = lambda a: a.reshape(1, -1)
    win = w_in[0].astype(BF16)
    wqft = w_in[0, :, :2 * D_MODEL].T.astype(BF16)
    lbl = hg_lb_logits.astype(F32)
    wpool = w_pool[0].astype(BF16)
    wout = w_out[0].astype(BF16)
    wr = jnp.pad(w_router[0], ((0, 0), (0, ROUTER_LANES - N_EXPERTS)))
    br = jnp.pad(b_router[0], (0, ROUTER_LANES - N_EXPERTS)).reshape(1, -1)
    shared = (row(hg_norm[0]), wpool, row(pool_scale[0]), wout, row(g_ffn[0]), wr, br)

    x1_s, h2_s, ti_s, tw_s, s_s, pool_s = _mixer_sample(
        x_sample.reshape(ns, D_MODEL), row(g_mix[0]), win, wqft, lbl.T, *shared,
        state_hgrn[0], state_pool[0], start_pos=PAST_LEN)
    x1_p, h2, ti_p, tw_p, s_p, pool_p = _mixer_prompt(x_prompt, row(g_mix[0]), win, lbl, *shared, h2_s)

    top_i = jnp.concatenate([ti_p[:, :TOP_K], ti_s[:, :TOP_K]], axis=0)
    routing = _moe_routing(top_i)
    e3 = lambda a: a[0].reshape(N_EXPERTS, 1, D_MODEL)
    y4 = _moe(h2, routing, w_gate[0], e3(b_gate), w_up[0], e3(b_up), w_down[0], e3(b_down))
    y4 = y4.reshape(TOP_K, n_tok * ROW_TILE, LANES)

    wpg = w_ple_gate[0].astype(BF16)
    wpp = w_ple_proj[0].astype(BF16)
    fin = (row(g_ple[0]), wpg, wpp, row(g_final))
    y_p = _final(x1_p, y4, tw_p, p_prompt[0], *fin, rows=FINAL_ROWS, tok_offset=0)
    y_s = _final(x1_s[None], y4, tw_s, p_sample[0].reshape(1, ns, -1), *fin, rows=ns, tok_offset=n_prompt)

    return (y_p, y_s.reshape(ns, 1, D_MODEL), s_p[None], pool_p[None], s_s[None], pool_s[None])
```

```python
import functools

import jax
import jax.numpy as jnp
from jax import lax
from jax.experimental import pallas as pl
from jax.experimental.pallas import tpu as pltpu

F32 = jnp.float32
BF16 = jnp.bfloat16

D_MODEL = 1024
HEADS = 8
HEAD_DIM = 128
CHUNK = 32
POOL_WINDOWS = (2, 4, 8, 16)
POOL_GW = D_MODEL // len(POOL_WINDOWS)
POOL_BUF = 15
POOL_CARRY = 16
N_EXPERTS = 32
TOP_K = 4
ROUTER_LANES = 128
SWIGLU_LIMIT = 7.0
SWIGLU_ALPHA = 1.702
EPS = 1e-6
PAST_LEN = 16384

SEG_Q, SEG_F, SEG_I, SEG_G, SEG_U, SEG_A, SEG_B = range(7)

MIXER_ROWS = 512
EW_LANES = 256
MOE_ROWS = 512
MOE_DEPTH = 3
FINAL_ROWS = 512
VMEM_LIMIT = 56 * 1024 * 1024


def _rms(x, g):
    ms = jnp.mean(x * x, axis=-1, keepdims=True)
    return x * lax.rsqrt(ms + EPS) * g


def _sigmoid(x):
    return 0.5 * jnp.tanh(0.5 * x) + 0.5


def _dot(a, b):
    return jnp.dot(a, b, preferred_element_type=F32)


def _dot_nt(a, b):
    return lax.dot_general(a, b, (((1,), (1,)), ((), ())), preferred_element_type=F32)


def _dot_tn(a, b):
    return lax.dot_general(a, b, (((0,), (0,)), ((), ())), preferred_element_type=F32)


LANES = 128
ROW_TILE = D_MODEL // LANES


def _store_row_tiles(ref, lead, x, period=ROW_TILE, offset=0):
    rows = x.shape[0]
    for c in range(ROW_TILE):
        ref[lead + (pl.ds(offset + c, rows, stride=period), slice(None))] = x[:, c * LANES:(c + 1) * LANES]


def _load_row_tiles(ref, lead, rows, period=ROW_TILE, offset=0):
    return jnp.concatenate(
        [ref[lead + (pl.ds(offset + c, rows, stride=period), slice(None))] for c in range(ROW_TILE)], axis=-1)


def _split_bf16(x):
    hi = x.astype(BF16)
    lo = (x - hi.astype(F32)).astype(BF16)
    return hi, lo


def _forget_lower_bound(lbl):
    m = jnp.max(lbl, axis=0, keepdims=True)
    e = jnp.exp(lbl - m)
    return e[0:1] / jnp.sum(e, axis=0, keepdims=True)


def _head_norm_gate(o, g_raw, hgn):
    parts = []
    for h in range(HEADS):
        oh = o[:, h * HEAD_DIM:(h + 1) * HEAD_DIM]
        parts.append(_rms(oh, hgn))
    return jnp.concatenate(parts, axis=-1) * (g_raw * _sigmoid(g_raw))


def _route(h_hi, h_lo, wr_ref, br_ref, ti_ref, tw_ref):
    rows = h_hi.shape[0]
    w_hi, w_lo = _split_bf16(wr_ref[...])
    logits = _dot(h_hi, w_hi) + _dot(h_lo, w_hi) + _dot(h_hi, w_lo) + br_ref[...]
    lane = lax.broadcasted_iota(jnp.int32, (rows, ROUTER_LANES), 1)
    neg = jnp.float32(-jnp.inf)
    l = jnp.where(lane < N_EXPERTS, logits, neg)
    ti = jnp.zeros((rows, ROUTER_LANES), jnp.int32)
    tw = jnp.zeros((rows, ROUTER_LANES), F32)
    m0 = None
    denom = None
    es = []
    for j in range(TOP_K):
        m = jnp.max(l, axis=-1, keepdims=True)
        idx = jnp.min(jnp.where(l == m, lane, ROUTER_LANES), axis=-1, keepdims=True)
        l = jnp.where(lane == idx, neg, l)
        if j == 0:
            m0 = m
        e = jnp.exp(m - m0)
        es.append(e)
        denom = e if denom is None else denom + e
        ti = jnp.where(lane == j, idx, ti)
    for j in range(TOP_K):
        tw = jnp.where(lane == j, es[j] / denom, tw)
    ti_ref[...] = ti
    tw_ref[...] = tw


def _mixer_prompt_kernel(nt, x_ref, gmix_ref, win_ref, lbl_ref, hgn_ref, wpool_ref, pscale_ref, wout_ref, gffn_ref,
                         wr_ref, br_ref, h2s_ref, x1_ref, h2_ref, ti_ref, tw_ref, sfin_ref, ptail_ref, *scratch):
    s = pl.program_id(0)
    n_main = pl.num_programs(0) - 1

    @pl.when(s < n_main)
    def _():
        _mixer_prompt_block(s % nt, nt - 1, x_ref.at[0], gmix_ref, win_ref, lbl_ref, hgn_ref, wpool_ref, pscale_ref,
                            wout_ref, gffn_ref, wr_ref, br_ref,
                            x1_ref.at[0], h2_ref, ti_ref, tw_ref, sfin_ref.at[0], ptail_ref.at[0], *scratch)

    @pl.when(s == n_main)
    def _():
        h2_ref[0:h2s_ref.shape[0], :] = h2s_ref[...]


def _mixer_prompt_block(t, last_t, x_ref, gmix_ref, win_ref, lbl_ref, hgn_ref, wpool_ref, pscale_ref,
                        wout_ref, gffn_ref, wr_ref, br_ref,
                        x1_ref, h2_ref, ti_ref, tw_ref, sfin_ref, ptail_ref,
                        st_ref, uext_ref, qe_ref, ke_ref, kd_ref, v_ref, o_ref, p_ref, h_ref,
                        za_ref, zb_ref, plast_ref):
    rows = MIXER_ROWS
    n_chunks = rows // CHUNK

    def proj(seg):
        return _dot(h_ref[...], win_ref[:, seg * D_MODEL:(seg + 1) * D_MODEL])

    def phase_project():
        @pl.when(t == 0)
        def _():
            st_ref[...] = jnp.zeros_like(st_ref)
            uext_ref[0:POOL_CARRY, :] = jnp.zeros((POOL_CARRY, D_MODEL), F32)

        h_ref[...] = _rms(x_ref[...], gmix_ref[...]).astype(BF16)
        o_ref[...] = proj(SEG_Q)
        p_ref[...] = proj(SEG_F)
        v_ref[...] = proj(SEG_I).astype(BF16)

    def phase_hgrn_inputs():
        lb = _forget_lower_bound(lbl_ref[...])
        row_in_chunk = lax.broadcasted_iota(jnp.int32, (CHUNK, EW_LANES), 0)
        for c in range(n_chunks):
            rs = slice(c * CHUNK, (c + 1) * CHUNK)
            for l0 in range(0, D_MODEL, EW_LANES):
                ls = slice(l0, l0 + EW_LANES)
                q_raw = o_ref[rs, ls]
                q = q_raw * _sigmoid(q_raw)
                lbt = lb[:, ls]
                f = lbt + (1.0 - lbt) * _sigmoid(p_ref[rs, ls])
                k = 1.0 - f
                p = f
                s = 1
                while s < CHUNK:
                    p = p * jnp.where(row_in_chunk >= s, pltpu.roll(p, s, axis=0), 1.0)
                    s *= 2
                plast = p[CHUNK - 1:CHUNK, :]
                qe_ref[rs, ls] = (q * p).astype(BF16)
                ke_ref[rs, ls] = (k / p).astype(BF16)
                kd_ref[rs, ls] = (k * (plast / p)).astype(BF16)
                plast_ref[c:c + 1, ls] = plast

    def phase_hgrn():
        causal = (lax.broadcasted_iota(jnp.int32, (CHUNK, CHUNK), 0)
                  >= lax.broadcasted_iota(jnp.int32, (CHUNK, CHUNK), 1))
        for hd in range(HEADS):
            cols = slice(hd * HEAD_DIM, (hd + 1) * HEAD_DIM)
            st = st_ref[hd]
            for c in range(n_chunks):
                rs = slice(c * CHUNK, (c + 1) * CHUNK)
                qe = qe_ref[rs, cols]
                ke = ke_ref[rs, cols]
                kd = kd_ref[rs, cols]
                vv = v_ref[rs, cols]
                decay = plast_ref[c:c + 1, cols]
                scores = jnp.where(causal, _dot_nt(qe, ke), 0.0)
                o_ref[rs, cols] = _dot_nt(qe, st.astype(BF16)) + _dot(scores.astype(BF16), vv)
                st = st * decay + _dot_tn(vv, kd)
            st_ref[hd] = st

        @pl.when(t == last_t)
        def _():
            for hd in range(HEADS):
                sfin_ref[hd] = st_ref[hd].T

    def phase_pool_and_gates():
        uext_ref[POOL_CARRY:POOL_CARRY + rows, :] = proj(SEG_U)
        pos1 = t * rows + lax.broadcasted_iota(jnp.int32, (rows, 1), 0) + 1
        pooled = []
        for g, w in enumerate(POOL_WINDOWS):
            cols = slice(g * POOL_GW, (g + 1) * POOL_GW)
            sw = uext_ref[:, cols]
            s = 1
            while s < w:
                sw = sw + pltpu.roll(sw, s, axis=0)
                s *= 2
            inv_cnt = 1.0 / jnp.minimum(pos1, w).astype(F32)
            dg = sw[POOL_CARRY:, :] * inv_cnt - uext_ref[POOL_CARRY:POOL_CARRY + rows, cols]
            pooled.append(_dot(dg.astype(BF16), wpool_ref[g]))

        @pl.when(t == last_t)
        def _():
            ptail_ref[...] = uext_ref[rows + POOL_CARRY - POOL_BUF:rows + POOL_CARRY, :]

        uext_ref[0:POOL_CARRY, :] = uext_ref[rows:rows + POOL_CARRY, :]
        uext_ref[POOL_CARRY:POOL_CARRY + rows, :] = jnp.concatenate(pooled, axis=-1) * pscale_ref[...]
        p_ref[...] = proj(SEG_G)
        za_ref[...] = proj(SEG_A)
        zb_ref[...] = proj(SEG_B)

    def phase_merge():
        hgn = hgn_ref[...]
        for c in range(n_chunks):
            rs = slice(c * CHUNK, (c + 1) * CHUNK)
            ys = slice(POOL_CARRY + c * CHUNK, POOL_CARRY + (c + 1) * CHUNK)
            for hd in range(HEADS):
                cols = slice(hd * HEAD_DIM, (hd + 1) * HEAD_DIM)
                g_raw = p_ref[rs, cols]
                o = _rms(o_ref[rs, cols], hgn) * (g_raw * _sigmoid(g_raw))
                merged = _sigmoid(za_ref[rs, cols]) * o + _sigmoid(zb_ref[rs, cols]) * uext_ref[ys, cols]
                h_ref[rs, cols] = merged.astype(BF16)

    def phase_out_and_route():
        x1_ref[...] = x_ref[...] + _dot(h_ref[...], wout_ref[...])
        gffn = gffn_ref[...]
        for c in range(n_chunks):
            rs = slice(c * CHUNK, (c + 1) * CHUNK)
            h2 = _rms(x1_ref[rs, :], gffn)
            _store_row_tiles(h2_ref, (), h2, offset=c * CHUNK * ROW_TILE)
            qe_ref[rs, :], ke_ref[rs, :] = _split_bf16(h2)
        _route(qe_ref[...], ke_ref[...], wr_ref, br_ref, ti_ref, tw_ref)

    for phase in (phase_project, phase_hgrn_inputs, phase_hgrn, phase_pool_and_gates, phase_merge,
                  phase_out_and_route):
        phase()


def _const_spec(shape):
    zeros = (0,) * len(shape)
    return pl.BlockSpec(shape, lambda *_: zeros, pipeline_mode=pl.Buffered(1))


def _mixer_prompt(x, gmix, win, lbl, hgn, wpool, pscale, wout, gffn, wr, br, h2_tail):
    b, t, _ = x.shape
    rows = MIXER_ROWS
    nt = t // rows
    n_main = b * nt
    assert t % rows == 0 and h2_tail.shape[0] <= rows * ROW_TILE
    main = lambda s: jnp.minimum(s, n_main - 1)
    tok_spec = pl.BlockSpec((1, rows, D_MODEL), lambda s: (main(s) // nt, main(s) % nt, 0))
    flat_spec = pl.BlockSpec((rows * ROW_TILE, LANES), lambda s: (s, 0))
    lane_spec = pl.BlockSpec((rows, ROUTER_LANES), lambda s: (main(s), 0))
    return pl.pallas_call(
        functools.partial(_mixer_prompt_kernel, nt),
        grid=(n_main + 1,),
        in_specs=[
            tok_spec,
            _const_spec((1, D_MODEL)),
            _const_spec(win.shape),
            _const_spec(lbl.shape),
            _const_spec((1, HEAD_DIM)),
            _const_spec(wpool.shape),
            _const_spec((1, D_MODEL)),
            _const_spec(wout.shape),
            _const_spec((1, D_MODEL)),
            _const_spec(wr.shape),
            _const_spec(br.shape),
            _const_spec(h2_tail.shape),
        ],
        out_specs=[
            tok_spec,
            flat_spec,
            lane_spec,
            lane_spec,
            pl.BlockSpec((1, HEADS, HEAD_DIM, HEAD_DIM), lambda s: (main(s) // nt, 0, 0, 0)),
            pl.BlockSpec((1, POOL_BUF, D_MODEL), lambda s: (main(s) // nt, 0, 0)),
        ],
        out_shape=[
            jax.ShapeDtypeStruct(x.shape, F32),
            jax.ShapeDtypeStruct((b * t * ROW_TILE + h2_tail.shape[0], LANES), F32),
            jax.ShapeDtypeStruct((b * t, ROUTER_LANES), jnp.int32),
            jax.ShapeDtypeStruct((b * t, ROUTER_LANES), F32),
            jax.ShapeDtypeStruct((b, HEADS, HEAD_DIM, HEAD_DIM), F32),
            jax.ShapeDtypeStruct((b, POOL_BUF, D_MODEL), F32),
        ],
        scratch_shapes=[
            pltpu.VMEM((HEADS, HEAD_DIM, HEAD_DIM), F32),
            pltpu.VMEM((rows + POOL_CARRY, D_MODEL), F32),
            pltpu.VMEM((rows, D_MODEL), BF16),
            pltpu.VMEM((rows, D_MODEL), BF16),
            pltpu.VMEM((rows, D_MODEL), BF16),
            pltpu.VMEM((rows, D_MODEL), BF16),
            pltpu.VMEM((rows, D_MODEL), F32),
            pltpu.VMEM((rows, D_MODEL), F32),
            pltpu.VMEM((rows, D_MODEL), BF16),
            pltpu.VMEM((rows, D_MODEL), F32),
            pltpu.VMEM((rows, D_MODEL), F32),
            pltpu.VMEM((rows // CHUNK, D_MODEL), F32),
        ],
        compiler_params=pltpu.CompilerParams(dimension_semantics=("arbitrary",), vmem_limit_bytes=VMEM_LIMIT),
        name="mixer_prompt",
    )(x, gmix, win, lbl, hgn, wpool, pscale, wout, gffn, wr, br, h2_tail)


SAMPLE_GROUP = 8


def _mixer_sample_kernel(pool_cnt, x_ref, gmix_ref, win_ref, wqft_ref, lblt_ref, hgn_ref, wpool_ref, pscale_ref,
                         wout_ref, gffn_ref, wr_ref, br_ref, s_ref, pbuf_ref,
                         x1_ref, h2_ref, ti_ref, tw_ref, snew_ref, pnew_ref,
                         h_ref, znat_ref, o_ref, bsum_ref):
    step = pl.program_id(0)
    last = pl.num_programs(0) - 1
    r0 = pl.multiple_of(step * SAMPLE_GROUP, SAMPLE_GROUP)

    @pl.when(step == 0)
    def _():
        hf = _rms(x_ref[...], gmix_ref[...])
        h_ref[...] = hf
        h = hf.astype(BF16)
        for i, seg in enumerate((SEG_I, SEG_G, SEG_U, SEG_A, SEG_B)):
            znat_ref[:, i * D_MODEL:(i + 1) * D_MODEL] = _dot(h, win_ref[:, seg * D_MODEL:(seg + 1) * D_MODEL])

    hg = h_ref[pl.ds(r0, SAMPLE_GROUP), :].astype(BF16)
    qft = _dot_nt(wqft_ref[...], hg)
    lbl = lblt_ref[...]
    m = jnp.max(lbl, axis=1, keepdims=True)
    e = jnp.exp(lbl - m)
    lb = e[:, 0:1] / jnp.sum(e, axis=1, keepdims=True)
    q_raw = qft[0:D_MODEL]
    qt = q_raw * _sigmoid(q_raw)
    ft = lb + (1.0 - lb) * _sigmoid(qft[D_MODEL:2 * D_MODEL])
    kt = 1.0 - ft

    for j in range(SAMPLE_GROUP):
        v_row = znat_ref[pl.ds(r0 + j, 1), 0:D_MODEL]
        o_parts = []
        for hd in range(HEADS):
            rs = slice(hd * HEAD_DIM, (hd + 1) * HEAD_DIM)
            s_new = ft[rs, j:j + 1] * s_ref[j, hd] + kt[rs, j:j + 1] * v_row[:, rs]
            snew_ref[j, hd] = s_new
            o_parts.append(jnp.sum(qt[rs, j:j + 1] * s_new, axis=0, keepdims=True))
        o_ref[pl.ds(r0 + j, 1), :] = jnp.concatenate(o_parts, axis=-1)

    u_g = znat_ref[pl.ds(r0, SAMPLE_GROUP), 2 * D_MODEL:3 * D_MODEL]
    sums = []
    for g, w in enumerate(POOL_WINDOWS):
        acc = jnp.zeros((SAMPLE_GROUP, POOL_GW), F32)
        for j in range(1, w):
            row = POOL_BUF - j
            acc = acc + pbuf_ref[:, row * D_MODEL + g * POOL_GW:row * D_MODEL + (g + 1) * POOL_GW]
        sums.append(acc)
    bsum_ref[pl.ds(r0, SAMPLE_GROUP), :] = jnp.concatenate(sums, axis=-1)
    pnew_ref[:, 0:(POOL_BUF - 1) * D_MODEL] = pbuf_ref[:, D_MODEL:POOL_BUF * D_MODEL]
    pnew_ref[:, (POOL_BUF - 1) * D_MODEL:POOL_BUF * D_MODEL] = u_g

    @pl.when(step == last)
    def _():
        x = x_ref[...]
        g_raw = znat_ref[:, D_MODEL:2 * D_MODEL]
        u = znat_ref[:, 2 * D_MODEL:3 * D_MODEL]
        o = _head_norm_gate(o_ref[...], g_raw, hgn_ref[...])
        sw = bsum_ref[...] + u
        pooled = []
        for g, w in enumerate(POOL_WINDOWS):
            cols = slice(g * POOL_GW, (g + 1) * POOL_GW)
            dg = sw[:, cols] * (1.0 / pool_cnt[g]) - u[:, cols]
            pooled.append(_dot(dg.astype(BF16), wpool_ref[g]))
        y_pool = jnp.concatenate(pooled, axis=-1) * pscale_ref[...]
        merged = (_sigmoid(znat_ref[:, 3 * D_MODEL:4 * D_MODEL]) * o
                  + _sigmoid(znat_ref[:, 4 * D_MODEL:5 * D_MODEL]) * y_pool)
        x1 = x + _dot(merged.astype(BF16), wout_ref[...])
        x1_ref[...] = x1
        h2 = _rms(x1, gffn_ref[...])
        _store_row_tiles(h2_ref, (), h2)
        _route(*_split_bf16(h2), wr_ref, br_ref, ti_ref, tw_ref)


def _mixer_sample(x, gmix, win, wqft, lblt, hgn, wpool, pscale, wout, gffn, wr, br, state, pbuf, start_pos):
    n = x.shape[0]
    steps = n // SAMPLE_GROUP
    pool_cnt = tuple(float(min(start_pos + 1, w)) for w in POOL_WINDOWS)
    pbuf2 = pbuf.reshape(n, POOL_BUF * D_MODEL)
    full = _const_spec
    out = pl.pallas_call(
        functools.partial(_mixer_sample_kernel, pool_cnt),
        grid=(steps,),
        in_specs=[
            full((n, D_MODEL)),
            full((1, D_MODEL)),
            full(win.shape),
            full(wqft.shape),
            full(lblt.shape),
            full((1, HEAD_DIM)),
            full(wpool.shape),
            full((1, D_MODEL)),
            full(wout.shape),
            full((1, D_MODEL)),
            full(wr.shape),
            full(br.shape),
            pl.BlockSpec((SAMPLE_GROUP, HEADS, HEAD_DIM, HEAD_DIM), lambda i: (i, 0, 0, 0)),
            pl.BlockSpec((SAMPLE_GROUP, POOL_BUF * D_MODEL), lambda i: (i, 0)),
        ],
        out_specs=[
            pl.BlockSpec((n, D_MODEL), lambda i: (0, 0)),
            pl.BlockSpec((n * ROW_TILE, LANES), lambda i: (0, 0)),
            pl.BlockSpec((n, ROUTER_LANES), lambda i: (0, 0)),
            pl.BlockSpec((n, ROUTER_LANES), lambda i: (0, 0)),
            pl.BlockSpec((SAMPLE_GROUP, HEADS, HEAD_DIM, HEAD_DIM), lambda i: (i, 0, 0, 0)),
            pl.BlockSpec((SAMPLE_GROUP, POOL_BUF * D_MODEL), lambda i: (i, 0)),
        ],
        out_shape=[
            jax.ShapeDtypeStruct((n, D_MODEL), F32),
            jax.ShapeDtypeStruct((n * ROW_TILE, LANES), F32),
            jax.ShapeDtypeStruct((n, ROUTER_LANES), jnp.int32),
            jax.ShapeDtypeStruct((n, ROUTER_LANES), F32),
            jax.ShapeDtypeStruct(state.shape, F32),
            jax.ShapeDtypeStruct(pbuf2.shape, F32),
        ],
        scratch_shapes=[
            pltpu.VMEM((n, D_MODEL), F32),
            pltpu.VMEM((n, 5 * D_MODEL), F32),
            pltpu.VMEM((n, D_MODEL), F32),
            pltpu.VMEM((n, D_MODEL), F32),
        ],
        compiler_params=pltpu.CompilerParams(dimension_semantics=("arbitrary",), vmem_limit_bytes=VMEM_LIMIT),
        name="mixer_sample",
    )(x, gmix, win, wqft, lblt, hgn, wpool, pscale, wout, gffn, wr, br, state, pbuf2)
    x1, h2, ti, tw, snew, pnew = out
    return x1, h2, ti, tw, snew, pnew.reshape(n, POOL_BUF, D_MODEL)


def _moe_kernel(n_tokens, be_ref, off_ref, nv_ref, nu_ref, sa_ref, h2_hbm,
                wg_ref, bg_ref, wu_ref, bu_ref, wd_ref, bd_ref, y_hbm,
                xbuf0, xbuf1, xbuf2, ybuf0, ybuf1, ybuf2, wgb, wub, wdb, xb_ref, gate_ref, act_ref, gsem, ssem):
    rows = MOE_ROWS
    i = pl.program_id(0)
    n_used = nu_ref[0]
    n_blocks = pl.num_programs(0)
    nv_back = lambda k: jnp.where(i >= k, nv_ref[jnp.maximum(i - k, 0)], 0)
    prev = jnp.maximum(i - 1, 0)
    nv_prev = nv_back(1)
    xbuf = (xbuf0, xbuf1, xbuf2)
    ybuf = (ybuf0, ybuf1, ybuf2)

    def tile_rows(first, n=1):
        start = first * ROW_TILE
        return pl.ds(start if isinstance(first, int) else pl.multiple_of(start, ROW_TILE), n * ROW_TILE)

    def gather_copy(a, r, s):
        tok = a >> 2
        return pltpu.make_async_copy(h2_hbm.at[tile_rows(tok), :], xbuf[s].at[tile_rows(r), :], gsem.at[s])

    def scatter_copy(a, r, n, s):
        dst = (a & (TOP_K - 1)) * n_tokens + (a >> 2)
        return pltpu.make_async_copy(ybuf[s].at[tile_rows(r, n), :], y_hbm.at[tile_rows(dst, n), :], ssem.at[s])

    def wait_gather(s):
        pltpu.make_async_copy(h2_hbm.at[tile_rows(0, rows), :], xbuf[s], gsem.at[s]).wait()

    def wait_scatter(n, s):
        size = rows
        while size >= 1:
            @pl.when((n & size) != 0)
            def _():
                scatter_copy(0, 0, size, s).wait()
            size //= 2

    def scatter_loop(blk, n, s):
        def body(r, c):
            scatter_copy(sa_ref[off_ref[blk] + r], r, 1, s).start()
            return c
        lax.fori_loop(0, n, body, 0)

    n_phases = 3

    def start_copies(phase, cur, full_prev):
        ahead, behind = (cur + 2) % MOE_DEPTH, (cur + MOE_DEPTH - 1) % MOE_DEPTH
        off_ahead = off_ref[jnp.minimum(i + 2, n_blocks - 1)]
        off_prev = off_ref[prev]
        for r in range(rows * phase // n_phases, rows * (phase + 1) // n_phases):
            gather_copy(sa_ref[off_ahead + r], r, ahead).start()
            if full_prev:
                scatter_copy(sa_ref[off_prev + r], r, 1, behind).start()

    def phase_load(cur, full_prev):
        wait_gather(cur)
        xb_ref[...] = _load_row_tiles(xbuf[cur], (), rows).astype(BF16)

    def phase_gate(cur, full_prev):
        start_copies(0, cur, full_prev)
        gate_ref[...] = jnp.minimum(_dot(xb_ref[...], wgb[...]) + bg_ref[0], SWIGLU_LIMIT)

    def phase_up(cur, full_prev):
        start_copies(1, cur, full_prev)
        up = jnp.clip(_dot(xb_ref[...], wub[...]) + bu_ref[0], -SWIGLU_LIMIT, SWIGLU_LIMIT)
        gate = gate_ref[...]
        act_ref[...] = ((up + 1.0) * gate * _sigmoid(SWIGLU_ALPHA * gate)).astype(BF16)

    def phase_down(cur, full_prev):
        nxt, behind = (cur + 1) % MOE_DEPTH, (cur + MOE_DEPTH - 1) % MOE_DEPTH
        start_copies(2, cur, full_prev)
        wait_scatter(nv_back(MOE_DEPTH), cur)
        _store_row_tiles(ybuf[cur], (), _dot(act_ref[...], wdb[...]) + bd_ref[0])
        if not full_prev:
            scatter_loop(prev, nv_prev, behind)

        @pl.when(i == n_used - 1)
        def _():
            scatter_loop(i, nv_ref[i], cur)
            wait_scatter(nv_back(2), nxt)
            wait_scatter(nv_prev, behind)
            wait_scatter(nv_ref[i], cur)
            wait_gather(nxt)
            wait_gather((cur + 2) % MOE_DEPTH)

    @pl.when(i == 0)
    def _():
        for blk in range(MOE_DEPTH - 1):
            def body(r, c):
                gather_copy(sa_ref[off_ref[blk] + r], r, blk).start()
                return c
            lax.fori_loop(0, rows, body, 0)

    @pl.when(i < n_used)
    def _():
        first_of_expert = jnp.logical_or(i == 0, be_ref[i] != be_ref[prev])

        @pl.when(first_of_expert)
        def _():
            wgb[...] = wg_ref[0].astype(BF16)
            wub[...] = wu_ref[0].astype(BF16)
            wdb[...] = wd_ref[0].astype(BF16)

        prev_is_full = nv_prev == rows
        for phase in (phase_load, phase_gate, phase_up, phase_down):
            for cur in range(MOE_DEPTH):
                for full_prev in (True, False):
                    full_cond = prev_is_full if full_prev else jnp.logical_not(prev_is_full)

                    @pl.when(jnp.logical_and(i % MOE_DEPTH == cur, full_cond))
                    def _():
                        phase(cur, full_prev)


def _moe(h2, routing, wg, bg, wu, bu, wd, bd):
    rows = MOE_ROWS
    n_tokens = h2.shape[0] // ROW_TILE
    block_e, block_off, block_nv, n_used, sorted_a = routing
    n_blocks = block_e.shape[0]
    w_spec = pl.BlockSpec((1, D_MODEL, D_MODEL), lambda i, be, *_: (be[i], 0, 0))
    b_spec = pl.BlockSpec((1, 1, D_MODEL), lambda i, be, *_: (be[i], 0, 0))
    grid_spec = pltpu.PrefetchScalarGridSpec(
        num_scalar_prefetch=5,
        grid=(n_blocks,),
        in_specs=[pl.BlockSpec(memory_space=pl.ANY), w_spec, b_spec, w_spec, b_spec, w_spec, b_spec],
        out_specs=pl.BlockSpec(memory_space=pl.ANY),
        scratch_shapes=[
            *[pltpu.VMEM((rows * ROW_TILE, LANES), F32)] * (2 * MOE_DEPTH),
            pltpu.VMEM((D_MODEL, D_MODEL), BF16),
            pltpu.VMEM((D_MODEL, D_MODEL), BF16),
            pltpu.VMEM((D_MODEL, D_MODEL), BF16),
            pltpu.VMEM((rows, D_MODEL), BF16),
            pltpu.VMEM((rows, D_MODEL), F32),
            pltpu.VMEM((rows, D_MODEL), BF16),
            pltpu.SemaphoreType.DMA((MOE_DEPTH,)),
            pltpu.SemaphoreType.DMA((MOE_DEPTH,)),
        ],
    )
    return pl.pallas_call(
        functools.partial(_moe_kernel, n_tokens),
        grid_spec=grid_spec,
        out_shape=jax.ShapeDtypeStruct((n_tokens * TOP_K * ROW_TILE, LANES), F32),
        compiler_params=pltpu.CompilerParams(dimension_semantics=("arbitrary",), vmem_limit_bytes=VMEM_LIMIT),
        name="moe_experts",
    )(block_e, block_off, block_nv, n_used, sorted_a, h2, wg, bg, wu, bu, wd, bd)


ASSIGN_BITS = 17


def _moe_routing(top_i):
    rows = MOE_ROWS
    n_assign = top_i.shape[0] * TOP_K
    assert n_assign <= 1 << ASSIGN_BITS
    n_blocks = -(-n_assign // rows) + N_EXPERTS
    flat_e = top_i.reshape(-1)
    keys = jnp.sort(flat_e * (1 << ASSIGN_BITS) + jnp.arange(n_assign, dtype=jnp.int32))
    smem_len = -(-(n_assign + rows) // 1024) * 1024
    sorted_a = jnp.pad(keys & ((1 << ASSIGN_BITS) - 1), (0, smem_len - n_assign))
    experts = jnp.arange(N_EXPERTS, dtype=jnp.int32)
    counts = jnp.sum((flat_e[:, None] == experts[None, :]).astype(jnp.int32), axis=0)
    nblk = (counts + rows - 1) // rows
    blk_end = jnp.cumsum(nblk)
    blk_start = blk_end - nblk
    row_start = jnp.cumsum(counts) - counts
    blk = jnp.arange(n_blocks, dtype=jnp.int32)
    owner = (blk[:, None] >= blk_start[None, :]) & (blk[:, None] < blk_end[None, :])
    pick = lambda v: jnp.sum(jnp.where(owner, v[None, :], 0), axis=1)
    within = (blk - pick(blk_start)) * rows
    used = blk < blk_end[-1]
    block_e = jnp.where(used, pick(experts), N_EXPERTS - 1).astype(jnp.int32)
    block_off = jnp.where(used, pick(row_start) + within, 0).astype(jnp.int32)
    block_nv = jnp.where(used, jnp.clip(pick(counts) - within, 0, rows), 0).astype(jnp.int32)
    n_used = blk_end[-1].astype(jnp.int32).reshape(1)
    return block_e, block_off, block_nv, n_used, sorted_a.astype(jnp.int32)


def _final_kernel(x1_ref, y_ref, tw_ref, p_ref, gple_ref, wpg_ref, wpp_ref, gfin_ref, out_ref):
    tw = tw_ref[...]
    rows = tw.shape[0]
    x = x1_ref[...].reshape(rows, D_MODEL)
    for j in range(TOP_K):
        x = x + tw[:, j:j + 1] * _load_row_tiles(y_ref, (j,), rows)
    gate = _sigmoid(_dot(_rms(x, gple_ref[...]).astype(BF16), wpg_ref[...]))
    x = x + gate * _dot(p_ref[...].reshape(rows, -1).astype(BF16), wpp_ref[...])
    out_ref[...] = _rms(x, gfin_ref[...]).reshape(out_ref.shape)


def _final(x1, y4, tw, p, gple, wpg, wpp, gfin, rows, tok_offset):
    g, t, _ = x1.shape
    nt = t // rows
    off = tok_offset // rows
    assert off * rows == tok_offset and t % rows == 0
    ple = p.shape[-1]
    tok_map = lambda i: (i // nt, i % nt, 0)
    return pl.pallas_call(
        _final_kernel,
        grid=(g * nt,),
        in_specs=[
            pl.BlockSpec((1, rows, D_MODEL), tok_map),
            pl.BlockSpec((TOP_K, rows * ROW_TILE, LANES), lambda i: (0, i + off, 0)),
            pl.BlockSpec((rows, ROUTER_LANES), lambda i: (i, 0)),
            pl.BlockSpec((1, rows, ple), tok_map),
            _const_spec((1, D_MODEL)),
            _const_spec(wpg.shape),
            _const_spec(wpp.shape),
            _const_spec((1, D_MODEL)),
        ],
        out_specs=pl.BlockSpec((1, rows, D_MODEL), tok_map),
        out_shape=jax.ShapeDtypeStruct(x1.shape, F32),
        compiler_params=pltpu.CompilerParams(dimension_semantics=("arbitrary",), vmem_limit_bytes=VMEM_LIMIT),
        name="final",
    )(x1, y4, tw, p, gple, wpg, wpp, gfin)


def kernel(x_prompt, x_sample, p_prompt, p_sample, state_hgrn, state_pool, g_mix, w_in, hg_lb_logits, hg_norm, w_pool,
           pool_scale, w_out, g_ffn, w_router, b_router, w_gate, b_gate, w_up, b_up, w_down, b_down, g_ple,
           w_ple_gate, w_ple_proj, g_final):
    depth = w_in.shape[0]
    assert depth == 1, "single-layer step"
    b, t, _ = x_prompt.shape
    ns = x_sample.shape[0]
    assert x_sample.shape[1] == 1
    n_prompt = b * t
    n_tok = n_prompt + ns

    row = lambda a: a.reshape(1, -1)
    win = w_in[0].astype(BF16)
    wqft = w_in[0, :, :2 * D_MODEL].T.astype(BF16)
    lbl = hg_lb_logits.astype(F32)
    wpool = w_pool[0].astype(BF16)
    wout = w_out[0].astype(BF16)
    wr = jnp.pad(w_router[0], ((0, 0), (0, ROUTER_LANES - N_EXPERTS)))
    br = jnp.pad(b_router[0], (0, ROUTER_LANES - N_EXPERTS)).reshape(1, -1)
    shared = (row(hg_norm[0]), wpool, row(pool_scale[0]), wout, row(g_ffn[0]), wr, br)

    x1_s, h2_s, ti_s, tw_s, s_s, pool_s = _mixer_sample(
        x_sample.reshape(ns, D_MODEL), row(g_mix[0]), win, wqft, lbl.T, *shared,
        state_hgrn[0], state_pool[0], start_pos=PAST_LEN)
    x1_p, h2, ti_p, tw_p, s_p, pool_p = _mixer_prompt(x_prompt, row(g_mix[0]), win, lbl, *shared, h2_s)

    top_i = jnp.concatenate([ti_p[:, :TOP_K], ti_s[:, :TOP_K]], axis=0)
    routing = _moe_routing(top_i)
    e3 = lambda a: a[0].reshape(N_EXPERTS, 1, D_MODEL)
    y4 = _moe(h2, routing, w_gate[0], e3(b_gate), w_up[0], e3(b_up), w_down[0], e3(b_down))
    y4 = y4.reshape(TOP_K, n_tok * ROW_TILE, LANES)

    wpg = w_ple_gate[0].astype(BF16)
    wpp = w_ple_proj[0].astype(BF16)
    fin = (row(g_ple[0]), wpg, wpp, row(g_final))
    y_p = _final(x1_p, y4, tw_p, p_prompt[0], *fin, rows=FINAL_ROWS, tok_offset=0)
    y_s = _final(x1_s[None], y4, tw_s, p_sample[0].reshape(1, ns, -1), *fin, rows=ns, tok_offset=n_prompt)

    return (y_p, y_s.reshape(ns, 1, D_MODEL), s_p[None], pool_p[None], s_s[None], pool_s[None])
```

```python
import functools

import jax
import jax.numpy as jnp
from jax import lax
from jax.experimental import pallas as pl
from jax.experimental.pallas import tpu as pltpu

F32 = jnp.float32
BF16 = jnp.bfloat16

D_MODEL = 1024
HEADS = 8
HEAD_DIM = 128
CHUNK = 32
POOL_WINDOWS = (2, 4, 8, 16)
POOL_GW = D_MODEL // len(POOL_WINDOWS)
POOL_BUF = 15
POOL_CARRY = 16
N_EXPERTS = 32
TOP_K = 4
ROUTER_LANES = 128
SWIGLU_LIMIT = 7.0
SWIGLU_ALPHA = 1.702
EPS = 1e-6
PAST_LEN = 16384

SEG_Q, SEG_F, SEG_I, SEG_G, SEG_U, SEG_A, SEG_B = range(7)

MIXER_ROWS = 512
EW_LANES = 256
MOE_ROWS = 512
MOE_DEPTH = 3
FINAL_ROWS = 512
VMEM_LIMIT = 56 * 1024 * 1024


def _rms(x, g):
    ms = jnp.mean(x * x, axis=-1, keepdims=True)
    return x * lax.rsqrt(ms + EPS) * g


def _sigmoid(x):
    return 0.5 * jnp.tanh(0.5 * x) + 0.5


def _silu(x):
    h = 0.5 * x
    return h + h * jnp.tanh(h)


def _dot(a, b):
    return jnp.dot(a, b, preferred_element_type=F32)


def _dot_nt(a, b):
    return lax.dot_general(a, b, (((1,), (1,)), ((), ())), preferred_element_type=F32)


def _dot_tn(a, b):
    return lax.dot_general(a, b, (((0,), (0,)), ((), ())), preferred_element_type=F32)


LANES = 128
ROW_TILE = D_MODEL // LANES


def _store_row_tiles(ref, lead, x, period=ROW_TILE, offset=0):
    rows = x.shape[0]
    for c in range(ROW_TILE):
        ref[lead + (pl.ds(offset + c, rows, stride=period), slice(None))] = x[:, c * LANES:(c + 1) * LANES]


def _load_row_tiles(ref, lead, rows, period=ROW_TILE, offset=0):
    return jnp.concatenate(
        [ref[lead + (pl.ds(offset + c, rows, stride=period), slice(None))] for c in range(ROW_TILE)], axis=-1)


def _split_bf16(x):
    hi = x.astype(BF16)
    lo = (x - hi.astype(F32)).astype(BF16)
    return hi, lo


def _forget_lower_bound(lbl):
    m = jnp.max(lbl, axis=0, keepdims=True)
    e = jnp.exp(lbl - m)
    return e[0:1] / jnp.sum(e, axis=0, keepdims=True)


def _head_norm_gate(o, g_raw, hgn):
    parts = []
    for h in range(HEADS):
        oh = o[:, h * HEAD_DIM:(h + 1) * HEAD_DIM]
        parts.append(_rms(oh, hgn))
    return jnp.concatenate(parts, axis=-1) * _silu(g_raw)


def _route(h_hi, h_lo, wr_ref, br_ref, ti_ref, tw_ref):
    rows = h_hi.shape[0]
    w_hi, w_lo = _split_bf16(wr_ref[...])
    logits = _dot(h_hi, w_hi) + _dot(h_lo, w_hi) + _dot(h_hi, w_lo) + br_ref[...]
    lane = lax.broadcasted_iota(jnp.int32, (rows, ROUTER_LANES), 1)
    neg = jnp.float32(-jnp.inf)
    l = jnp.where(lane < N_EXPERTS, logits, neg)
    ti = jnp.zeros((rows, ROUTER_LANES), jnp.int32)
    tw = jnp.zeros((rows, ROUTER_LANES), F32)
    m0 = None
    denom = None
    es = []
    for j in range(TOP_K):
        m = jnp.max(l, axis=-1, keepdims=True)
        idx = jnp.min(jnp.where(l == m, lane, ROUTER_LANES), axis=-1, keepdims=True)
        l = jnp.where(lane == idx, neg, l)
        if j == 0:
            m0 = m
        e = jnp.exp(m - m0)
        es.append(e)
        denom = e if denom is None else denom + e
        ti = jnp.where(lane == j, idx, ti)
    for j in range(TOP_K):
        tw = jnp.where(lane == j, es[j] / denom, tw)
    ti_ref[...] = ti
    tw_ref[...] = tw


def _mixer_prompt_kernel(nt, x_ref, gmix_ref, win_ref, lbl_ref, hgn_ref, wpool_ref, pscale_ref, wout_ref, gffn_ref,
                         wr_ref, br_ref, h2s_ref, x1_ref, h2_ref, ti_ref, tw_ref, sfin_ref, ptail_ref, *scratch):
    s = pl.program_id(0)
    n_main = pl.num_programs(0) - 1

    @pl.when(s < n_main)
    def _():
        _mixer_prompt_block(s % nt, nt - 1, x_ref.at[0], gmix_ref, win_ref, lbl_ref, hgn_ref, wpool_ref, pscale_ref,
                            wout_ref, gffn_ref, wr_ref, br_ref,
                            x1_ref.at[0], h2_ref, ti_ref, tw_ref, sfin_ref.at[0], ptail_ref.at[0], *scratch)

    @pl.when(s == n_main)
    def _():
        h2_ref[0:h2s_ref.shape[0], :] = h2s_ref[...]


def _mixer_prompt_block(t, last_t, x_ref, gmix_ref, win_ref, lbl_ref, hgn_ref, wpool_ref, pscale_ref,
                        wout_ref, gffn_ref, wr_ref, br_ref,
                        x1_ref, h2_ref, ti_ref, tw_ref, sfin_ref, ptail_ref,
                        st_ref, uext_ref, qe_ref, ke_ref, kd_ref, v_ref, o_ref, p_ref, h_ref,
                        za_ref, zb_ref, plast_ref):
    rows = MIXER_ROWS
    n_chunks = rows // CHUNK

    def proj(seg):
        return _dot(h_ref[...], win_ref[:, seg * D_MODEL:(seg + 1) * D_MODEL])

    def phase_project():
        @pl.when(t == 0)
        def _():
            st_ref[...] = jnp.zeros_like(st_ref)
            uext_ref[0:POOL_CARRY, :] = jnp.zeros((POOL_CARRY, D_MODEL), F32)

        h_ref[...] = _rms(x_ref[...], gmix_ref[...]).astype(BF16)
        o_ref[...] = proj(SEG_Q)
        p_ref[...] = proj(SEG_F)
        v_ref[...] = proj(SEG_I).astype(BF16)

    def phase_hgrn_inputs():
        lb = _forget_lower_bound(lbl_ref[...])
        row_in_chunk = lax.broadcasted_iota(jnp.int32, (CHUNK, EW_LANES), 0)
        for c in range(n_chunks):
            rs = slice(c * CHUNK, (c + 1) * CHUNK)
            for l0 in range(0, D_MODEL, EW_LANES):
                ls = slice(l0, l0 + EW_LANES)
                q = _silu(o_ref[rs, ls])
                lbt = lb[:, ls]
                f = lbt + (1.0 - lbt) * _sigmoid(p_ref[rs, ls])
                k = 1.0 - f
                p = f
                s = 1
                while s < CHUNK:
                    p = p * jnp.where(row_in_chunk >= s, pltpu.roll(p, s, axis=0), 1.0)
                    s *= 2
                plast = p[CHUNK - 1:CHUNK, :]
                k_over_p = k / p
                qe_ref[rs, ls] = (q * p).astype(BF16)
                ke_ref[rs, ls] = k_over_p.astype(BF16)
                kd_ref[rs, ls] = (k_over_p * plast).astype(BF16)
                plast_ref[c:c + 1, ls] = plast

    def phase_hgrn():
        causal = (lax.broadcasted_iota(jnp.int32, (CHUNK, CHUNK), 0)
                  >= lax.broadcasted_iota(jnp.int32, (CHUNK, CHUNK), 1))
        for hd in range(HEADS):
            cols = slice(hd * HEAD_DIM, (hd + 1) * HEAD_DIM)
            st = st_ref[hd]
            for c in range(n_chunks):
                rs = slice(c * CHUNK, (c + 1) * CHUNK)
                qe = qe_ref[rs, cols]
                ke = ke_ref[rs, cols]
                kd = kd_ref[rs, cols]
                vv = v_ref[rs, cols]
                decay = plast_ref[c:c + 1, cols]
                scores = jnp.where(causal, _dot_nt(qe, ke), 0.0)
                o_ref[rs, cols] = _dot_nt(qe, st.astype(BF16)) + _dot(scores.astype(BF16), vv)
                st = st * decay + _dot_tn(vv, kd)
            st_ref[hd] = st

        @pl.when(t == last_t)
        def _():
            for hd in range(HEADS):
                sfin_ref[hd] = st_ref[hd].T

    def phase_pool_and_gates():
        uext_ref[POOL_CARRY:POOL_CARRY + rows, :] = proj(SEG_U)
        pos1 = t * rows + lax.broadcasted_iota(jnp.int32, (rows, 1), 0) + 1
        pooled = []
        for g, w in enumerate(POOL_WINDOWS):
            cols = slice(g * POOL_GW, (g + 1) * POOL_GW)
            sw = uext_ref[:, cols]
            s = 1
            while s < w:
                sw = sw + pltpu.roll(sw, s, axis=0)
                s *= 2
            inv_cnt = 1.0 / jnp.minimum(pos1, w).astype(F32)
            dg = sw[POOL_CARRY:, :] * inv_cnt - uext_ref[POOL_CARRY:POOL_CARRY + rows, cols]
            pooled.append(_dot(dg.astype(BF16), wpool_ref[g]))

        @pl.when(t == last_t)
        def _():
            ptail_ref[...] = uext_ref[rows + POOL_CARRY - POOL_BUF:rows + POOL_CARRY, :]

        uext_ref[0:POOL_CARRY, :] = uext_ref[rows:rows + POOL_CARRY, :]
        uext_ref[POOL_CARRY:POOL_CARRY + rows, :] = jnp.concatenate(pooled, axis=-1) * pscale_ref[...]
        p_ref[...] = proj(SEG_G)
        za_ref[...] = proj(SEG_A)
        zb_ref[...] = proj(SEG_B)

    def phase_merge():
        hgn = hgn_ref[...]
        for c in range(n_chunks):
            rs = slice(c * CHUNK, (c + 1) * CHUNK)
            ys = slice(POOL_CARRY + c * CHUNK, POOL_CARRY + (c + 1) * CHUNK)
            for hd in range(HEADS):
                cols = slice(hd * HEAD_DIM, (hd + 1) * HEAD_DIM)
                g_raw = p_ref[rs, cols]
                o = _rms(o_ref[rs, cols], hgn) * _silu(g_raw)
                merged = _sigmoid(za_ref[rs, cols]) * o + _sigmoid(zb_ref[rs, cols]) * uext_ref[ys, cols]
                h_ref[rs, cols] = merged.astype(BF16)

    def phase_out_and_route():
        x1_ref[...] = x_ref[...] + _dot(h_ref[...], wout_ref[...])
        gffn = gffn_ref[...]
        for c in range(n_chunks):
            rs = slice(c * CHUNK, (c + 1) * CHUNK)
            h2 = _rms(x1_ref[rs, :], gffn)
            _store_row_tiles(h2_ref, (), h2, offset=c * CHUNK * ROW_TILE)
            qe_ref[rs, :], ke_ref[rs, :] = _split_bf16(h2)
        _route(qe_ref[...], ke_ref[...], wr_ref, br_ref, ti_ref, tw_ref)

    for phase in (phase_project, phase_hgrn_inputs, phase_hgrn, phase_pool_and_gates, phase_merge,
                  phase_out_and_route):
        phase()


def _const_spec(shape):
    zeros = (0,) * len(shape)
    return pl.BlockSpec(shape, lambda *_: zeros, pipeline_mode=pl.Buffered(1))


def _mixer_prompt(x, gmix, win, lbl, hgn, wpool, pscale, wout, gffn, wr, br, h2_tail):
    b, t, _ = x.shape
    rows = MIXER_ROWS
    nt = t // rows
    n_main = b * nt
    assert t % rows == 0 and h2_tail.shape[0] <= rows * ROW_TILE
    main = lambda s: jnp.minimum(s, n_main - 1)
    tok_spec = pl.BlockSpec((1, rows, D_MODEL), lambda s: (main(s) // nt, main(s) % nt, 0))
    flat_spec = pl.BlockSpec((rows * ROW_TILE, LANES), lambda s: (s, 0))
    lane_spec = pl.BlockSpec((rows, ROUTER_LANES), lambda s: (main(s), 0))
    return pl.pallas_call(
        functools.partial(_mixer_prompt_kernel, nt),
        grid=(n_main + 1,),
        in_specs=[
            tok_spec,
            _const_spec((1, D_MODEL)),
            _const_spec(win.shape),
            _const_spec(lbl.shape),
            _const_spec((1, HEAD_DIM)),
            _const_spec(wpool.shape),
            _const_spec((1, D_MODEL)),
            _const_spec(wout.shape),
            _const_spec((1, D_MODEL)),
            _const_spec(wr.shape),
            _const_spec(br.shape),
            _const_spec(h2_tail.shape),
        ],
        out_specs=[
            tok_spec,
            flat_spec,
            lane_spec,
            lane_spec,
            pl.BlockSpec((1, HEADS, HEAD_DIM, HEAD_DIM), lambda s: (main(s) // nt, 0, 0, 0)),
            pl.BlockSpec((1, POOL_BUF, D_MODEL), lambda s: (main(s) // nt, 0, 0)),
        ],
        out_shape=[
            jax.ShapeDtypeStruct(x.shape, F32),
            jax.ShapeDtypeStruct((b * t * ROW_TILE + h2_tail.shape[0], LANES), F32),
            jax.ShapeDtypeStruct((b * t, ROUTER_LANES), jnp.int32),
            jax.ShapeDtypeStruct((b * t, ROUTER_LANES), F32),
            jax.ShapeDtypeStruct((b, HEADS, HEAD_DIM, HEAD_DIM), F32),
            jax.ShapeDtypeStruct((b, POOL_BUF, D_MODEL), F32),
        ],
        scratch_shapes=[
            pltpu.VMEM((HEADS, HEAD_DIM, HEAD_DIM), F32),
            pltpu.VMEM((rows + POOL_CARRY, D_MODEL), F32),
            pltpu.VMEM((rows, D_MODEL), BF16),
            pltpu.VMEM((rows, D_MODEL), BF16),
            pltpu.VMEM((rows, D_MODEL), BF16),
            pltpu.VMEM((rows, D_MODEL), BF16),
            pltpu.VMEM((rows, D_MODEL), F32),
            pltpu.VMEM((rows, D_MODEL), F32),
            pltpu.VMEM((rows, D_MODEL), BF16),
            pltpu.VMEM((rows, D_MODEL), F32),
            pltpu.VMEM((rows, D_MODEL), F32),
            pltpu.VMEM((rows // CHUNK, D_MODEL), F32),
        ],
        compiler_params=pltpu.CompilerParams(dimension_semantics=("arbitrary",), vmem_limit_bytes=VMEM_LIMIT),
        name="mixer_prompt",
    )(x, gmix, win, lbl, hgn, wpool, pscale, wout, gffn, wr, br, h2_tail)


SAMPLE_GROUP = 8


def _mixer_sample_kernel(pool_cnt, x_ref, gmix_ref, win_ref, wqft_ref, lblt_ref, hgn_ref, wpool_ref, pscale_ref,
                         wout_ref, gffn_ref, wr_ref, br_ref, s_ref, pbuf_ref,
                         x1_ref, h2_ref, ti_ref, tw_ref, snew_ref, pnew_ref,
                         h_ref, znat_ref, o_ref, bsum_ref):
    step = pl.program_id(0)
    last = pl.num_programs(0) - 1
    r0 = pl.multiple_of(step * SAMPLE_GROUP, SAMPLE_GROUP)

    @pl.when(step == 0)
    def _():
        hf = _rms(x_ref[...], gmix_ref[...])
        h_ref[...] = hf
        h = hf.astype(BF16)
        for i, seg in enumerate((SEG_I, SEG_G, SEG_U, SEG_A, SEG_B)):
            znat_ref[:, i * D_MODEL:(i + 1) * D_MODEL] = _dot(h, win_ref[:, seg * D_MODEL:(seg + 1) * D_MODEL])

    hg = h_ref[pl.ds(r0, SAMPLE_GROUP), :].astype(BF16)
    qft = _dot_nt(wqft_ref[...], hg)
    lbl = lblt_ref[...]
    m = jnp.max(lbl, axis=1, keepdims=True)
    e = jnp.exp(lbl - m)
    lb = e[:, 0:1] / jnp.sum(e, axis=1, keepdims=True)
    q_raw = qft[0:D_MODEL]
    qt = _silu(q_raw)
    ft = lb + (1.0 - lb) * _sigmoid(qft[D_MODEL:2 * D_MODEL])
    kt = 1.0 - ft

    for j in range(SAMPLE_GROUP):
        v_row = znat_ref[pl.ds(r0 + j, 1), 0:D_MODEL]
        o_parts = []
        for hd in range(HEADS):
            rs = slice(hd * HEAD_DIM, (hd + 1) * HEAD_DIM)
            s_new = ft[rs, j:j + 1] * s_ref[j, hd] + kt[rs, j:j + 1] * v_row[:, rs]
            snew_ref[j, hd] = s_new
            o_parts.append(jnp.sum(qt[rs, j:j + 1] * s_new, axis=0, keepdims=True))
        o_ref[pl.ds(r0 + j, 1), :] = jnp.concatenate(o_parts, axis=-1)

    u_g = znat_ref[pl.ds(r0, SAMPLE_GROUP), 2 * D_MODEL:3 * D_MODEL]
    sums = []
    for g, w in enumerate(POOL_WINDOWS):
        acc = jnp.zeros((SAMPLE_GROUP, POOL_GW), F32)
        for j in range(1, w):
            row = POOL_BUF - j
            acc = acc + pbuf_ref[:, row * D_MODEL + g * POOL_GW:row * D_MODEL + (g + 1) * POOL_GW]
        sums.append(acc)
    bsum_ref[pl.ds(r0, SAMPLE_GROUP), :] = jnp.concatenate(sums, axis=-1)
    pnew_ref[:, 0:(POOL_BUF - 1) * D_MODEL] = pbuf_ref[:, D_MODEL:POOL_BUF * D_MODEL]
    pnew_ref[:, (POOL_BUF - 1) * D_MODEL:POOL_BUF * D_MODEL] = u_g

    @pl.when(step == last)
    def _():
        x = x_ref[...]
        g_raw = znat_ref[:, D_MODEL:2 * D_MODEL]
        u = znat_ref[:, 2 * D_MODEL:3 * D_MODEL]
        o = _head_norm_gate(o_ref[...], g_raw, hgn_ref[...])
        sw = bsum_ref[...] + u
        pooled = []
        for g, w in enumerate(POOL_WINDOWS):
            cols = slice(g * POOL_GW, (g + 1) * POOL_GW)
            dg = sw[:, cols] * (1.0 / pool_cnt[g]) - u[:, cols]
            pooled.append(_dot(dg.astype(BF16), wpool_ref[g]))
        y_pool = jnp.concatenate(pooled, axis=-1) * pscale_ref[...]
        merged = (_sigmoid(znat_ref[:, 3 * D_MODEL:4 * D_MODEL]) * o
                  + _sigmoid(znat_ref[:, 4 * D_MODEL:5 * D_MODEL]) * y_pool)
        x1 = x + _dot(merged.astype(BF16), wout_ref[...])
        x1_ref[...] = x1
        h2 = _rms(x1, gffn_ref[...])
        _store_row_tiles(h2_ref, (), h2)
        _route(*_split_bf16(h2), wr_ref, br_ref, ti_ref, tw_ref)


def _mixer_sample(x, gmix, win, wqft, lblt, hgn, wpool, pscale, wout, gffn, wr, br, state, pbuf, start_pos):
    n = x.shape[0]
    steps = n // SAMPLE_GROUP
    pool_cnt = tuple(float(min(start_pos + 1, w)) for w in POOL_WINDOWS)
    pbuf2 = pbuf.reshape(n, POOL_BUF * D_MODEL)
    full = _const_spec
    out = pl.pallas_call(
        functools.partial(_mixer_sample_kernel, pool_cnt),
        grid=(steps,),
        in_specs=[
            full((n, D_MODEL)),
            full((1, D_MODEL)),
            full(win.shape),
            full(wqft.shape),
            full(lblt.shape),
            full((1, HEAD_DIM)),
            full(wpool.shape),
            full((1, D_MODEL)),
            full(wout.shape),
            full((1, D_MODEL)),
            full(wr.shape),
            full(br.shape),
            pl.BlockSpec((SAMPLE_GROUP, HEADS, HEAD_DIM, HEAD_DIM), lambda i: (i, 0, 0, 0)),
            pl.BlockSpec((SAMPLE_GROUP, POOL_BUF * D_MODEL), lambda i: (i, 0)),
        ],
        out_specs=[
            pl.BlockSpec((n, D_MODEL), lambda i: (0, 0)),
            pl.BlockSpec((n * ROW_TILE, LANES), lambda i: (0, 0)),
            pl.BlockSpec((n, ROUTER_LANES), lambda i: (0, 0)),
            pl.BlockSpec((n, ROUTER_LANES), lambda i: (0, 0)),
            pl.BlockSpec((SAMPLE_GROUP, HEADS, HEAD_DIM, HEAD_DIM), lambda i: (i, 0, 0, 0)),
            pl.BlockSpec((SAMPLE_GROUP, POOL_BUF * D_MODEL), lambda i: (i, 0)),
        ],
        out_shape=[
            jax.ShapeDtypeStruct((n, D_MODEL), F32),
            jax.ShapeDtypeStruct((n * ROW_TILE, LANES), F32),
            jax.ShapeDtypeStruct((n, ROUTER_LANES), jnp.int32),
            jax.ShapeDtypeStruct((n, ROUTER_LANES), F32),
            jax.ShapeDtypeStruct(state.shape, F32),
            jax.ShapeDtypeStruct(pbuf2.shape, F32),
        ],
        scratch_shapes=[
            pltpu.VMEM((n, D_MODEL), F32),
            pltpu.VMEM((n, 5 * D_MODEL), F32),
            pltpu.VMEM((n, D_MODEL), F32),
            pltpu.VMEM((n, D_MODEL), F32),
        ],
        compiler_params=pltpu.CompilerParams(dimension_semantics=("arbitrary",), vmem_limit_bytes=VMEM_LIMIT),
        name="mixer_sample",
    )(x, gmix, win, wqft, lblt, hgn, wpool, pscale, wout, gffn, wr, br, state, pbuf2)
    x1, h2, ti, tw, snew, pnew = out
    return x1, h2, ti, tw, snew, pnew.reshape(n, POOL_BUF, D_MODEL)


def _moe_kernel(n_tokens, be_ref, off_ref, nv_ref, nu_ref, sa_ref, h2_hbm,
                wg_ref, bg_ref, wu_ref, bu_ref, wd_ref, bd_ref, y_hbm,
                xbuf0, xbuf1, xbuf2, ybuf0, ybuf1, ybuf2, wgb, wub, wdb, xb_ref, gate_ref, act_ref, gsem, ssem):
    rows = MOE_ROWS
    i = pl.program_id(0)
    n_used = nu_ref[0]
    n_blocks = pl.num_programs(0)
    nv_back = lambda k: jnp.where(i >= k, nv_ref[jnp.maximum(i - k, 0)], 0)
    prev = jnp.maximum(i - 1, 0)
    nv_prev = nv_back(1)
    xbuf = (xbuf0, xbuf1, xbuf2)
    ybuf = (ybuf0, ybuf1, ybuf2)

    def tile_rows(first, n=1):
        start = first * ROW_TILE
        return pl.ds(start if isinstance(first, int) else pl.multiple_of(start, ROW_TILE), n * ROW_TILE)

    def gather_copy(a, r, s):
        tok = a >> 2
        return pltpu.make_async_copy(h2_hbm.at[tile_rows(tok), :], xbuf[s].at[tile_rows(r), :], gsem.at[s])

    def scatter_copy(a, r, n, s):
        dst = (a & (TOP_K - 1)) * n_tokens + (a >> 2)
        return pltpu.make_async_copy(ybuf[s].at[tile_rows(r, n), :], y_hbm.at[tile_rows(dst, n), :], ssem.at[s])

    def wait_gather(s):
        pltpu.make_async_copy(h2_hbm.at[tile_rows(0, rows), :], xbuf[s], gsem.at[s]).wait()

    def wait_scatter(n, s):
        size = rows
        while size >= 1:
            @pl.when((n & size) != 0)
            def _():
                scatter_copy(0, 0, size, s).wait()
            size //= 2

    def scatter_loop(blk, n, s):
        def body(r, c):
            scatter_copy(sa_ref[off_ref[blk] + r], r, 1, s).start()
            return c
        lax.fori_loop(0, n, body, 0)

    n_phases = 3

    def start_copies(phase, cur, full_prev):
        ahead, behind = (cur + 2) % MOE_DEPTH, (cur + MOE_DEPTH - 1) % MOE_DEPTH
        off_ahead = off_ref[jnp.minimum(i + 2, n_blocks - 1)]
        off_prev = off_ref[prev]
        for r in range(rows * phase // n_phases, rows * (phase + 1) // n_phases):
            gather_copy(sa_ref[off_ahead + r], r, ahead).start()
            if full_prev:
                scatter_copy(sa_ref[off_prev + r], r, 1, behind).start()

    def phase_load(cur, full_prev):
        wait_gather(cur)
        xb_ref[...] = _load_row_tiles(xbuf[cur], (), rows).astype(BF16)

    def phase_gate(cur, full_prev):
        start_copies(0, cur, full_prev)
        gate_ref[...] = jnp.minimum(_dot(xb_ref[...], wgb[...]) + bg_ref[0], SWIGLU_LIMIT)

    def phase_up(cur, full_prev):
        start_copies(1, cur, full_prev)
        up = jnp.clip(_dot(xb_ref[...], wub[...]) + bu_ref[0], -SWIGLU_LIMIT, SWIGLU_LIMIT)
        gate = gate_ref[...]
        act_ref[...] = ((up + 1.0) * gate * _sigmoid(SWIGLU_ALPHA * gate)).astype(BF16)

    def phase_down(cur, full_prev):
        nxt, behind = (cur + 1) % MOE_DEPTH, (cur + MOE_DEPTH - 1) % MOE_DEPTH
        start_copies(2, cur, full_prev)
        wait_scatter(nv_back(MOE_DEPTH), cur)
        _store_row_tiles(ybuf[cur], (), _dot(act_ref[...], wdb[...]) + bd_ref[0])
        if not full_prev:
            scatter_loop(prev, nv_prev, behind)

        @pl.when(i == n_used - 1)
        def _():
            scatter_loop(i, nv_ref[i], cur)
            wait_scatter(nv_back(2), nxt)
            wait_scatter(nv_prev, behind)
            wait_scatter(nv_ref[i], cur)
            wait_gather(nxt)
            wait_gather((cur + 2) % MOE_DEPTH)

    @pl.when(i == 0)
    def _():
        for blk in range(MOE_DEPTH - 1):
            def body(r, c):
                gather_copy(sa_ref[off_ref[blk] + r], r, blk).start()
                return c
            lax.fori_loop(0, rows, body, 0)

    @pl.when(i < n_used)
    def _():
        first_of_expert = jnp.logical_or(i == 0, be_ref[i] != be_ref[prev])

        @pl.when(first_of_expert)
        def _():
            wgb[...] = wg_ref[0].astype(BF16)
            wub[...] = wu_ref[0].astype(BF16)
            wdb[...] = wd_ref[0].astype(BF16)

        prev_is_full = nv_prev == rows
        for phase in (phase_load, phase_gate, phase_up, phase_down):
            for cur in range(MOE_DEPTH):
                for full_prev in (True, False):
                    full_cond = prev_is_full if full_prev else jnp.logical_not(prev_is_full)

                    @pl.when(jnp.logical_and(i % MOE_DEPTH == cur, full_cond))
                    def _():
                        phase(cur, full_prev)


def _moe(h2, routing, wg, bg, wu, bu, wd, bd):
    rows = MOE_ROWS
    n_tokens = h2.shape[0] // ROW_TILE
    block_e, block_off, block_nv, n_used, sorted_a = routing
    n_blocks = block_e.shape[0]
    w_spec = pl.BlockSpec((1, D_MODEL, D_MODEL), lambda i, be, *_: (be[i], 0, 0))
    b_spec = pl.BlockSpec((1, 1, D_MODEL), lambda i, be, *_: (be[i], 0, 0))
    grid_spec = pltpu.PrefetchScalarGridSpec(
        num_scalar_prefetch=5,
        grid=(n_blocks,),
        in_specs=[pl.BlockSpec(memory_space=pl.ANY), w_spec, b_spec, w_spec, b_spec, w_spec, b_spec],
        out_specs=pl.BlockSpec(memory_space=pl.ANY),
        scratch_shapes=[
            *[pltpu.VMEM((rows * ROW_TILE, LANES), F32)] * (2 * MOE_DEPTH),
            pltpu.VMEM((D_MODEL, D_MODEL), BF16),
            pltpu.VMEM((D_MODEL, D_MODEL), BF16),
            pltpu.VMEM((D_MODEL, D_MODEL), BF16),
            pltpu.VMEM((rows, D_MODEL), BF16),
            pltpu.VMEM((rows, D_MODEL), F32),
            pltpu.VMEM((rows, D_MODEL), BF16),
            pltpu.SemaphoreType.DMA((MOE_DEPTH,)),
            pltpu.SemaphoreType.DMA((MOE_DEPTH,)),
        ],
    )
    return pl.pallas_call(
        functools.partial(_moe_kernel, n_tokens),
        grid_spec=grid_spec,
        out_shape=jax.ShapeDtypeStruct((n_tokens * TOP_K * ROW_TILE, LANES), F32),
        compiler_params=pltpu.CompilerParams(dimension_semantics=("arbitrary",), vmem_limit_bytes=VMEM_LIMIT),
        name="moe_experts",
    )(block_e, block_off, block_nv, n_used, sorted_a, h2, wg, bg, wu, bu, wd, bd)


ASSIGN_BITS = 17


def _moe_routing(top_i):
    rows = MOE_ROWS
    n_assign = top_i.shape[0] * TOP_K
    assert n_assign <= 1 << ASSIGN_BITS
    n_blocks = -(-n_assign // rows) + N_EXPERTS
    flat_e = top_i.reshape(-1)
    keys = jnp.sort(flat_e * (1 << ASSIGN_BITS) + jnp.arange(n_assign, dtype=jnp.int32))
    smem_len = -(-(n_assign + rows) // 1024) * 1024
    sorted_a = jnp.pad(keys & ((1 << ASSIGN_BITS) - 1), (0, smem_len - n_assign))
    experts = jnp.arange(N_EXPERTS, dtype=jnp.int32)
    counts = jnp.sum((flat_e[:, None] == experts[None, :]).astype(jnp.int32), axis=0)
    nblk = (counts + rows - 1) // rows
    blk_end = jnp.cumsum(nblk)
    blk_start = blk_end - nblk
    row_start = jnp.cumsum(counts) - counts
    blk = jnp.arange(n_blocks, dtype=jnp.int32)
    owner = (blk[:, None] >= blk_start[None, :]) & (blk[:, None] < blk_end[None, :])
    pick = lambda v: jnp.sum(jnp.where(owner, v[None, :], 0), axis=1)
    within = (blk - pick(blk_start)) * rows
    used = blk < blk_end[-1]
    block_e = jnp.where(used, pick(experts), N_EXPERTS - 1).astype(jnp.int32)
    block_off = jnp.where(used, pick(row_start) + within, 0).astype(jnp.int32)
    block_nv = jnp.where(used, jnp.clip(pick(counts) - within, 0, rows), 0).astype(jnp.int32)
    n_used = blk_end[-1].astype(jnp.int32).reshape(1)
    return block_e, block_off, block_nv, n_used, sorted_a.astype(jnp.int32)


def _final_kernel(x1_ref, y_ref, tw_ref, p_ref, gple_ref, wpg_ref, wpp_ref, gfin_ref, out_ref):
    tw = tw_ref[...]
    rows = tw.shape[0]
    x = x1_ref[...].reshape(rows, D_MODEL)
    for j in range(TOP_K):
        x = x + tw[:, j:j + 1] * _load_row_tiles(y_ref, (j,), rows)
    gate = _sigmoid(_dot(_rms(x, gple_ref[...]).astype(BF16), wpg_ref[...]))
    x = x + gate * _dot(p_ref[...].reshape(rows, -1).astype(BF16), wpp_ref[...])
    out_ref[...] = _rms(x, gfin_ref[...]).reshape(out_ref.shape)


def _final(x1, y4, tw, p, gple, wpg, wpp, gfin, rows, tok_offset):
    g, t, _ = x1.shape
    nt = t // rows
    off = tok_offset // rows
    assert off * rows == tok_offset and t % rows == 0
    ple = p.shape[-1]
    tok_map = lambda i: (i // nt, i % nt, 0)
    return pl.pallas_call(
        _final_kernel,
        grid=(g * nt,),
        in_specs=[
            pl.BlockSpec((1, rows, D_MODEL), tok_map),
            pl.BlockSpec((TOP_K, rows * ROW_TILE, LANES), lambda i: (0, i + off, 0)),
            pl.BlockSpec((rows, ROUTER_LANES), lambda i: (i, 0)),
            pl.BlockSpec((1, rows, ple), tok_map),
            _const_spec((1, D_MODEL)),
            _const_spec(wpg.shape),
            _const_spec(wpp.shape),
            _const_spec((1, D_MODEL)),
        ],
        out_specs=pl.BlockSpec((1, rows, D_MODEL), tok_map),
        out_shape=jax.ShapeDtypeStruct(x1.shape, F32),
        compiler_params=pltpu.CompilerParams(dimension_semantics=("arbitrary",), vmem_limit_bytes=VMEM_LIMIT),
        name="final",
    )(x1, y4, tw, p, gple, wpg, wpp, gfin)


def kernel(x_prompt, x_sample, p_prompt, p_sample, state_hgrn, state_pool, g_mix, w_in, hg_lb_logits, hg_norm, w_pool,
           pool_scale, w_out, g_ffn, w_router, b_router, w_gate, b_gate, w_up, b_up, w_down, b_down, g_ple,
           w_ple_gate, w_ple_proj, g_final):
    depth = w_in.shape[0]
    assert depth == 1, "single-layer step"
    b, t, _ = x_prompt.shape
    ns = x_sample.shape[0]
    assert x_sample.shape[1] == 1
    n_prompt = b * t
    n_tok = n_prompt + ns

    row = lambda a: a.reshape(1, -1)
    win = w_in[0].astype(BF16)
    wqft = w_in[0, :, :2 * D_MODEL].T.astype(BF16)
    lbl = hg_lb_logits.astype(F32)
    wpool = w_pool[0].astype(BF16)
    wout = w_out[0].astype(BF16)
    wr = jnp.pad(w_router[0], ((0, 0), (0, ROUTER_LANES - N_EXPERTS)))
    br = jnp.pad(b_router[0], (0, ROUTER_LANES - N_EXPERTS)).reshape(1, -1)
    shared = (row(hg_norm[0]), wpool, row(pool_scale[0]), wout, row(g_ffn[0]), wr, br)

    x1_s, h2_s, ti_s, tw_s, s_s, pool_s = _mixer_sample(
        x_sample.reshape(ns, D_MODEL), row(g_mix[0]), win, wqft, lbl.T, *shared,
        state_hgrn[0], state_pool[0], start_pos=PAST_LEN)
    x1_p, h2, ti_p, tw_p, s_p, pool_p = _mixer_prompt(x_prompt, row(g_mix[0]), win, lbl, *shared, h2_s)

    top_i = jnp.concatenate([ti_p[:, :TOP_K], ti_s[:, :TOP_K]], axis=0)
    routing = _moe_routing(top_i)
    e3 = lambda a: a[0].reshape(N_EXPERTS, 1, D_MODEL)
    y4 = _moe(h2, routing, w_gate[0], e3(b_gate), w_up[0], e3(b_up), w_down[0], e3(b_down))
    y4 = y4.reshape(TOP_K, n_tok * ROW_TILE, LANES)

    wpg = w_ple_gate[0].astype(BF16)
    wpp = w_ple_proj[0].astype(BF16)
    fin = (row(g_ple[0]), wpg, wpp, row(g_final))
    y_p = _final(x1_p, y4, tw_p, p_prompt[0], *fin, rows=FINAL_ROWS, tok_offset=0)
    y_s = _final(x1_s[None], y4, tw_s, p_sample[0].reshape(1, ns, -1), *fin, rows=ns, tok_offset=n_prompt)

    return (y_p, y_s.reshape(ns, 1, D_MODEL), s_p[None], pool_p[None], s_s[None], pool_s[None])
```

```python
import functools

import jax
import jax.numpy as jnp
from jax import lax
from jax.experimental import pallas as pl
from jax.experimental.pallas import tpu as pltpu

F32 = jnp.float32
BF16 = jnp.bfloat16

D_MODEL = 1024
HEADS = 8
HEAD_DIM = 128
CHUNK = 32
POOL_WINDOWS = (2, 4, 8, 16)
POOL_GW = D_MODEL // len(POOL_WINDOWS)
POOL_BUF = 15
POOL_CARRY = 16
N_EXPERTS = 32
TOP_K = 4
ROUTER_LANES = 128
SWIGLU_LIMIT = 7.0
SWIGLU_ALPHA = 1.702
EPS = 1e-6
PAST_LEN = 16384

SEG_Q, SEG_F, SEG_I, SEG_G, SEG_U, SEG_A, SEG_B = range(7)

MIXER_ROWS = 512
EW_LANES = 256
MOE_ROWS = 512
MOE_DEPTH = 3
FINAL_ROWS = 512
VMEM_LIMIT = 56 * 1024 * 1024


def _rms(x, g):
    ms = jnp.mean(x * x, axis=-1, keepdims=True)
    return x * lax.rsqrt(ms + EPS) * g


def _sigmoid(x):
    return 0.5 * jnp.tanh(0.5 * x) + 0.5


def _dot(a, b):
    return jnp.dot(a, b, preferred_element_type=F32)


def _dot_nt(a, b):
    return lax.dot_general(a, b, (((1,), (1,)), ((), ())), preferred_element_type=F32)


def _dot_tn(a, b):
    return lax.dot_general(a, b, (((0,), (0,)), ((), ())), preferred_element_type=F32)


LANES = 128
ROW_TILE = D_MODEL // LANES


def _store_row_tiles(ref, lead, x, period=ROW_TILE, offset=0):
    rows = x.shape[0]
    for c in range(ROW_TILE):
        ref[lead + (pl.ds(offset + c, rows, stride=period), slice(None))] = x[:, c * LANES:(c + 1) * LANES]


def _load_row_tiles(ref, lead, rows, period=ROW_TILE, offset=0):
    return jnp.concatenate(
        [ref[lead + (pl.ds(offset + c, rows, stride=period), slice(None))] for c in range(ROW_TILE)], axis=-1)


def _split_bf16(x):
    hi = x.astype(BF16)
    lo = (x - hi.astype(F32)).astype(BF16)
    return hi, lo


def _forget_lower_bound(lbl):
    m = jnp.max(lbl, axis=0, keepdims=True)
    e = jnp.exp(lbl - m)
    return e[0:1] / jnp.sum(e, axis=0, keepdims=True)


def _head_norm_gate(o, g_raw, hgn):
    parts = []
    for h in range(HEADS):
        oh = o[:, h * HEAD_DIM:(h + 1) * HEAD_DIM]
        parts.append(_rms(oh, hgn))
    return jnp.concatenate(parts, axis=-1) * (g_raw * _sigmoid(g_raw))


def _route(h_hi, h_lo, wr_ref, br_ref, ti_ref, tw_ref):
    rows = h_hi.shape[0]
    w_hi, w_lo = _split_bf16(wr_ref[...])
    logits = _dot(h_hi, w_hi) + _dot(h_lo, w_hi) + _dot(h_hi, w_lo) + br_ref[...]
    lane = lax.broadcasted_iota(jnp.int32, (rows, ROUTER_LANES), 1)
    neg = jnp.float32(-jnp.inf)
    l = jnp.where(lane < N_EXPERTS, logits, neg)
    ti = jnp.zeros((rows, ROUTER_LANES), jnp.int32)
    tw = jnp.zeros((rows, ROUTER_LANES), F32)
    m0 = None
    denom = None
    es = []
    for j in range(TOP_K):
        m = jnp.max(l, axis=-1, keepdims=True)
        idx = jnp.min(jnp.where(l == m, lane, ROUTER_LANES), axis=-1, keepdims=True)
        l = jnp.where(lane == idx, neg, l)
        if j == 0:
            m0 = m
        e = jnp.exp(m - m0)
        es.append(e)
        denom = e if denom is None else denom + e
        ti = jnp.where(lane == j, idx, ti)
    for j in range(TOP_K):
        tw = jnp.where(lane == j, es[j] / denom, tw)
    ti_ref[...] = ti
    tw_ref[...] = tw


def _mixer_prompt_kernel(nt, x_ref, gmix_ref, win_ref, lbl_ref, hgn_ref, wpool_ref, pscale_ref, wout_ref, gffn_ref,
                         wr_ref, br_ref, h2s_ref, x1_ref, h2_ref, ti_ref, tw_ref, sfin_ref, ptail_ref, *scratch):
    s = pl.program_id(0)
    n_main = pl.num_programs(0) - 1

    @pl.when(s < n_main)
    def _():
        _mixer_prompt_block(s % nt, nt - 1, x_ref.at[0], gmix_ref, win_ref, lbl_ref, hgn_ref, wpool_ref, pscale_ref,
                            wout_ref, gffn_ref, wr_ref, br_ref,
                            x1_ref.at[0], h2_ref, ti_ref, tw_ref, sfin_ref.at[0], ptail_ref.at[0], *scratch)

    @pl.when(s == n_main)
    def _():
        h2_ref[0:h2s_ref.shape[0], :] = h2s_ref[...]


def _mixer_prompt_block(t, last_t, x_ref, gmix_ref, win_ref, lbl_ref, hgn_ref, wpool_ref, pscale_ref,
                        wout_ref, gffn_ref, wr_ref, br_ref,
                        x1_ref, h2_ref, ti_ref, tw_ref, sfin_ref, ptail_ref,
                        st_ref, uext_ref, qe_ref, ke_ref, kd_ref, v_ref, o_ref, p_ref, h_ref,
                        za_ref, zb_ref, plast_ref):
    rows = MIXER_ROWS
    n_chunks = rows // CHUNK

    def proj(seg):
        return _dot(h_ref[...], win_ref[:, seg * D_MODEL:(seg + 1) * D_MODEL])

    def phase_project():
        @pl.when(t == 0)
        def _():
            st_ref[...] = jnp.zeros_like(st_ref)
            uext_ref[0:POOL_CARRY, :] = jnp.zeros((POOL_CARRY, D_MODEL), F32)

        h_ref[...] = _rms(x_ref[...], gmix_ref[...]).astype(BF16)
        o_ref[...] = proj(SEG_Q)
        p_ref[...] = proj(SEG_F)
        v_ref[...] = proj(SEG_I).astype(BF16)

    def phase_hgrn_inputs():
        lb = _forget_lower_bound(lbl_ref[...])
        row_in_chunk = lax.broadcasted_iota(jnp.int32, (CHUNK, EW_LANES), 0)
        for c in range(n_chunks):
            rs = slice(c * CHUNK, (c + 1) * CHUNK)
            for l0 in range(0, D_MODEL, EW_LANES):
                ls = slice(l0, l0 + EW_LANES)
                q_raw = o_ref[rs, ls]
                q = q_raw * _sigmoid(q_raw)
                lbt = lb[:, ls]
                f = lbt + (1.0 - lbt) * _sigmoid(p_ref[rs, ls])
                k = 1.0 - f
                p = f
                s = 1
                while s < CHUNK:
                    p = p * jnp.where(row_in_chunk >= s, pltpu.roll(p, s, axis=0), 1.0)
                    s *= 2
                plast = p[CHUNK - 1:CHUNK, :]
                qe_ref[rs, ls] = (q * p).astype(BF16)
                ke_ref[rs, ls] = (k / p).astype(BF16)
                kd_ref[rs, ls] = (k * (plast / p)).astype(BF16)
                plast_ref[c:c + 1, ls] = plast

    def phase_hgrn():
        causal = (lax.broadcasted_iota(jnp.int32, (CHUNK, CHUNK), 0)
                  >= lax.broadcasted_iota(jnp.int32, (CHUNK, CHUNK), 1))
        for hd in range(HEADS):
            cols = slice(hd * HEAD_DIM, (hd + 1) * HEAD_DIM)
            st = st_ref[hd]
            for c in range(n_chunks):
                rs = slice(c * CHUNK, (c + 1) * CHUNK)
                qe = qe_ref[rs, cols]
                ke = ke_ref[rs, cols]
                kd = kd_ref[rs, cols]
                vv = v_ref[rs, cols]
                decay = plast_ref[c:c + 1, cols]
                scores = jnp.where(causal, _dot_nt(qe, ke), 0.0)
                o_ref[rs, cols] = _dot_nt(qe, st.astype(BF16)) + _dot(scores.astype(BF16), vv)
                st = st * decay + _dot_tn(vv, kd)
            st_ref[hd] = st

        @pl.when(t == last_t)
        def _():
            for hd in range(HEADS):
                sfin_ref[hd] = st_ref[hd].T

    def phase_pool_and_gates():
        uext_ref[POOL_CARRY:POOL_CARRY + rows, :] = proj(SEG_U)
        pos1 = t * rows + lax.broadcasted_iota(jnp.int32, (rows, 1), 0) + 1
        pooled = []
        for g, w in enumerate(POOL_WINDOWS):
            cols = slice(g * POOL_GW, (g + 1) * POOL_GW)
            sw = uext_ref[:, cols]
            s = 1
            while s < w:
                sw = sw + pltpu.roll(sw, s, axis=0)
                s *= 2
            inv_cnt = 1.0 / jnp.minimum(pos1, w).astype(F32)
            dg = sw[POOL_CARRY:, :] * inv_cnt - uext_ref[POOL_CARRY:POOL_CARRY + rows, cols]
            pooled.append(_dot(dg.astype(BF16), wpool_ref[g]))

        @pl.when(t == last_t)
        def _():
            ptail_ref[...] = uext_ref[rows + POOL_CARRY - POOL_BUF:rows + POOL_CARRY, :]

        uext_ref[0:POOL_CARRY, :] = uext_ref[rows:rows + POOL_CARRY, :]
        uext_ref[POOL_CARRY:POOL_CARRY + rows, :] = jnp.concatenate(pooled, axis=-1) * pscale_ref[...]
        p_ref[...] = proj(SEG_G)
        za_ref[...] = proj(SEG_A)
        zb_ref[...] = proj(SEG_B)

    def phase_merge():
        hgn = hgn_ref[...]
        for c in range(n_chunks):
            rs = slice(c * CHUNK, (c + 1) * CHUNK)
            ys = slice(POOL_CARRY + c * CHUNK, POOL_CARRY + (c + 1) * CHUNK)
            for hd in range(HEADS):
                cols = slice(hd * HEAD_DIM, (hd + 1) * HEAD_DIM)
                g_raw = p_ref[rs, cols]
                o = _rms(o_ref[rs, cols], hgn) * (g_raw * _sigmoid(g_raw))
                merged = _sigmoid(za_ref[rs, cols]) * o + _sigmoid(zb_ref[rs, cols]) * uext_ref[ys, cols]
                h_ref[rs, cols] = merged.astype(BF16)

    def phase_out_and_route():
        x1_ref[...] = x_ref[...] + _dot(h_ref[...], wout_ref[...])
        gffn = gffn_ref[...]
        for c in range(n_chunks):
            rs = slice(c * CHUNK, (c + 1) * CHUNK)
            h2 = _rms(x1_ref[rs, :], gffn)
            _store_row_tiles(h2_ref, (), h2, offset=c * CHUNK * ROW_TILE)
            qe_ref[rs, :], ke_ref[rs, :] = _split_bf16(h2)
        _route(qe_ref[...], ke_ref[...], wr_ref, br_ref, ti_ref, tw_ref)

    for phase in (phase_project, phase_hgrn_inputs, phase_hgrn, phase_pool_and_gates, phase_merge,
                  phase_out_and_route):
        phase()


def _const_spec(shape):
    zeros = (0,) * len(shape)
    return pl.BlockSpec(shape, lambda *_: zeros, pipeline_mode=pl.Buffered(1))


def _mixer_prompt(x, gmix, win, lbl, hgn, wpool, pscale, wout, gffn, wr, br, h2_tail):
    b, t, _ = x.shape
    rows = MIXER_ROWS
    nt = t // rows
    n_main = b * nt
    assert t % rows == 0 and h2_tail.shape[0] <= rows * ROW_TILE
    main = lambda s: jnp.minimum(s, n_main - 1)
    tok_spec = pl.BlockSpec((1, rows, D_MODEL), lambda s: (main(s) // nt, main(s) % nt, 0))
    flat_spec = pl.BlockSpec((rows * ROW_TILE, LANES), lambda s: (s, 0))
    lane_spec = pl.BlockSpec((rows, ROUTER_LANES), lambda s: (main(s), 0))
    return pl.pallas_call(
        functools.partial(_mixer_prompt_kernel, nt),
        grid=(n_main + 1,),
        in_specs=[
            tok_spec,
            _const_spec((1, D_MODEL)),
            _const_spec(win.shape),
            _const_spec(lbl.shape),
            _const_spec((1, HEAD_DIM)),
            _const_spec(wpool.shape),
            _const_spec((1, D_MODEL)),
            _const_spec(wout.shape),
            _const_spec((1, D_MODEL)),
            _const_spec(wr.shape),
            _const_spec(br.shape),
            _const_spec(h2_tail.shape),
        ],
        out_specs=[
            tok_spec,
            flat_spec,
            lane_spec,
            lane_spec,
            pl.BlockSpec((1, HEADS, HEAD_DIM, HEAD_DIM), lambda s: (main(s) // nt, 0, 0, 0)),
            pl.BlockSpec((1, POOL_BUF, D_MODEL), lambda s: (main(s) // nt, 0, 0)),
        ],
        out_shape=[
            jax.ShapeDtypeStruct(x.shape, F32),
            jax.ShapeDtypeStruct((b * t * ROW_TILE + h2_tail.shape[0], LANES), F32),
            jax.ShapeDtypeStruct((b * t, ROUTER_LANES), jnp.int32),
            jax.ShapeDtypeStruct((b * t, ROUTER_LANES), F32),
            jax.ShapeDtypeStruct((b, HEADS, HEAD_DIM, HEAD_DIM), F32),
            jax.ShapeDtypeStruct((b, POOL_BUF, D_MODEL), F32),
        ],
        scratch_shapes=[
            pltpu.VMEM((HEADS, HEAD_DIM, HEAD_DIM), F32),
            pltpu.VMEM((rows + POOL_CARRY, D_MODEL), F32),
            pltpu.VMEM((rows, D_MODEL), BF16),
            pltpu.VMEM((rows, D_MODEL), BF16),
            pltpu.VMEM((rows, D_MODEL), BF16),
            pltpu.VMEM((rows, D_MODEL), BF16),
            pltpu.VMEM((rows, D_MODEL), F32),
            pltpu.VMEM((rows, D_MODEL), F32),
            pltpu.VMEM((rows, D_MODEL), BF16),
            pltpu.VMEM((rows, D_MODEL), F32),
            pltpu.VMEM((rows, D_MODEL), F32),
            pltpu.VMEM((rows // CHUNK, D_MODEL), F32),
        ],
        compiler_params=pltpu.CompilerParams(dimension_semantics=("arbitrary",), vmem_limit_bytes=VMEM_LIMIT),
        name="mixer_prompt",
    )(x, gmix, win, lbl, hgn, wpool, pscale, wout, gffn, wr, br, h2_tail)


SAMPLE_GROUP = 8


def _mixer_sample_kernel(pool_cnt, x_ref, gmix_ref, win_ref, wqft_ref, lblt_ref, hgn_ref, wpool_ref, pscale_ref,
                         wout_ref, gffn_ref, wr_ref, br_ref, s_ref, pbuf_ref,
                         x1_ref, h2_ref, ti_ref, tw_ref, snew_ref, pnew_ref,
                         h_ref, znat_ref, o_ref, bsum_ref):
    step = pl.program_id(0)
    last = pl.num_programs(0) - 1
    r0 = pl.multiple_of(step * SAMPLE_GROUP, SAMPLE_GROUP)

    @pl.when(step == 0)
    def _():
        hf = _rms(x_ref[...], gmix_ref[...])
        h_ref[...] = hf
        h = hf.astype(BF16)
        for i, seg in enumerate((SEG_I, SEG_G, SEG_U, SEG_A, SEG_B)):
            znat_ref[:, i * D_MODEL:(i + 1) * D_MODEL] = _dot(h, win_ref[:, seg * D_MODEL:(seg + 1) * D_MODEL])

    hg = h_ref[pl.ds(r0, SAMPLE_GROUP), :].astype(BF16)
    qft = _dot_nt(wqft_ref[...], hg)
    lbl = lblt_ref[...]
    m = jnp.max(lbl, axis=1, keepdims=True)
    e = jnp.exp(lbl - m)
    lb = e[:, 0:1] / jnp.sum(e, axis=1, keepdims=True)
    q_raw = qft[0:D_MODEL]
    qt = q_raw * _sigmoid(q_raw)
    ft = lb + (1.0 - lb) * _sigmoid(qft[D_MODEL:2 * D_MODEL])
    kt = 1.0 - ft

    for j in range(SAMPLE_GROUP):
        v_row = znat_ref[pl.ds(r0 + j, 1), 0:D_MODEL]
        o_parts = []
        for hd in range(HEADS):
            rs = slice(hd * HEAD_DIM, (hd + 1) * HEAD_DIM)
            s_new = ft[rs, j:j + 1] * s_ref[j, hd] + kt[rs, j:j + 1] * v_row[:, rs]
            snew_ref[j, hd] = s_new
            o_parts.append(jnp.sum(qt[rs, j:j + 1] * s_new, axis=0, keepdims=True))
        o_ref[pl.ds(r0 + j, 1), :] = jnp.concatenate(o_parts, axis=-1)

    u_g = znat_ref[pl.ds(r0, SAMPLE_GROUP), 2 * D_MODEL:3 * D_MODEL]
    sums = []
    for g, w in enumerate(POOL_WINDOWS):
        acc = jnp.zeros((SAMPLE_GROUP, POOL_GW), F32)
        for j in range(1, w):
            row = POOL_BUF - j
            acc = acc + pbuf_ref[:, row * D_MODEL + g * POOL_GW:row * D_MODEL + (g + 1) * POOL_GW]
        sums.append(acc)
    bsum_ref[pl.ds(r0, SAMPLE_GROUP), :] = jnp.concatenate(sums, axis=-1)
    pnew_ref[:, 0:(POOL_BUF - 1) * D_MODEL] = pbuf_ref[:, D_MODEL:POOL_BUF * D_MODEL]
    pnew_ref[:, (POOL_BUF - 1) * D_MODEL:POOL_BUF * D_MODEL] = u_g

    @pl.when(step == last)
    def _():
        x = x_ref[...]
        g_raw = znat_ref[:, D_MODEL:2 * D_MODEL]
        u = znat_ref[:, 2 * D_MODEL:3 * D_MODEL]
        o = _head_norm_gate(o_ref[...], g_raw, hgn_ref[...])
        sw = bsum_ref[...] + u
        pooled = []
        for g, w in enumerate(POOL_WINDOWS):
            cols = slice(g * POOL_GW, (g + 1) * POOL_GW)
            dg = sw[:, cols] * (1.0 / pool_cnt[g]) - u[:, cols]
            pooled.append(_dot(dg.astype(BF16), wpool_ref[g]))
        y_pool = jnp.concatenate(pooled, axis=-1) * pscale_ref[...]
        merged = (_sigmoid(znat_ref[:, 3 * D_MODEL:4 * D_MODEL]) * o
                  + _sigmoid(znat_ref[:, 4 * D_MODEL:5 * D_MODEL]) * y_pool)
        x1 = x + _dot(merged.astype(BF16), wout_ref[...])
        x1_ref[...] = x1
        h2 = _rms(x1, gffn_ref[...])
        _store_row_tiles(h2_ref, (), h2)
        _route(*_split_bf16(h2), wr_ref, br_ref, ti_ref, tw_ref)


def _mixer_sample(x, gmix, win, wqft, lblt, hgn, wpool, pscale, wout, gffn, wr, br, state, pbuf, start_pos):
    n = x.shape[0]
    steps = n // SAMPLE_GROUP
    pool_cnt = tuple(float(min(start_pos + 1, w)) for w in POOL_WINDOWS)
    pbuf2 = pbuf.reshape(n, POOL_BUF * D_MODEL)
    full = _const_spec
    out = pl.pallas_call(
        functools.partial(_mixer_sample_kernel, pool_cnt),
        grid=(steps,),
        in_specs=[
            full((n, D_MODEL)),
            full((1, D_MODEL)),
            full(win.shape),
            full(wqft.shape),
            full(lblt.shape),
            full((1, HEAD_DIM)),
            full(wpool.shape),
            full((1, D_MODEL)),
            full(wout.shape),
            full((1, D_MODEL)),
            full(wr.shape),
            full(br.shape),
            pl.BlockSpec((SAMPLE_GROUP, HEADS, HEAD_DIM, HEAD_DIM), lambda i: (i, 0, 0, 0)),
            pl.BlockSpec((SAMPLE_GROUP, POOL_BUF * D_MODEL), lambda i: (i, 0)),
        ],
        out_specs=[
            pl.BlockSpec((n, D_MODEL), lambda i: (0, 0)),
            pl.BlockSpec((n * ROW_TILE, LANES), lambda i: (0, 0)),
            pl.BlockSpec((n, ROUTER_LANES), lambda i: (0, 0)),
            pl.BlockSpec((n, ROUTER_LANES), lambda i: (0, 0)),
            pl.BlockSpec((SAMPLE_GROUP, HEADS, HEAD_DIM, HEAD_DIM), lambda i: (i, 0, 0, 0)),
            pl.BlockSpec((SAMPLE_GROUP, POOL_BUF * D_MODEL), lambda i: (i, 0)),
        ],
        out_shape=[
            jax.ShapeDtypeStruct((n, D_MODEL), F32),
            jax.ShapeDtypeStruct((n * ROW_TILE, LANES), F32),
            jax.ShapeDtypeStruct((n, ROUTER_LANES), jnp.int32),
            jax.ShapeDtypeStruct((n, ROUTER_LANES), F32),
            jax.ShapeDtypeStruct(state.shape, F32),
            jax.ShapeDtypeStruct(pbuf2.shape, F32),
        ],
        scratch_shapes=[
            pltpu.VMEM((n, D_MODEL), F32),
            pltpu.VMEM((n, 5 * D_MODEL), F32),
            pltpu.VMEM((n, D_MODEL), F32),
            pltpu.VMEM((n, D_MODEL), F32),
        ],
        compiler_params=pltpu.CompilerParams(dimension_semantics=("arbitrary",), vmem_limit_bytes=VMEM_LIMIT),
        name="mixer_sample",
    )(x, gmix, win, wqft, lblt, hgn, wpool, pscale, wout, gffn, wr, br, state, pbuf2)
    x1, h2, ti, tw, snew, pnew = out
    return x1, h2, ti, tw, snew, pnew.reshape(n, POOL_BUF, D_MODEL)


def _moe_kernel(n_tokens, be_ref, off_ref, nv_ref, nu_ref, sa_ref, h2_hbm,
                wg_ref, bg_ref, wu_ref, bu_ref, wd_ref, bd_ref, y_hbm,
                xbuf0, xbuf1, xbuf2, ybuf0, ybuf1, ybuf2, wgb, wub, wdb, xb_ref, gate_ref, act_ref, gsem, ssem):
    rows = MOE_ROWS
    i = pl.program_id(0)
    n_used = nu_ref[0]
    n_blocks = pl.num_programs(0)
    nv_back = lambda k: jnp.where(i >= k, nv_ref[jnp.maximum(i - k, 0)], 0)
    prev = jnp.maximum(i - 1, 0)
    nv_prev = nv_back(1)
    xbuf = (xbuf0, xbuf1, xbuf2)
    ybuf = (ybuf0, ybuf1, ybuf2)

    def tile_rows(first, n=1):
        start = first * ROW_TILE
        return pl.ds(start if isinstance(first, int) else pl.multiple_of(start, ROW_TILE), n * ROW_TILE)

    def gather_copy(a, r, s):
        tok = a >> 2
        return pltpu.make_async_copy(h2_hbm.at[tile_rows(tok), :], xbuf[s].at[tile_rows(r), :], gsem.at[s])

    def scatter_copy(a, r, n, s):
        dst = (a & (TOP_K - 1)) * n_tokens + (a >> 2)
        return pltpu.make_async_copy(ybuf[s].at[tile_rows(r, n), :], y_hbm.at[tile_rows(dst, n), :], ssem.at[s])

    def wait_gather(s):
        pltpu.make_async_copy(h2_hbm.at[tile_rows(0, rows), :], xbuf[s], gsem.at[s]).wait()

    def wait_scatter(n, s):
        size = rows
        while size >= 1:
            @pl.when((n & size) != 0)
            def _():
                scatter_copy(0, 0, size, s).wait()
            size //= 2

    def scatter_loop(blk, n, s):
        def body(r, c):
            scatter_copy(sa_ref[off_ref[blk] + r], r, 1, s).start()
            return c
        lax.fori_loop(0, n, body, 0)

    n_phases = 3

    def start_copies(phase, cur, full_prev):
        ahead, behind = (cur + 2) % MOE_DEPTH, (cur + MOE_DEPTH - 1) % MOE_DEPTH
        off_ahead = off_ref[jnp.minimum(i + 2, n_blocks - 1)]
        off_prev = off_ref[prev]
        for r in range(rows * phase // n_phases, rows * (phase + 1) // n_phases):
            gather_copy(sa_ref[off_ahead + r], r, ahead).start()
            if full_prev:
                scatter_copy(sa_ref[off_prev + r], r, 1, behind).start()

    def phase_load(cur, full_prev):
        wait_gather(cur)
        xb_ref[...] = _load_row_tiles(xbuf[cur], (), rows).astype(BF16)

    def phase_gate(cur, full_prev):
        start_copies(0, cur, full_prev)
        gate_ref[...] = jnp.minimum(_dot(xb_ref[...], wgb[...]) + bg_ref[0], SWIGLU_LIMIT)

    def phase_up(cur, full_prev):
        start_copies(1, cur, full_prev)
        up = jnp.clip(_dot(xb_ref[...], wub[...]) + bu_ref[0], -SWIGLU_LIMIT, SWIGLU_LIMIT)
        gate = gate_ref[...]
        act_ref[...] = ((up + 1.0) * gate * _sigmoid(SWIGLU_ALPHA * gate)).astype(BF16)

    def phase_down(cur, full_prev):
        nxt, behind = (cur + 1) % MOE_DEPTH, (cur + MOE_DEPTH - 1) % MOE_DEPTH
        start_copies(2, cur, full_prev)
        wait_scatter(nv_back(MOE_DEPTH), cur)
        _store_row_tiles(ybuf[cur], (), _dot(act_ref[...], wdb[...]) + bd_ref[0])
        if not full_prev:
            scatter_loop(prev, nv_prev, behind)

        @pl.when(i == n_used - 1)
        def _():
            scatter_loop(i, nv_ref[i], cur)
            wait_scatter(nv_back(2), nxt)
            wait_scatter(nv_prev, behind)
            wait_scatter(nv_ref[i], cur)
            wait_gather(nxt)
            wait_gather((cur + 2) % MOE_DEPTH)

    @pl.when(i == 0)
    def _():
        for blk in range(MOE_DEPTH - 1):
            def body(r, c):
                gather_copy(sa_ref[off_ref[blk] + r], r, blk).start()
                return c
            lax.fori_loop(0, rows, body, 0)

    @pl.when(i < n_used)
    def _():
        first_of_expert = jnp.logical_or(i == 0, be_ref[i] != be_ref[prev])

        @pl.when(first_of_expert)
        def _():
            wgb[...] = wg_ref[0].astype(BF16)
            wub[...] = wu_ref[0].astype(BF16)
            wdb[...] = wd_ref[0].astype(BF16)

        prev_is_full = nv_prev == rows
        for phase in (phase_load, phase_gate, phase_up, phase_down):
            for cur in range(MOE_DEPTH):
                for full_prev in (True, False):
                    full_cond = prev_is_full if full_prev else jnp.logical_not(prev_is_full)

                    @pl.when(jnp.logical_and(i % MOE_DEPTH == cur, full_cond))
                    def _():
                        phase(cur, full_prev)


def _moe(h2, routing, wg, bg, wu, bu, wd, bd):
    rows = MOE_ROWS
    n_tokens = h2.shape[0] // ROW_TILE
    block_e, block_off, block_nv, n_used, sorted_a = routing
    n_blocks = block_e.shape[0]
    w_spec = pl.BlockSpec((1, D_MODEL, D_MODEL), lambda i, be, *_: (be[i], 0, 0))
    b_spec = pl.BlockSpec((1, 1, D_MODEL), lambda i, be, *_: (be[i], 0, 0))
    grid_spec = pltpu.PrefetchScalarGridSpec(
        num_scalar_prefetch=5,
        grid=(n_blocks,),
        in_specs=[pl.BlockSpec(memory_space=pl.ANY), w_spec, b_spec, w_spec, b_spec, w_spec, b_spec],
        out_specs=pl.BlockSpec(memory_space=pl.ANY),
        scratch_shapes=[
            *[pltpu.VMEM((rows * ROW_TILE, LANES), F32)] * (2 * MOE_DEPTH),
            pltpu.VMEM((D_MODEL, D_MODEL), BF16),
            pltpu.VMEM((D_MODEL, D_MODEL), BF16),
            pltpu.VMEM((D_MODEL, D_MODEL), BF16),
            pltpu.VMEM((rows, D_MODEL), BF16),
            pltpu.VMEM((rows, D_MODEL), F32),
            pltpu.VMEM((rows, D_MODEL), BF16),
            pltpu.SemaphoreType.DMA((MOE_DEPTH,)),
            pltpu.SemaphoreType.DMA((MOE_DEPTH,)),
        ],
    )
    return pl.pallas_call(
        functools.partial(_moe_kernel, n_tokens),
        grid_spec=grid_spec,
        out_shape=jax.ShapeDtypeStruct((n_tokens * TOP_K * ROW_TILE, LANES), F32),
        compiler_params=pltpu.CompilerParams(dimension_semantics=("arbitrary",), vmem_limit_bytes=VMEM_LIMIT),
        name="moe_experts",
    )(block_e, block_off, block_nv, n_used, sorted_a, h2, wg, bg, wu, bu, wd, bd)


ASSIGN_BITS = 17


def _moe_routing(top_i):
    rows = MOE_ROWS
    n_assign = top_i.shape[0] * TOP_K
    assert n_assign <= 1 << ASSIGN_BITS
    n_blocks = -(-n_assign // rows) + N_EXPERTS
    flat_e = top_i.reshape(-1)
    keys = lax.sort(flat_e * (1 << ASSIGN_BITS) + jnp.arange(n_assign, dtype=jnp.int32), is_stable=False)
    smem_len = -(-(n_assign + rows) // 1024) * 1024
    sorted_a = jnp.pad(keys & ((1 << ASSIGN_BITS) - 1), (0, smem_len - n_assign))
    experts = jnp.arange(N_EXPERTS, dtype=jnp.int32)
    counts = jnp.sum((flat_e[:, None] == experts[None, :]).astype(jnp.int32), axis=0)
    nblk = (counts + rows - 1) // rows
    blk_end = jnp.cumsum(nblk)
    blk_start = blk_end - nblk
    row_start = jnp.cumsum(counts) - counts
    blk = jnp.arange(n_blocks, dtype=jnp.int32)
    owner = (blk[:, None] >= blk_start[None, :]) & (blk[:, None] < blk_end[None, :])
    pick = lambda v: jnp.sum(jnp.where(owner, v[None, :], 0), axis=1)
    within = (blk - pick(blk_start)) * rows
    used = blk < blk_end[-1]
    block_e = jnp.where(used, pick(experts), N_EXPERTS - 1).astype(jnp.int32)
    block_off = jnp.where(used, pick(row_start) + within, 0).astype(jnp.int32)
    block_nv = jnp.where(used, jnp.clip(pick(counts) - within, 0, rows), 0).astype(jnp.int32)
    n_used = blk_end[-1].astype(jnp.int32).reshape(1)
    return block_e, block_off, block_nv, n_used, sorted_a.astype(jnp.int32)


def _final_kernel(x1_ref, y_ref, tw_ref, p_ref, gple_ref, wpg_ref, wpp_ref, gfin_ref, out_ref):
    tw = tw_ref[...]
    rows = tw.shape[0]
    x = x1_ref[...].reshape(rows, D_MODEL)
    for j in range(TOP_K):
        x = x + tw[:, j:j + 1] * _load_row_tiles(y_ref, (j,), rows)
    gate = _sigmoid(_dot(_rms(x, gple_ref[...]).astype(BF16), wpg_ref[...]))
    x = x + gate * _dot(p_ref[...].reshape(rows, -1).astype(BF16), wpp_ref[...])
    out_ref[...] = _rms(x, gfin_ref[...]).reshape(out_ref.shape)


def _final(x1, y4, tw, p, gple, wpg, wpp, gfin, rows, tok_offset):
    g, t, _ = x1.shape
    nt = t // rows
    off = tok_offset // rows
    assert off * rows == tok_offset and t % rows == 0
    ple = p.shape[-1]
    tok_map = lambda i: (i // nt, i % nt, 0)
    return pl.pallas_call(
        _final_kernel,
        grid=(g * nt,),
        in_specs=[
            pl.BlockSpec((1, rows, D_MODEL), tok_map),
            pl.BlockSpec((TOP_K, rows * ROW_TILE, LANES), lambda i: (0, i + off, 0)),
            pl.BlockSpec((rows, ROUTER_LANES), lambda i: (i, 0)),
            pl.BlockSpec((1, rows, ple), tok_map),
            _const_spec((1, D_MODEL)),
            _const_spec(wpg.shape),
            _const_spec(wpp.shape),
            _const_spec((1, D_MODEL)),
        ],
        out_specs=pl.BlockSpec((1, rows, D_MODEL), tok_map),
        out_shape=jax.ShapeDtypeStruct(x1.shape, F32),
        compiler_params=pltpu.CompilerParams(dimension_semantics=("arbitrary",), vmem_limit_bytes=VMEM_LIMIT),
        name="final",
    )(x1, y4, tw, p, gple, wpg, wpp, gfin)


def kernel(x_prompt, x_sample, p_prompt, p_sample, state_hgrn, state_pool, g_mix, w_in, hg_lb_logits, hg_norm, w_pool,
           pool_scale, w_out, g_ffn, w_router, b_router, w_gate, b_gate, w_up, b_up, w_down, b_down, g_ple,
           w_ple_gate, w_ple_proj, g_final):
    depth = w_in.shape[0]
    assert depth == 1, "single-layer step"
    b, t, _ = x_prompt.shape
    ns = x_sample.shape[0]
    assert x_sample.shape[1] == 1
    n_prompt = b * t
    n_tok = n_prompt + ns

    row = lambda a: a.reshape(1, -1)
    win = w_in[0].astype(BF16)
    wqft = w_in[0, :, :2 * D_MODEL].T.astype(BF16)
    lbl = hg_lb_logits.astype(F32)
    wpool = w_pool[0].astype(BF16)
    wout = w_out[0].astype(BF16)
    wr = jnp.pad(w_router[0], ((0, 0), (0, ROUTER_LANES - N_EXPERTS)))
    br = jnp.pad(b_router[0], (0, ROUTER_LANES - N_EXPERTS)).reshape(1, -1)
    shared = (row(hg_norm[0]), wpool, row(pool_scale[0]), wout, row(g_ffn[0]), wr, br)

    x1_s, h2_s, ti_s, tw_s, s_s, pool_s = _mixer_sample(
        x_sample.reshape(ns, D_MODEL), row(g_mix[0]), win, wqft, lbl.T, *shared,
        state_hgrn[0], state_pool[0], start_pos=PAST_LEN)
    x1_p, h2, ti_p, tw_p, s_p, pool_p = _mixer_prompt(x_prompt, row(g_mix[0]), win, lbl, *shared, h2_s)

    top_i = jnp.concatenate([ti_p[:, :TOP_K], ti_s[:, :TOP_K]], axis=0)
    routing = _moe_routing(top_i)
    e3 = lambda a: a[0].reshape(N_EXPERTS, 1, D_MODEL)
    y4 = _moe(h2, routing, w_gate[0], e3(b_gate), w_up[0], e3(b_up), w_down[0], e3(b_down))
    y4 = y4.reshape(TOP_K, n_tok * ROW_TILE, LANES)

    wpg = w_ple_gate[0].astype(BF16)
    wpp = w_ple_proj[0].astype(BF16)
    fin = (row(g_ple[0]), wpg, wpp, row(g_final))
    y_p = _final(x1_p, y4, tw_p, p_prompt[0], *fin, rows=FINAL_ROWS, tok_offset=0)
    y_s = _final(x1_s[None], y4, tw_s, p_sample[0].reshape(1, ns, -1), *fin, rows=ns, tok_offset=n_prompt)

    return (y_p, y_s.reshape(ns, 1, D_MODEL), s_p[None], pool_p[None], s_s[None], pool_s[None])
```
